```python
import math
import jax, jax.numpy as jnp
from jax import lax
import numpy as np

D_MODEL = 2048
BATCH = 4
SEQ = 8192
DEPTH = 4
DEC_BATCH = 32
DEC_SEQ = 64
PAST_LEN = 4096

CHUNK = 64
N_META = 16
N_MIXERS = 2
N_CONV_LAYERS = (DEPTH + 1) // 2
N_SSD_LAYERS = DEPTH // 2
SC_WIDTH = 3
SSD_INNER = 2 * D_MODEL
SSD_HEADDIM = 64
SSD_HEADS = SSD_INNER // SSD_HEADDIM
SSD_GROUPS = 8
SSD_STATE = 128
SSD_CONV_WIDTH = 4
SSD_CONV_DIM = SSD_INNER + 2 * SSD_GROUPS * SSD_STATE
SSD_BLOCK = 64
D_FF = 5632
FFN_CONV_WIDTH = 3
EPS = 1e-6

kernel_name = 'hybrid_shortconv_ssd_convffn_stream_step'


def rms_norm(x, w):
    xf = x.astype(jnp.float32)
    y = xf * lax.rsqrt(jnp.mean(xf * xf, axis=-1, keepdims=True) + EPS)
    return (y * w.astype(jnp.float32)).astype(x.dtype)


def causal_dwconv(u, buf, w):
    width = w.shape[0]
    L = u.shape[1]
    full = jnp.concatenate([buf.astype(u.dtype), u], axis=1)
    y = sum(w[k] * full[:, k:k + L] for k in range(width))
    return y, full[:, L:]


def pad_seq(t, pad):
    return jnp.pad(t, [(0, 0), (0, pad)] + [(0, 0)] * (t.ndim - 2))


def short_conv_mixer(h, buf, w_in, conv_w, w_out):
    b_gate, c_gate, v = jnp.split(h @ w_in, 3, axis=-1)
    y, new_buf = causal_dwconv(c_gate * v, buf, conv_w)
    return (b_gate * y) @ w_out, new_buf


def ssd_scan(x, dt, A, Bm, Cm, state0):
    b, L = x.shape[:2]
    nc = L // SSD_BLOCK
    hpg = SSD_HEADS // SSD_GROUPS

    def blocks(t):
        return jnp.moveaxis(t.reshape((b, nc, SSD_BLOCK) + t.shape[2:]), 1, 0)

    xs = blocks(x.reshape(b, L, SSD_GROUPS, hpg, SSD_HEADDIM))
    dts = blocks(dt.reshape(b, L, SSD_GROUPS, hpg))
    Bs = blocks(Bm)
    Cs = blocks(Cm)
    Ag = A.reshape(SSD_GROUPS, hpg)
    mask = jnp.tril(jnp.ones((SSD_BLOCK, SSD_BLOCK), dtype=bool))[None, :, :, None, None]

    def step(state, blk):
        xb, dtb, Bb, Cb = blk
        cs = jnp.cumsum(dtb * Ag, axis=1)
        seg = cs[:, :, None] - cs[:, None, :]
        decay = jnp.exp(jnp.where(mask, seg, -jnp.inf))
        w_ij = decay * dtb[:, None]
        cb = jnp.einsum('bign,bjgn->bijg', Cb, Bb)
        y_intra = jnp.einsum('bijg,bijgh,bjghp->bighp', cb, w_ij, xb)
        y_inter = jnp.einsum('bign,bghpn->bighp', Cb, state) * jnp.exp(cs)[..., None]
        last = cs[:, -1]
        w_tail = jnp.exp(last[:, None] - cs) * dtb
        new_state = state * jnp.exp(last)[..., None, None] + jnp.einsum(
            'bjgn,bjgh,bjghp->bghpn', Bb, w_tail, xb)
        return new_state, y_intra + y_inter

    state_g = state0.astype(jnp.float32).reshape(b, SSD_GROUPS, hpg, SSD_HEADDIM, SSD_STATE)
    final, ys = lax.scan(step, state_g, (xs, dts, Bs, Cs))
    y = jnp.moveaxis(ys, 0, 1).reshape(b, L, SSD_HEADS, SSD_HEADDIM)
    return y, final.reshape(b, SSD_HEADS, SSD_HEADDIM, SSD_STATE)


def ssd_mixer(h, conv_buf, ssm_state, w_in, conv_w, conv_b, dt_bias, a_log, d_skip, norm_w, w_out):
    b, L = h.shape[:2]
    f32 = jnp.float32
    z, xbc, dt = jnp.split(h @ w_in, [SSD_INNER, SSD_INNER + SSD_CONV_DIM], axis=-1)
    xbc, new_conv = causal_dwconv(xbc, conv_buf, conv_w)
    xbc = jax.nn.silu(xbc + conv_b)
    xs, Bm, Cm = jnp.split(xbc, [SSD_INNER, SSD_INNER + SSD_GROUPS * SSD_STATE], axis=-1)
    xs = xs.astype(f32).reshape(b, L, SSD_HEADS, SSD_HEADDIM)
    Bm = Bm.astype(f32).reshape(b, L, SSD_GROUPS, SSD_STATE)
    Cm = Cm.astype(f32).reshape(b, L, SSD_GROUPS, SSD_STATE)
    dt = jax.nn.softplus(dt.astype(f32) + dt_bias.astype(f32))
    A = -jnp.exp(a_log.astype(f32))
    pad = (-L) % SSD_BLOCK
    y, new_state = ssd_scan(pad_seq(xs, pad), pad_seq(dt, pad), A,
                            pad_seq(Bm, pad), pad_seq(Cm, pad), ssm_state)
    y = y[:, :L] + d_skip.astype(f32)[:, None] * xs
    g = y.reshape(b, L, SSD_INNER) * jax.nn.silu(z.astype(f32))
    g = g.reshape(b, L, SSD_GROUPS, SSD_INNER // SSD_GROUPS)
    g = g * lax.rsqrt(jnp.mean(g * g, axis=-1, keepdims=True) + EPS)
    g = (g.reshape(b, L, SSD_INNER) * norm_w.astype(f32)).astype(h.dtype)
    return g @ w_out, new_conv, new_state


def conv_ffn(h, buf, w_up, conv_w, conv_b, w_down):
    u, new_buf = causal_dwconv(h @ w_up, buf, conv_w)
    a, v = jnp.split(u + conv_b, 2, axis=-1)
    return (jax.nn.silu(a) * v) @ w_down, new_buf


def trunk(x, conv_a_buf, ssd_conv_buf, ssd_state, ffn_buf,
          norm_mix, norm_ffn, norm_final, sc_w_in, sc_conv_w, sc_w_out,
          ssd_w_in, ssd_conv_w, ssd_conv_b, ssd_dt_bias, ssd_a_log, ssd_d, ssd_norm_w, ssd_w_out,
          ffn_w_up, ffn_conv_w, ffn_conv_b, ffn_w_down):
    new_conv_a, new_ssd_conv, new_ssd, new_ffn = [], [], [], []
    for i in range(DEPTH):
        h = rms_norm(x, norm_mix[i])
        j = i // N_MIXERS
        if i % N_MIXERS == 0:
            out, nb = short_conv_mixer(h, conv_a_buf[j], sc_w_in[j], sc_conv_w[j], sc_w_out[j])
            new_conv_a.append(nb)
        else:
            out, nc, ns = ssd_mixer(h, ssd_conv_buf[j], ssd_state[j], ssd_w_in[j], ssd_conv_w[j],
                                    ssd_conv_b[j], ssd_dt_bias[j], ssd_a_log[j], ssd_d[j],
                                    ssd_norm_w[j], ssd_w_out[j])
            new_ssd_conv.append(nc)
            new_ssd.append(ns)
        x = x + out
        h = rms_norm(x, norm_ffn[i])
        out, nf = conv_ffn(h, ffn_buf[i], ffn_w_up[i], ffn_conv_w[i], ffn_conv_b[i], ffn_w_down[i])
        new_ffn.append(nf)
        x = x + out
    x = rms_norm(x, norm_final)
    return (x, jnp.stack(new_conv_a), jnp.stack(new_ssd_conv), jnp.stack(new_ssd), jnp.stack(new_ffn))


def setup_inputs(seed: int = 0) -> dict:
    key = jax.random.key(seed)
    ks = jax.random.split(key, 32)
    f32 = jnp.float32

    def nrm(k, shape, scale):
        return jax.random.normal(k, shape, f32) * scale

    res = (2 * DEPTH) ** -0.5
    dt0 = jnp.exp(jax.random.uniform(ks[20], (N_SSD_LAYERS, SSD_HEADS), f32,
                                     math.log(1e-3), math.log(1e-1)))
    return {
        'x_prompt': nrm(ks[0], (BATCH, SEQ, D_MODEL), 1.0),
        'x_sample': nrm(ks[1], (DEC_BATCH, DEC_SEQ, D_MODEL), 1.0),
        'state_conv_a': nrm(ks[2], (N_CONV_LAYERS, DEC_BATCH, SC_WIDTH - 1, D_MODEL), 1.0),
        'state_ssd_conv': nrm(ks[3], (N_SSD_LAYERS, DEC_BATCH, SSD_CONV_WIDTH - 1, SSD_CONV_DIM), 1.0),
        'state_ssd': nrm(ks[4], (N_SSD_LAYERS, DEC_BATCH, SSD_HEADS, SSD_HEADDIM, SSD_STATE), 0.1),
        'state_ffn_conv': nrm(ks[5], (DEPTH, DEC_BATCH, FFN_CONV_WIDTH - 1, 2 * D_FF), 1.0),
        'meta_tokens': nrm(ks[6], (N_META, D_MODEL), 1.0),
        'norm_mix': 1.0 + nrm(ks[7], (DEPTH, D_MODEL), 0.02),
        'norm_ffn': 1.0 + nrm(ks[8], (DEPTH, D_MODEL), 0.02),
        'norm_final': 1.0 + nrm(ks[9], (D_MODEL,), 0.02),
        'sc_w_in': nrm(ks[10], (N_CONV_LAYERS, D_MODEL, 3 * D_MODEL), D_MODEL ** -0.5),
        'sc_conv_w': nrm(ks[11], (N_CONV_LAYERS, SC_WIDTH, D_MODEL), SC_WIDTH ** -0.5),
        'sc_w_out': nrm(ks[12], (N_CONV_LAYERS, D_MODEL, D_MODEL), res * D_MODEL ** -0.5),
        'ssd_w_in': nrm(ks[13], (N_SSD_LAYERS, D_MODEL, 2 * SSD_INNER + 2 * SSD_GROUPS * SSD_STATE + SSD_HEADS),
                        D_MODEL ** -0.5),
        'ssd_conv_w': nrm(ks[14], (N_SSD_LAYERS, SSD_CONV_WIDTH, SSD_CONV_DIM), SSD_CONV_WIDTH ** -0.5),
        'ssd_conv_b': nrm(ks[15], (N_SSD_LAYERS, SSD_CONV_DIM), 0.02),
        'ssd_dt_bias': dt0 + jnp.log(-jnp.expm1(-dt0)),
        'ssd_a_log': jnp.log(jax.random.uniform(ks[16], (N_SSD_LAYERS, SSD_HEADS), f32, 1.0, 16.0)),
        'ssd_d': 1.0 + nrm(ks[17], (N_SSD_LAYERS, SSD_HEADS), 0.02),
        'ssd_norm_w': 1.0 + nrm(ks[18], (N_SSD_LAYERS, SSD_INNER), 0.02),
        'ssd_w_out': nrm(ks[19], (N_SSD_LAYERS, SSD_INNER, D_MODEL), res * SSD_INNER ** -0.5),
        'ffn_w_up': nrm(ks[21], (DEPTH, D_MODEL, 2 * D_FF), D_MODEL ** -0.5),
        'ffn_conv_w': nrm(ks[22], (DEPTH, FFN_CONV_WIDTH, 2 * D_FF), FFN_CONV_WIDTH ** -0.5),
        'ffn_conv_b': nrm(ks[23], (DEPTH, 2 * D_FF), 0.02),
        'ffn_w_down': nrm(ks[24], (DEPTH, D_FF, D_MODEL), res * D_FF ** -0.5),
    }


def reference(x_prompt, x_sample, state_conv_a, state_ssd_conv, state_ssd, state_ffn_conv,
              meta_tokens, norm_mix, norm_ffn, norm_final, sc_w_in, sc_conv_w, sc_w_out,
              ssd_w_in, ssd_conv_w, ssd_conv_b, ssd_dt_bias, ssd_a_log, ssd_d, ssd_norm_w, ssd_w_out,
              ffn_w_up, ffn_conv_w, ffn_conv_b, ffn_w_down):
    b = x_prompt.shape[0]
    dtype = x_prompt.dtype
    meta = jnp.broadcast_to(meta_tokens.astype(dtype)[None], (b, N_META, D_MODEL))
    xp = jnp.concatenate([meta, x_prompt], axis=1)
    zero_conv_a = jnp.zeros((N_CONV_LAYERS, b, SC_WIDTH - 1, D_MODEL), dtype)
    zero_ssd_conv = jnp.zeros((N_SSD_LAYERS, b, SSD_CONV_WIDTH - 1, SSD_CONV_DIM), dtype)
    zero_ssd = jnp.zeros((N_SSD_LAYERS, b, SSD_HEADS, SSD_HEADDIM, SSD_STATE), jnp.float32)
    zero_ffn = jnp.zeros((DEPTH, b, FFN_CONV_WIDTH - 1, 2 * D_FF), dtype)
    yp, p_conv_a, p_ssd_conv, p_ssd, p_ffn_conv = trunk(
        xp, zero_conv_a, zero_ssd_conv, zero_ssd, zero_ffn,
        norm_mix, norm_ffn, norm_final, sc_w_in, sc_conv_w, sc_w_out,
        ssd_w_in, ssd_conv_w, ssd_conv_b, ssd_dt_bias, ssd_a_log, ssd_d, ssd_norm_w, ssd_w_out,
        ffn_w_up, ffn_conv_w, ffn_conv_b, ffn_w_down)
    y_prompt = yp[:, N_META:]
    y_sample, s_conv_a, s_ssd_conv, s_ssd, s_ffn_conv = trunk(
        x_sample, state_conv_a, state_ssd_conv, state_ssd, state_ffn_conv,
        norm_mix, norm_ffn, norm_final, sc_w_in, sc_conv_w, sc_w_out,
        ssd_w_in, ssd_conv_w, ssd_conv_b, ssd_dt_bias, ssd_a_log, ssd_d, ssd_norm_w, ssd_w_out,
        ffn_w_up, ffn_conv_w, ffn_conv_b, ffn_w_down)
    return (y_prompt, y_sample, p_conv_a, p_ssd_conv, p_ssd, p_ffn_conv,
            s_conv_a, s_ssd_conv, s_ssd, s_ffn_conv)
```

```python
import functools

import jax
import jax.numpy as jnp
from jax import lax
from jax.experimental import pallas as pl
from jax.experimental.pallas import tpu as pltpu

EPS = 1e-6
F32 = jnp.float32
BF16 = jnp.bfloat16

SUBLANES = 8
VMEM_LIMIT_BYTES = 56 * 1024 * 1024
ROW_TILE = 512
COL_TILE = 512


def _dot(a, b):
    return jnp.dot(a, b, preferred_element_type=F32)


def _params(*sem):
    return pltpu.CompilerParams(dimension_semantics=sem, vmem_limit_bytes=VMEM_LIMIT_BYTES)


def _rms(x, w):
    return x * lax.rsqrt(jnp.mean(x * x, axis=-1, keepdims=True) + EPS) * w


def _silu(x):
    return x / (1.0 + jnp.exp(-x))


def _softplus(x):
    return jnp.maximum(x, 0.0) + jnp.log1p(jnp.exp(-jnp.abs(x)))


def _delay_rows(u, halo, d):
    rolled = pltpu.roll(u, d, 0)
    row = lax.broadcasted_iota(jnp.int32, (SUBLANES, u.shape[1]), 0)
    head = jnp.where(row < d, pltpu.roll(halo, d, 0), rolled[0:SUBLANES])
    if u.shape[0] == SUBLANES:
        return head
    return jnp.concatenate([head, rolled[SUBLANES:]], axis=0)


def _causal_conv(u, halo, w):
    width = w.shape[0]
    y = u * w[width - 1:width]
    for d in range(1, width):
        y = y + _delay_rows(u, halo, d) * w[width - 1 - d:width - d]
    return y


def _segment_conv(u, st_ref, carry_ref, nb_ref, w, *, n_sub, seg, tps):
    i, j = pl.program_id(0), pl.program_id(1)
    carry = None
    if tps > 1:
        @pl.when(i == 0)
        def _():
            carry_ref[j] = jnp.zeros(carry_ref.shape[1:], F32)

        carry = carry_ref[j]
    ys = []
    for s in range(n_sub):
        us = u[s * seg:(s + 1) * seg]
        halo = st_ref[s]
        if carry is not None:
            halo = jnp.where((i % tps) == 0, halo, carry)
        ys.append(_causal_conv(us, halo, w))
        nb_ref[s, j] = us[seg - SUBLANES:seg]
    if tps > 1:
        carry_ref[j] = u[u.shape[0] - SUBLANES:]
    return ys[0] if n_sub == 1 else jnp.concatenate(ys, axis=0)


def _sc_kernel(x_ref, nw_ref, wb_ref, wc_ref, wv_ref, cw_ref, st_ref, w2_ref, o_ref, nb_ref,
               h_ref, *carry_refs, n_sub, seg, tps):
    @pl.when(pl.program_id(1) == 0)
    def _():
        x = x_ref[...]
        h_ref[...] = _rms(x, nw_ref[...]).astype(BF16)
        o_ref[...] = x

    h = h_ref[...]
    u = _dot(h, wc_ref[...]) * _dot(h, wv_ref[...])
    y = _segment_conv(u, st_ref, carry_refs[0] if tps > 1 else None, nb_ref, cw_ref[...],
                      n_sub=n_sub, seg=seg, tps=tps)
    g = (_dot(h, wb_ref[...]) * y).astype(BF16)
    o_ref[...] += _dot(g, w2_ref[...])


def _sc_layer(x, st8, norm_w, w_in, conv_w, w_out, *, n_seq, seg):
    rows, d = x.shape
    tm, n_sub, tps = _row_tiling(n_seq, seg)
    tn = _col_tile(d)
    nj = d // tn
    grid = (rows // tm, nj)
    kern = functools.partial(_sc_kernel, n_sub=n_sub, seg=min(seg, tm), tps=tps)
    scratch = [pltpu.VMEM((tm, d), BF16)]
    if tps > 1:
        scratch.append(pltpu.VMEM((nj, SUBLANES, tn), F32))
    out, nb = pl.pallas_call(
        kern,
        grid=grid,
        in_specs=[
            pl.BlockSpec((tm, d), lambda i, j: (i, 0)),
            pl.BlockSpec((1, d), lambda i, j: (0, 0)),
            pl.BlockSpec((d, tn), lambda i, j: (0, j)),
            pl.BlockSpec((d, tn), lambda i, j: (0, nj + j)),
            pl.BlockSpec((d, tn), lambda i, j: (0, 2 * nj + j)),
            pl.BlockSpec((conv_w.shape[0], tn), lambda i, j: (0, j)),
            pl.BlockSpec((n_sub, SUBLANES, tn), lambda i, j: (i // tps, 0, j)),
            pl.BlockSpec((tn, d), lambda i, j: (j, 0)),
        ],
        out_specs=[
            pl.BlockSpec((tm, d), lambda i, j: (i, 0)),
            _nb_spec(n_sub, nj, tn, tps),
        ],
        out_shape=[jax.ShapeDtypeStruct((rows, d), F32), _nb_shape(n_seq, nj, tn)],
        scratch_shapes=scratch,
        compiler_params=_params("arbitrary", "arbitrary"),
        name="short_conv_mixer",
    )(x, norm_w, w_in, w_in, w_in, conv_w, st8, w_out)
    return out, _nb_merge(nb)


def _ffn_kernel(x_ref, nw_ref, wa_ref, wv_ref, cwa_ref, cwv_ref, ba_ref, bv_ref, sta_ref, stv_ref, w2_ref,
                fw_ref, o_ref, nba_ref, nbv_ref, h_ref, *carry_refs, n_sub, seg, tps, final_norm):
    j = pl.program_id(1)

    @pl.when(j == 0)
    def _():
        x = x_ref[...]
        h_ref[...] = _rms(x, nw_ref[...]).astype(BF16)
        o_ref[...] = x

    h = h_ref[...]
    ca, cv = carry_refs if tps > 1 else (None, None)
    a = _segment_conv(_dot(h, wa_ref[...]), sta_ref, ca, nba_ref, cwa_ref[...],
                      n_sub=n_sub, seg=seg, tps=tps) + ba_ref[...]
    v = _segment_conv(_dot(h, wv_ref[...]), stv_ref, cv, nbv_ref, cwv_ref[...],
                      n_sub=n_sub, seg=seg, tps=tps) + bv_ref[...]
    g = (_silu(a) * v).astype(BF16)
    o_ref[...] += _dot(g, w2_ref[...])

    if final_norm:
        @pl.when(j == pl.num_programs(1) - 1)
        def _():
            o_ref[...] = _rms(o_ref[...], fw_ref[...])


def _ffn_layer(x, st8, norm_w, w_up, conv_w, conv_b, w_down, final_w, *, n_seq, seg, final_norm):
    rows, d = x.shape
    dff = w_down.shape[0]
    tm, n_sub, tps = _row_tiling(n_seq, seg)
    tn = _col_tile(dff)
    nj = dff // tn
    grid = (rows // tm, nj)
    kern = functools.partial(_ffn_kernel, n_sub=n_sub, seg=min(seg, tm), tps=tps, final_norm=final_norm)
    scratch = [pltpu.VMEM((tm, d), BF16)]
    if tps > 1:
        scratch += [pltpu.VMEM((nj, SUBLANES, tn), F32), pltpu.VMEM((nj, SUBLANES, tn), F32)]
    width = conv_w.shape[0]
    st_spec_a = pl.BlockSpec((n_sub, SUBLANES, tn), lambda i, j: (i // tps, 0, j))
    st_spec_v = pl.BlockSpec((n_sub, SUBLANES, tn), lambda i, j: (i // tps, 0, nj + j))
    out, nba, nbv = pl.pallas_call(
        kern,
        grid=grid,
        in_specs=[
            pl.BlockSpec((tm, d), lambda i, j: (i, 0)),
            pl.BlockSpec((1, d), lambda i, j: (0, 0)),
            pl.BlockSpec((d, tn), lambda i, j: (0, j)),
            pl.BlockSpec((d, tn), lambda i, j: (0, nj + j)),
            pl.BlockSpec((width, tn), lambda i, j: (0, j)),
            pl.BlockSpec((width, tn), lambda i, j: (0, nj + j)),
            pl.BlockSpec((1, tn), lambda i, j: (0, j)),
            pl.BlockSpec((1, tn), lambda i, j: (0, nj + j)),
            st_spec_a,
            st_spec_v,
            pl.BlockSpec((tn, d), lambda i, j: (j, 0)),
            pl.BlockSpec((1, d), lambda i, j: (0, 0)),
        ],
        out_specs=[
            pl.BlockSpec((tm, d), lambda i, j: (i, 0)),
            _nb_spec(n_sub, nj, tn, tps),
            _nb_spec(n_sub, nj, tn, tps),
        ],
        out_shape=[jax.ShapeDtypeStruct((rows, d), F32), _nb_shape(n_seq, nj, tn), _nb_shape(n_seq, nj, tn)],
        scratch_shapes=scratch,
        compiler_params=_params("arbitrary", "arbitrary"),
        name="conv_ffn",
    )(x, norm_w, w_up, w_up, conv_w, conv_w, conv_b, conv_b, st8, st8, w_down, final_w)
    return out, jnp.concatenate([_nb_merge(nba), _nb_merge(nbv)], axis=-1)


def _ssd_z_kernel(x_ref, nw_ref, w_ref, o_ref, h_ref):
    @pl.when(pl.program_id(1) == 0)
    def _():
        h_ref[...] = _rms(x_ref[...], nw_ref[...]).astype(BF16)

    o_ref[...] = _silu(_dot(h_ref[...], w_ref[...]))


def _ssd_xbc_kernel(x_ref, nw_ref, w_ref, cw_ref, cb_ref, st_ref, wdt_ref, dtb_ref, o_ref, nb_ref, dt_ref,
                    h_ref, *carry_refs, n_sub, seg, tps):
    j = pl.program_id(1)

    @pl.when(j == 0)
    def _():
        h_ref[...] = _rms(x_ref[...], nw_ref[...]).astype(BF16)

    h = h_ref[...]
    y = _segment_conv(_dot(h, w_ref[...]), st_ref, carry_refs[0] if tps > 1 else None, nb_ref, cw_ref[...],
                      n_sub=n_sub, seg=seg, tps=tps)
    o_ref[...] = _silu(y + cb_ref[...])

    @pl.when(j == pl.num_programs(1) - 1)
    def _():
        dt_ref[...] = _softplus(_dot(h, wdt_ref[...]) + dtb_ref[...])


def _ssd_in(x, st8, norm_w, w_in, conv_w, conv_b, dt_bias, *, n_seq, seg, inner, heads):
    rows, d = x.shape
    conv_dim = conv_w.shape[1]
    tm, n_sub, tps = _row_tiling(n_seq, seg)
    tn = _col_tile(inner)
    assert conv_dim % tn == 0
    x_spec = pl.BlockSpec((tm, d), lambda i, j: (i, 0))
    nw_spec = pl.BlockSpec((1, d), lambda i, j: (0, 0))
    zs = pl.pallas_call(
        _ssd_z_kernel,
        grid=(rows // tm, inner // tn),
        in_specs=[x_spec, nw_spec, pl.BlockSpec((d, tn), lambda i, j: (0, j))],
        out_specs=pl.BlockSpec((tm, tn), lambda i, j: (i, j)),
        out_shape=jax.ShapeDtypeStruct((rows, inner), F32),
        scratch_shapes=[pltpu.VMEM((tm, d), BF16)],
        compiler_params=_params("arbitrary", "arbitrary"),
        name="ssd_gate_proj",
    )(x, norm_w, w_in)

    nj = conv_dim // tn
    off = inner // tn
    w_dt = w_in[:, inner + conv_dim:]
    kern = functools.partial(_ssd_xbc_kernel, n_sub=n_sub, seg=min(seg, tm), tps=tps)
    scratch = [pltpu.VMEM((tm, d), BF16)]
    if tps > 1:
        scratch.append(pltpu.VMEM((nj, SUBLANES, tn), F32))
    xbc, nb, dt = pl.pallas_call(
        kern,
        grid=(rows // tm, nj),
        in_specs=[
            x_spec, nw_spec,
            pl.BlockSpec((d, tn), lambda i, j: (0, off + j)),
            pl.BlockSpec((conv_w.shape[0], tn), lambda i, j: (0, j)),
            pl.BlockSpec((1, tn), lambda i, j: (0, j)),
            pl.BlockSpec((n_sub, SUBLANES, tn), lambda i, j: (i // tps, 0, j)),
            pl.BlockSpec((d, heads), lambda i, j: (0, 0)),
            pl.BlockSpec((1, heads), lambda i, j: (0, 0)),
        ],
        out_specs=[
            pl.BlockSpec((tm, tn), lambda i, j: (i, j)),
            _nb_spec(n_sub, nj, tn, tps),
            pl.BlockSpec((tm, heads), lambda i, j: (i, 0)),
        ],
        out_shape=[jax.ShapeDtypeStruct((rows, conv_dim), F32), _nb_shape(n_seq, nj, tn),
                   jax.ShapeDtypeStruct((rows, heads), F32)],
        scratch_shapes=scratch,
        compiler_params=_params("arbitrary", "arbitrary"),
        name="ssd_xbc_proj",
    )(x, norm_w, w_in, conv_w, conv_b, st8, w_dt, dt_bias)
    return zs, xbc, dt, _nb_merge(nb)


def _split3(x):
    hi = x.astype(BF16)
    r = x - hi.astype(F32)
    mid = r.astype(BF16)
    lo = (r - mid.astype(F32)).astype(BF16)
    return hi, mid, lo


def _dot_f32_lhs(x, sel):
    hi, mid, lo = _split3(x)
    return _dot(hi, sel) + _dot(mid, sel) + _dot(lo, sel)


def _dot_f32_rhs(sel, x):
    hi, mid, lo = _split3(x)
    return _dot(sel, hi) + _dot(sel, mid) + _dot(sel, lo)


def _scan_kernel(xs_ref, b_ref, c_ref, dt_ref, dtt_ref, zs_ref, a_ref, at_ref, dsk_ref, nw_ref, st0_ref,
                 rep_ref, tril_ref, tri2_ref, mask2_ref, bd_ref, y_ref, stout_ref, st_ref,
                 *, groups, hpg, q, n):
    c = pl.program_id(1)
    gw = hpg * q

    @pl.when(c == 0)
    def _():
        st_ref[...] = st0_ref[0]

    dt = dt_ref[...]
    da = dt * a_ref[...]
    cs = _dot_f32_rhs(tril_ref[...], da)
    last = cs[q - 1:q]
    stack = jnp.concatenate([cs, dt, jnp.exp(last - cs) * dt, jnp.exp(cs)], axis=0)
    bc = _dot_f32_lhs(stack, rep_ref[...])
    cs_b, dt_b, tail_b, ecs_b = bc[0:q], bc[q:2 * q], bc[2 * q:3 * q], bc[3 * q:4 * q]
    cst = _dot_f32_lhs(dtt_ref[0] * at_ref[...], tri2_ref[...])
    mask2 = mask2_ref[...] > 0.0
    bd = bd_ref[...]

    for g in range(groups):
        gsl = slice(g * gw, (g + 1) * gw)
        cg = c_ref[:, g * n:(g + 1) * n].astype(BF16)
        bg = b_ref[:, g * n:(g + 1) * n].astype(BF16)
        cb2 = lax.dot_general(cg, jnp.concatenate([bg, bg], axis=0), (((1,), (1,)), ((), ())),
                              preferred_element_type=F32)
        xg = xs_ref[:, gsl]
        xdt = xg * dt_b[:, gsl]
        parts = []
        for k in range(hpg // 2):
            psl = slice(g * gw + 2 * k * q, g * gw + 2 * (k + 1) * q)
            pair = g * (hpg // 2) + k
            seg = cs_b[:, psl] - jnp.broadcast_to(cst[pair:pair + 1], (q, 2 * q))
            m = (cb2 * jnp.where(mask2, jnp.exp(seg), 0.0)).astype(BF16)
            xk = xdt[:, 2 * k * q:2 * (k + 1) * q]
            xbd = (jnp.concatenate([xk, xk], axis=0) * bd).astype(BF16)
            parts.append(_dot(m, xbd))
        y = parts[0] if len(parts) == 1 else jnp.concatenate(parts, axis=1)
        st = st_ref[g]
        y = y + _dot(cg, st.astype(BF16)) * ecs_b[:, gsl]
        xw = (xg * tail_b[:, gsl]).astype(BF16)
        ds = lax.dot_general(bg, xw, (((0,), (0,)), ((), ())), preferred_element_type=F32)
        st_ref[g] = st * jnp.exp(cs_b[q - 1:q, gsl]) + ds
        y = (y + dsk_ref[:, gsl] * xg) * zs_ref[:, gsl]
        y = y * lax.rsqrt(jnp.mean(y * y, axis=-1, keepdims=True) + EPS)
        y_ref[:, gsl] = (y * nw_ref[:, gsl]).astype(BF16)

    @pl.when(c == pl.num_programs(1) - 1)
    def _():
        stout_ref[0] = st_ref[...]


def _ssd_scan(xbc, dt, zs, st0, a_log, d_skip, norm_w, *, n_seq, seg, inner, groups, n, heads):
    rows = xbc.shape[0]
    p = inner // heads
    q = p
    hpg = heads // groups
    assert seg % q == 0 and hpg % 2 == 0 and inner % (groups * n) == 0
    nc = seg // q
    gn = groups * n
    a = -jnp.exp(a_log.astype(F32))
    a_row = a.reshape(1, heads)
    a_t = jnp.repeat(a.reshape(heads // 2, 2), q, axis=1)
    dtt = dt.reshape(rows // q, q, heads // 2, 2).transpose(0, 2, 3, 1).reshape(rows // q, heads // 2, 2 * q)
    dsk = jnp.repeat(d_skip.astype(F32), p).reshape(1, inner)
    rep3 = jnp.repeat(jnp.eye(heads, dtype=BF16), p, axis=1)
    tril3 = jnp.tril(jnp.ones((q, q), BF16))
    triu = jnp.triu(jnp.ones((q, q), F32))
    zero = jnp.zeros((q, q), F32)
    tri23 = jnp.block([[triu, zero], [zero, triu]]).astype(BF16)
    mask2 = jnp.concatenate([jnp.tril(jnp.ones((q, q), F32))] * 2, axis=1)
    one = jnp.ones((q, q), F32)
    bd = jnp.block([[one, zero], [zero, one]])

    kern = functools.partial(_scan_kernel, groups=groups, hpg=hpg, q=q, n=n)
    row_blk = lambda s, c: (s * nc + c, 0)
    const = lambda s, c: (0, 0)
    y, st = pl.pallas_call(
        kern,
        grid=(n_seq, nc),
        in_specs=[
            pl.BlockSpec((q, inner), row_blk),
            pl.BlockSpec((q, gn), lambda s, c: (s * nc + c, inner // gn)),
            pl.BlockSpec((q, gn), lambda s, c: (s * nc + c, inner // gn + 1)),
            pl.BlockSpec((q, heads), row_blk),
            pl.BlockSpec((1, heads // 2, 2 * q), lambda s, c: (s * nc + c, 0, 0)),
            pl.BlockSpec((q, inner), row_blk),
            pl.BlockSpec((1, heads), const),
            pl.BlockSpec((heads // 2, 2 * q), const),
            pl.BlockSpec((1, inner), const),
            pl.BlockSpec((1, inner), const),
            pl.BlockSpec((1, groups, n, hpg * p), lambda s, c: (s, 0, 0, 0)),
            pl.BlockSpec(rep3.shape, const),
            pl.BlockSpec(tril3.shape, const),
            pl.BlockSpec(tri23.shape, const),
            pl.BlockSpec(mask2.shape, const),
            pl.BlockSpec(bd.shape, const),
        ],
        out_specs=[
            pl.BlockSpec((q, inner), row_blk),
            pl.BlockSpec((1, groups, n, hpg * p), lambda s, c: (s, 0, 0, 0)),
        ],
        out_shape=[jax.ShapeDtypeStruct((rows, inner), BF16),
                   jax.ShapeDtypeStruct(st0.shape, F32)],
        scratch_shapes=[pltpu.VMEM((groups, n, hpg * p), F32)],
        compiler_params=_params("arbitrary", "arbitrary"),
        name="ssd_scan",
    )(xbc, xbc, xbc, dt, dtt, zs, a_row, a_t, dsk, norm_w, st0, rep3, tril3, tri23, mask2, bd)
    return y, st


def _out_proj_kernel(x_ref, g_ref, w_ref, o_ref):
    @pl.when(pl.program_id(1) == 0)
    def _():
        o_ref[...] = x_ref[...]

    o_ref[...] += _dot(g_ref[...], w_ref[...])


def _out_proj(x, g, w, *, n_seq, seg):
    rows, d = x.shape
    k = g.shape[1]
    tm, _, _ = _row_tiling(n_seq, seg)
    tk = _col_tile(k, 1024)
    return pl.pallas_call(
        _out_proj_kernel,
        grid=(rows // tm, k // tk),
        in_specs=[
            pl.BlockSpec((tm, d), lambda i, j: (i, 0)),
            pl.BlockSpec((tm, tk), lambda i, j: (i, j)),
            pl.BlockSpec((tk, d), lambda i, j: (j, 0)),
        ],
        out_specs=pl.BlockSpec((tm, d), lambda i, j: (i, 0)),
        out_shape=jax.ShapeDtypeStruct((rows, d), F32),
        compiler_params=_params("arbitrary", "arbitrary"),
        name="ssd_out_proj",
    )(x, g, w)


def _row_tiling(n_seq, seg):
    if seg >= ROW_TILE:
        assert seg % ROW_TILE == 0
        return ROW_TILE, 1, seg // ROW_TILE
    n_sub = max(1, min(n_seq, ROW_TILE // seg))
    while n_seq % n_sub:
        n_sub -= 1
    return n_sub * seg, n_sub, 1


def _col_tile(width, pref=COL_TILE):
    tn = pref
    while width % tn:
        tn //= 2
    assert tn >= 128
    return tn


def _nb_spec(n_sub, nj, tn, tps):
    return pl.BlockSpec((n_sub, nj, SUBLANES, tn), lambda i, j: (i // tps, 0, 0, 0))


def _nb_shape(n_seq, nj, tn):
    return jax.ShapeDtypeStruct((n_seq, nj, SUBLANES, tn), F32)


def _nb_merge(nb):
    n_seq, nj, rows, tn = nb.shape
    return nb.transpose(0, 2, 1, 3).reshape(n_seq, rows, nj * tn)


def _pad8(buf):
    return jnp.pad(buf, ((0, 0), (SUBLANES - buf.shape[1], 0), (0, 0)))


def _state_to_kernel(s, groups):
    b, h, p, n = s.shape
    return s.reshape(b, groups, h // groups, p, n).transpose(0, 1, 4, 2, 3).reshape(b, groups, n, (h // groups) * p)


def _state_from_kernel(s, heads):
    b, g, n, w = s.shape
    hpg = heads // g
    return s.reshape(b, g, n, hpg, w // hpg).transpose(0, 1, 3, 4, 2).reshape(b, heads, w // hpg, n)


def _trunk(x, conv_a8, ssd_conv8, ssd_st, ffn8, wts, *, n_seq, seg):
    depth = wts["norm_mix"].shape[0]
    new_a, new_sc, new_st, new_f = [], [], [], []
    for i in range(depth):
        nm = wts["norm_mix"][i][None]
        j = i // 2
        if i % 2 == 0:
            x, nb = _sc_layer(x, conv_a8[j], nm, wts["sc_w_in"][j], wts["sc_conv_w"][j], wts["sc_w_out"][j],
                              n_seq=n_seq, seg=seg)
            new_a.append(nb)
        else:
            heads = wts["ssd_a_log"].shape[1]
            inner = wts["ssd_w_out"].shape[1]
            groups, n = ssd_st.shape[2], ssd_st.shape[3]
            zs, xbc, dt, nb = _ssd_in(x, ssd_conv8[j], nm, wts["ssd_w_in"][j], wts["ssd_conv_w"][j],
                                      wts["ssd_conv_b"][j][None], wts["ssd_dt_bias"][j][None],
                                      n_seq=n_seq, seg=seg, inner=inner, heads=heads)
            q = inner // heads
            pad = (-seg) % q
            if pad:
                padr = lambda t: jnp.pad(t.reshape(n_seq, seg, -1), ((0, 0), (0, pad), (0, 0))).reshape(
                    n_seq * (seg + pad), -1)
                xbc_s, dt_s, zs_s = padr(xbc), padr(dt), padr(zs)
            else:
                xbc_s, dt_s, zs_s = xbc, dt, zs
            g, st = _ssd_scan(xbc_s, dt_s, zs_s, ssd_st[j], wts["ssd_a_log"][j], wts["ssd_d"][j],
                              wts["ssd_norm_w"][j][None], n_seq=n_seq, seg=seg + pad, inner=inner,
                              groups=groups, n=n, heads=heads)
            if pad:
                g = g.reshape(n_seq, seg + pad, -1)[:, :seg].reshape(n_seq * seg, -1)
            x = _out_proj(x, g, wts["ssd_w_out"][j], n_seq=n_seq, seg=seg)
            new_sc.append(nb)
            new_st.append(st)
        x, nf = _ffn_layer(x, ffn8[i], wts["norm_ffn"][i][None], wts["ffn_w_up"][i], wts["ffn_conv_w"][i],
                           wts["ffn_conv_b"][i][None], wts["ffn_w_down"][i], wts["norm_final"][None],
                           n_seq=n_seq, seg=seg, final_norm=(i == depth - 1))
        new_f.append(nf)
    return x, jnp.stack(new_a), jnp.stack(new_sc), jnp.stack(new_st), jnp.stack(new_f)


def kernel(x_prompt, x_sample, state_conv_a, state_ssd_conv, state_ssd, state_ffn_conv, meta_tokens, norm_mix,
           norm_ffn, norm_final, sc_w_in, sc_conv_w, sc_w_out, ssd_w_in, ssd_conv_w, ssd_conv_b, ssd_dt_bias,
           ssd_a_log, ssd_d, ssd_norm_w, ssd_w_out, ffn_w_up, ffn_conv_w, ffn_conv_b, ffn_w_down):
    b, seq, d = x_prompt.shape
    sb, sseq, _ = x_sample.shape
    n_meta = meta_tokens.shape[0]
    heads = ssd_a_log.shape[1]
    inner = ssd_w_out.shape[1]
    n_state = state_ssd.shape[-1]
    groups = (ssd_conv_w.shape[2] - inner) // (2 * n_state)
    wts = dict(
        norm_mix=norm_mix, norm_ffn=norm_ffn, norm_final=norm_final,
        sc_w_in=sc_w_in.astype(BF16), sc_conv_w=sc_conv_w, sc_w_out=sc_w_out.astype(BF16),
        ssd_w_in=ssd_w_in.astype(BF16), ssd_conv_w=ssd_conv_w, ssd_conv_b=ssd_conv_b, ssd_dt_bias=ssd_dt_bias,
        ssd_a_log=ssd_a_log, ssd_d=ssd_d, ssd_norm_w=ssd_norm_w, ssd_w_out=ssd_w_out.astype(BF16),
        ffn_w_up=ffn_w_up.astype(BF16), ffn_conv_w=ffn_conv_w, ffn_conv_b=ffn_conv_b,
        ffn_w_down=ffn_w_down.astype(BF16))

    def zeros8(like, n_seq):
        return jnp.zeros((like.shape[0], n_seq, SUBLANES, like.shape[-1]), F32)

    zero_st = jnp.zeros((state_ssd.shape[0], 1, groups, n_state, inner // groups), F32)
    _, m_a, m_sc, m_st, m_f = _trunk(
        meta_tokens.astype(F32), zeros8(state_conv_a, 1), zeros8(state_ssd_conv, 1), zero_st,
        zeros8(state_ffn_conv, 1), wts, n_seq=1, seg=n_meta)
    rep = lambda t: jnp.broadcast_to(t, (t.shape[0], b) + t.shape[2:])
    yp, p_a, p_sc, p_st, p_f = _trunk(
        x_prompt.reshape(b * seq, d), rep(m_a), rep(m_sc), rep(m_st), rep(m_f), wts, n_seq=b, seg=seq)
    ys, s_a, s_sc, s_st, s_f = _trunk(
        x_sample.reshape(sb * sseq, d), jax.vmap(_pad8)(state_conv_a), jax.vmap(_pad8)(state_ssd_conv),
        jax.vmap(lambda s: _state_to_kernel(s, groups))(state_ssd.astype(F32)), jax.vmap(_pad8)(state_ffn_conv),
        wts, n_seq=sb, seg=sseq)

    tail = lambda t, like: t[:, :, SUBLANES - like.shape[2]:, :]
    unst = jax.vmap(lambda s: _state_from_kernel(s, heads))
    return (yp.reshape(b, seq, d), ys.reshape(sb, sseq, d),
            tail(p_a, state_conv_a), tail(p_sc, state_ssd_conv), unst(p_st), tail(p_f, state_ffn_conv),
            tail(s_a, state_conv_a), tail(s_sc, state_ssd_conv), unst(s_st), tail(s_f, state_ffn_conv))
```

```python
import functools

import jax
import jax.numpy as jnp
from jax import lax
from jax.experimental import pallas as pl
from jax.experimental.pallas import tpu as pltpu

EPS = 1e-6
F32 = jnp.float32
BF16 = jnp.bfloat16

SUBLANES = 8
VMEM_LIMIT_BYTES = 56 * 1024 * 1024
ROW_TILE = 512
COL_TILE = 512
WIDE_COL_TILE = 1024
SUB_TILE = 256


def _dot(a, b):
    return jnp.dot(a, b, preferred_element_type=F32)


def _params(*sem):
    return pltpu.CompilerParams(dimension_semantics=sem, vmem_limit_bytes=VMEM_LIMIT_BYTES)


def _rms(x, w):
    return x * lax.rsqrt(jnp.mean(x * x, axis=-1, keepdims=True) + EPS) * w


def _silu(x):
    return x * (0.5 + 0.5 * jnp.tanh(0.5 * x))


def _softplus(x):
    return jnp.maximum(x, 0.0) + jnp.log1p(jnp.exp(-jnp.abs(x)))


def _stage(u_ref, idx, u, st_ref, carry_ref, nb_ref, cols, *, n_sub, seg, tps, chunk=None):
    i = pl.program_id(0)
    j = pl.program_id(1) if chunk is None else chunk
    for s in range(n_sub):
        us = u[s * seg:(s + 1) * seg]
        halo = st_ref[s, :, cols]
        if tps > 1:
            halo = jnp.where((i % tps) == 0, halo, carry_ref[j, :, cols])
        base = s * (seg + SUBLANES)
        u_ref[idx + (pl.ds(base, SUBLANES), slice(None))] = halo
        u_ref[idx + (pl.ds(base + SUBLANES, seg), slice(None))] = us
        nb_ref[s, j, :, cols] = us[seg - SUBLANES:seg]
    if tps > 1:
        carry_ref[j, :, cols] = u[u.shape[0] - SUBLANES:]


def _staged_conv(u_ref, idx, w, *, n_sub, seg):
    width = w.shape[0]
    ys = []
    for s in range(n_sub):
        base = s * (seg + SUBLANES) + SUBLANES
        y = None
        for d in range(width):
            term = u_ref[idx + (pl.ds(base - d, seg), slice(None))] * w[width - 1 - d:width - d]
            y = term if y is None else y + term
        ys.append(y)
    return ys[0] if n_sub == 1 else jnp.concatenate(ys, axis=0)


def _staged(u_ref, idx, *, n_sub, seg):
    parts = [u_ref[idx + (pl.ds(s * (seg + SUBLANES) + SUBLANES, seg), slice(None))] for s in range(n_sub)]
    return parts[0] if n_sub == 1 else jnp.concatenate(parts, axis=0)


def _clear_carries(carry_refs):
    @pl.when(pl.program_id(0) == 0)
    def _():
        for ref in carry_refs:
            ref[...] = jnp.zeros(ref.shape, F32)


def _sub_chunks(tn):
    step = min(SUB_TILE, tn)
    return [slice(k, k + step) for k in range(0, tn, step)]


def _u_scratch(parts, n_sub, seg, tn):
    step = min(SUB_TILE, tn)
    return [pltpu.VMEM((2, n_sub * (seg + SUBLANES), step), F32) for _ in range(parts * (tn // step))]


def _sc_kernel(x_ref, nw_ref, wb_ref, wc_ref, wv_ref, cw_ref, st_ref, w2_ref, o_ref, nb_ref,
               h_ref, *scratch, n_sub, seg, tps):
    j = pl.program_id(1)
    carry_refs, stage_refs = (scratch[:1], scratch[1:]) if tps > 1 else ((), scratch)
    subs = _sub_chunks(wb_ref.shape[1])
    u_refs, b_refs = stage_refs[:len(subs)], stage_refs[len(subs):]

    @pl.when(j == 0)
    def _():
        x = x_ref[...]
        h_ref[...] = _rms(x, nw_ref[...]).astype(BF16)
        o_ref[...] = x
        _clear_carries(carry_refs)

    h = h_ref[...]
    slot = j % 2
    carry = carry_refs[0] if tps > 1 else None
    g_prev, cs_prev = None, None
    for k, cs in enumerate(subs):
        b_refs[k][slot] = _dot(h, wb_ref[:, cs])
        _stage(u_refs[k], (slot,), _dot(h, wc_ref[:, cs]) * _dot(h, wv_ref[:, cs]), st_ref, carry, nb_ref, cs,
               n_sub=n_sub, seg=seg, tps=tps)
        if g_prev is not None:
            o_ref[...] += _dot(g_prev, w2_ref[cs_prev, :])
        y = _staged_conv(u_refs[k], (slot,), cw_ref[:, cs], n_sub=n_sub, seg=seg)
        g_prev, cs_prev = (b_refs[k][slot] * y).astype(BF16), cs
    o_ref[...] += _dot(g_prev, w2_ref[cs_prev, :])


def _sc_layer(x, st8, norm_w, w_in, conv_w, w_out, *, n_seq, seg):
    rows, d = x.shape
    tm, n_sub, tps = _row_tiling(n_seq, seg)
    tn = _col_tile(d)
    nj = d // tn
    kern = functools.partial(_sc_kernel, n_sub=n_sub, seg=min(seg, tm), tps=tps)
    step = min(SUB_TILE, tn)
    scratch = [pltpu.VMEM((tm, d), BF16)]
    if tps > 1:
        scratch.append(pltpu.VMEM((nj, SUBLANES, tn), F32))
    scratch += _u_scratch(1, n_sub, min(seg, tm), tn)
    scratch += [pltpu.VMEM((2, tm, step), F32) for _ in range(tn // step)]
    out, nb = pl.pallas_call(
        kern,
        grid=(rows // tm, nj),
        in_specs=[
            pl.BlockSpec((tm, d), lambda i, j: (i, 0)),
            pl.BlockSpec((1, d), lambda i, j: (0, 0)),
            pl.BlockSpec((d, tn), lambda i, j: (0, j)),
            pl.BlockSpec((d, tn), lambda i, j: (0, nj + j)),
            pl.BlockSpec((d, tn), lambda i, j: (0, 2 * nj + j)),
            pl.BlockSpec((conv_w.shape[0], tn), lambda i, j: (0, j)),
            pl.BlockSpec((n_sub, SUBLANES, tn), lambda i, j: (i // tps, 0, j)),
            pl.BlockSpec((tn, d), lambda i, j: (j, 0)),
        ],
        out_specs=[
            pl.BlockSpec((tm, d), lambda i, j: (i, 0)),
            _nb_spec(n_sub, nj, tn, tps),
        ],
        out_shape=[jax.ShapeDtypeStruct((rows, d), F32), _nb_shape(n_seq, nj, tn)],
        scratch_shapes=scratch,
        compiler_params=_params("arbitrary", "arbitrary"),
        name="short_conv_mixer",
    )(x, norm_w, w_in, w_in, w_in, conv_w, st8, w_out)
    return out, _nb_merge(nb)


def _ffn_kernel(x_ref, nw_ref, wa_ref, wv_ref, cwa_ref, cwv_ref, ba_ref, bv_ref, sta_ref, stv_ref, w2_ref,
                fw_ref, o_ref, nba_ref, nbv_ref, h_ref, *scratch, n_sub, seg, tps, final_norm):
    j = pl.program_id(1)
    carry_refs, stage_refs = (scratch[:2], scratch[2:]) if tps > 1 else ((), scratch)
    ca, cv = carry_refs if tps > 1 else (None, None)
    subs = _sub_chunks(wa_ref.shape[1])
    ua_refs, uv_refs = stage_refs[:len(subs)], stage_refs[len(subs):]
    stage = functools.partial(_stage, n_sub=n_sub, seg=seg, tps=tps)
    conv = functools.partial(_staged_conv, n_sub=n_sub, seg=seg)

    @pl.when(j == 0)
    def _():
        x = x_ref[...]
        h_ref[...] = _rms(x, nw_ref[...]).astype(BF16)
        o_ref[...] = x
        _clear_carries(carry_refs)

    h = h_ref[...]
    slot = j % 2
    g_prev, cs_prev = None, None
    for k, cs in enumerate(subs):
        stage(ua_refs[k], (slot,), _dot(h, wa_ref[:, cs]), sta_ref, ca, nba_ref, cs)
        stage(uv_refs[k], (slot,), _dot(h, wv_ref[:, cs]), stv_ref, cv, nbv_ref, cs)
        if g_prev is not None:
            o_ref[...] += _dot(g_prev, w2_ref[cs_prev, :])
        a = conv(ua_refs[k], (slot,), cwa_ref[:, cs]) + ba_ref[:, cs]
        v = conv(uv_refs[k], (slot,), cwv_ref[:, cs]) + bv_ref[:, cs]
        g_prev, cs_prev = (_silu(a) * v).astype(BF16), cs
    o_ref[...] += _dot(g_prev, w2_ref[cs_prev, :])

    if final_norm:
        @pl.when(j == pl.num_programs(1) - 1)
        def _():
            o_ref[...] = _rms(o_ref[...], fw_ref[...])


def _ffn_layer(x, st8, norm_w, w_up, conv_w, conv_b, w_down, final_w, *, n_seq, seg, final_norm):
    rows, d = x.shape
    dff = w_down.shape[0]
    tm, n_sub, tps = _row_tiling(n_seq, seg)
    tn = _col_tile(dff)
    nj = dff // tn
    kern = functools.partial(_ffn_kernel, n_sub=n_sub, seg=min(seg, tm), tps=tps, final_norm=final_norm)
    scratch = [pltpu.VMEM((tm, d), BF16)]
    if tps > 1:
        scratch += [pltpu.VMEM((nj, SUBLANES, tn), F32), pltpu.VMEM((nj, SUBLANES, tn), F32)]
    scratch += _u_scratch(2, n_sub, min(seg, tm), tn)
    width = conv_w.shape[0]
    out, nba, nbv = pl.pallas_call(
        kern,
        grid=(rows // tm, nj),
        in_specs=[
            pl.BlockSpec((tm, d), lambda i, j: (i, 0)),
            pl.BlockSpec((1, d), lambda i, j: (0, 0)),
            pl.BlockSpec((d, tn), lambda i, j: (0, j)),
            pl.BlockSpec((d, tn), lambda i, j: (0, nj + j)),
            pl.BlockSpec((width, tn), lambda i, j: (0, j)),
            pl.BlockSpec((width, tn), lambda i, j: (0, nj + j)),
            pl.BlockSpec((1, tn), lambda i, j: (0, j)),
            pl.BlockSpec((1, tn), lambda i, j: (0, nj + j)),
            pl.BlockSpec((n_sub, SUBLANES, tn), lambda i, j: (i // tps, 0, j)),
            pl.BlockSpec((n_sub, SUBLANES, tn), lambda i, j: (i // tps, 0, nj + j)),
            pl.BlockSpec((tn, d), lambda i, j: (j, 0)),
            pl.BlockSpec((1, d), lambda i, j: (0, 0)),
        ],
        out_specs=[
            pl.BlockSpec((tm, d), lambda i, j: (i, 0)),
            _nb_spec(n_sub, nj, tn, tps),
            _nb_spec(n_sub, nj, tn, tps),
        ],
        out_shape=[jax.ShapeDtypeStruct((rows, d), F32), _nb_shape(n_seq, nj, tn), _nb_shape(n_seq, nj, tn)],
        scratch_shapes=scratch,
        compiler_params=_params("arbitrary", "arbitrary"),
        name="conv_ffn",
    )(x, norm_w, w_up, w_up, conv_w, conv_w, conv_b, conv_b, st8, st8, w_down, final_w)
    return out, jnp.concatenate([_nb_merge(nba), _nb_merge(nbv)], axis=-1)


def _ssd_in_kernel(x_ref, nw_ref, w_ref, cw_ref, cb_ref, st_ref, wdt_ref, dtb_ref, o_ref, nb_ref, dt_ref,
                   h_ref, *scratch, n_sub, seg, tps, nz):
    j = pl.program_id(1)
    carry_refs, u_refs = (scratch[:1], scratch[1:]) if tps > 1 else ((), scratch)
    subs = _sub_chunks(w_ref.shape[1])

    @pl.when(j == 0)
    def _():
        h_ref[...] = _rms(x_ref[...], nw_ref[...]).astype(BF16)
        _clear_carries(carry_refs)

    h = h_ref[...]

    @pl.when(j < nz)
    def _():
        for cs in subs:
            o_ref[:, cs] = _silu(_dot(h, w_ref[:, cs]))

    @pl.when(j >= nz)
    def _():
        slot = j % 2
        for k, cs in enumerate(subs):
            _stage(u_refs[k], (slot,), _dot(h, w_ref[:, cs]), st_ref, carry_refs[0] if tps > 1 else None, nb_ref,
                   cs, n_sub=n_sub, seg=seg, tps=tps, chunk=j - nz)
            y = _staged_conv(u_refs[k], (slot,), cw_ref[:, cs], n_sub=n_sub, seg=seg)
            o_ref[:, cs] = _silu(y + cb_ref[:, cs])

    @pl.when(j == pl.num_programs(1) - 1)
    def _():
        dt_ref[...] = _softplus(_dot(h, wdt_ref[...]) + dtb_ref[...])


def _ssd_in(x, st8, norm_w, w_in, conv_w, conv_b, dt_bias, *, n_seq, seg, inner, heads):
    rows, d = x.shape
    width, conv_dim = conv_w.shape
    cols = inner + conv_dim
    tm, n_sub, tps = _row_tiling(n_seq, seg)
    tn = _col_tile(inner, WIDE_COL_TILE)
    assert conv_dim % tn == 0
    nj = cols // tn
    nz = inner // tn
    njx = nj - nz
    xc = lambda j: jnp.maximum(j - nz, 0)
    w_dt = w_in[:, cols:]
    kern = functools.partial(_ssd_in_kernel, n_sub=n_sub, seg=min(seg, tm), tps=tps, nz=nz)
    scratch = [pltpu.VMEM((tm, d), BF16)]
    if tps > 1:
        scratch.append(pltpu.VMEM((njx, SUBLANES, tn), F32))
    scratch += _u_scratch(1, n_sub, min(seg, tm), tn)
    zx, nb, dt = pl.pallas_call(
        kern,
        grid=(rows // tm, nj),
        in_specs=[
            pl.BlockSpec((tm, d), lambda i, j: (i, 0)),
            pl.BlockSpec((1, d), lambda i, j: (0, 0)),
            pl.BlockSpec((d, tn), lambda i, j: (0, j)),
            pl.BlockSpec((width, tn), lambda i, j: (0, xc(j))),
            pl.BlockSpec((1, tn), lambda i, j: (0, xc(j))),
            pl.BlockSpec((n_sub, SUBLANES, tn), lambda i, j: (i // tps, 0, xc(j))),
            pl.BlockSpec((d, heads), lambda i, j: (0, 0)),
            pl.BlockSpec((1, heads), lambda i, j: (0, 0)),
        ],
        out_specs=[
            pl.BlockSpec((tm, tn), lambda i, j: (i, j)),
            _nb_spec(n_sub, njx, tn, tps),
            pl.BlockSpec((tm, heads), lambda i, j: (i, 0)),
        ],
        out_shape=[jax.ShapeDtypeStruct((rows, cols), F32), _nb_shape(n_seq, njx, tn),
                   jax.ShapeDtypeStruct((rows, heads), F32)],
        scratch_shapes=scratch,
        compiler_params=_params("arbitrary", "arbitrary"),
        name="ssd_in_proj",
    )(x, norm_w, w_in, conv_w, conv_b, st8, w_dt, dt_bias)
    return zx, dt, _nb_merge(nb)


def _split3(x):
    hi = x.astype(BF16).astype(F32)
    r = x - hi
    mid = r.astype(BF16).astype(F32)
    lo = (r - mid).astype(BF16).astype(F32)
    return hi, mid, lo


def _dot_f32_lhs(x, sel):
    hi, mid, lo = _split3(x)
    return _dot(hi.astype(BF16), sel) + _dot(mid.astype(BF16), sel) + _dot(lo.astype(BF16), sel)


def _dot_f32_rhs(sel, x):
    hi, mid, lo = _split3(x)
    return _dot(sel, hi.astype(BF16)) + _dot(sel, mid.astype(BF16)) + _dot(sel, lo.astype(BF16))


def _scan_kernel(zs_ref, xs_ref, b_ref, c_ref, dt3_ref, dtt_ref, a3_ref, at_ref, dsk_ref, nw_ref, st0_ref,
                 rep3_ref, tril_ref, tri2_ref, mask2_ref, bd_ref, y_ref, stout_ref, st_ref,
                 *, groups, hpg, q, n):
    c = pl.program_id(1)
    gw = hpg * q
    heads = groups * hpg

    @pl.when(c == 0)
    def _():
        st_ref[...] = st0_ref[0]

    dt3 = dt3_ref[...]
    cs3 = _dot_f32_rhs(tril_ref[...], dt3 * a3_ref[...])
    tail3 = jnp.exp(cs3[q - 1:q] - cs3) * dt3
    hi, mid, lo = _split3(jnp.concatenate([cs3, tail3], axis=0))
    lane = lax.broadcasted_iota(jnp.int32, hi.shape, 1)
    pieces = jnp.where(lane < heads, hi, jnp.where(lane < 2 * heads, mid, lo)).astype(BF16)
    bc = _dot(pieces, rep3_ref[...])
    cs_b, tail_b = bc[0:q], bc[q:2 * q]
    ecs_b = jnp.exp(cs_b)
    dtt = dtt_ref[0]
    cst = _dot_f32_lhs(dtt * at_ref[...], tri2_ref[...])
    mask2 = mask2_ref[...] > 0.0
    bd = bd_ref[...]

    for g in range(groups):
        gsl = slice(g * gw, (g + 1) * gw)
        cg = c_ref[:, g * n:(g + 1) * n].astype(BF16)
        bg = b_ref[:, g * n:(g + 1) * n].astype(BF16)
        cb2 = lax.dot_general(cg, jnp.concatenate([bg, bg], axis=0), (((1,), (1,)), ((), ())),
                              preferred_element_type=F32)
        xg = xs_ref[:, gsl]
        parts = []
        for k in range(hpg // 2):
            psl = slice(g * gw + 2 * k * q, g * gw + 2 * (k + 1) * q)
            pair = g * (hpg // 2) + k
            seg = cs_b[:, psl] - jnp.broadcast_to(cst[pair:pair + 1], (q, 2 * q))
            w = jnp.where(mask2, jnp.exp(seg), 0.0) * jnp.broadcast_to(dtt[pair:pair + 1], (q, 2 * q))
            xk = xg[:, 2 * k * q:2 * (k + 1) * q]
            xbd = (jnp.concatenate([xk, xk], axis=0) * bd).astype(BF16)
            parts.append(_dot((cb2 * w).astype(BF16), xbd))
        y = parts[0] if len(parts) == 1 else jnp.concatenate(parts, axis=1)
        st = st_ref[g]
        y = y + _dot(cg, st.astype(BF16)) * ecs_b[:, gsl]
        xw = (xg * tail_b[:, gsl]).astype(BF16)
        ds = lax.dot_general(bg, xw, (((0,), (0,)), ((), ())), preferred_element_type=F32)
        st_ref[g] = st * ecs_b[q - 1:q, gsl] + ds
        y = (y + dsk_ref[:, gsl] * xg) * zs_ref[:, gsl]
        y = y * lax.rsqrt(jnp.mean(y * y, axis=-1, keepdims=True) + EPS)
        y_ref[:, gsl] = (y * nw_ref[:, gsl]).astype(BF16)

    @pl.when(c == pl.num_programs(1) - 1)
    def _():
        stout_ref[0] = st_ref[...]


def _ssd_scan(zx, dt, st0, a_log, d_skip, norm_w, *, n_seq, seg, inner, groups, n, heads):
    rows = zx.shape[0]
    p = inner // heads
    q = p
    hpg = heads // groups
    assert seg % q == 0 and hpg % 2 == 0 and inner % (groups * n) == 0
    nc = seg // q
    gn = groups * n
    a = -jnp.exp(a_log.astype(F32))
    a3 = jnp.tile(a.reshape(1, heads), (1, 3))
    a_t = jnp.repeat(a.reshape(heads // 2, 2), q, axis=1)
    dt3 = jnp.tile(dt, (1, 3))
    dtt = dt.reshape(rows // q, q, heads // 2, 2).transpose(0, 2, 3, 1).reshape(rows // q, heads // 2, 2 * q)
    dsk = jnp.repeat(d_skip.astype(F32), p).reshape(1, inner)
    rep3 = jnp.tile(jnp.repeat(jnp.eye(heads, dtype=BF16), p, axis=1), (3, 1))
    tril = jnp.tril(jnp.ones((q, q), BF16))
    triu = jnp.triu(jnp.ones((q, q), F32))
    zero = jnp.zeros((q, q), F32)
    tri2 = jnp.block([[triu, zero], [zero, triu]]).astype(BF16)
    mask2 = jnp.concatenate([jnp.tril(jnp.ones((q, q), F32))] * 2, axis=1)
    one = jnp.ones((q, q), F32)
    bd = jnp.block([[one, zero], [zero, one]])

    kern = functools.partial(_scan_kernel, groups=groups, hpg=hpg, q=q, n=n)
    const = lambda s, c: (0, 0)
    y, st = pl.pallas_call(
        kern,
        grid=(n_seq, nc),
        in_specs=[
            pl.BlockSpec((q, inner), lambda s, c: (s * nc + c, 0)),
            pl.BlockSpec((q, inner), lambda s, c: (s * nc + c, 1)),
            pl.BlockSpec((q, gn), lambda s, c: (s * nc + c, 2 * inner // gn)),
            pl.BlockSpec((q, gn), lambda s, c: (s * nc + c, 2 * inner // gn + 1)),
            pl.BlockSpec((q, 3 * heads), lambda s, c: (s * nc + c, 0)),
            pl.BlockSpec((1, heads // 2, 2 * q), lambda s, c: (s * nc + c, 0, 0)),
            pl.BlockSpec((1, 3 * heads), const),
            pl.BlockSpec((heads // 2, 2 * q), const),
            pl.BlockSpec((1, inner), const),
            pl.BlockSpec((1, inner), const),
            pl.BlockSpec((1, groups, n, hpg * p), lambda s, c: (s, 0, 0, 0)),
            pl.BlockSpec(rep3.shape, const),
            pl.BlockSpec(tril.shape, const),
            pl.BlockSpec(tri2.shape, const),
            pl.BlockSpec(mask2.shape, const),
            pl.BlockSpec(bd.shape, const),
        ],
        out_specs=[
            pl.BlockSpec((q, inner), lambda s, c: (s * nc + c, 0)),
            pl.BlockSpec((1, groups, n, hpg * p), lambda s, c: (s, 0, 0, 0)),
        ],
        out_shape=[jax.ShapeDtypeStruct((rows, inner), BF16),
                   jax.ShapeDtypeStruct(st0.shape, F32)],
        scratch_shapes=[pltpu.VMEM((groups, n, hpg * p), F32)],
        compiler_params=_params("arbitrary", "arbitrary"),
        name="ssd_scan",
    )(zx, zx, zx, zx, dt3, dtt, a3, a_t, dsk, norm_w, st0, rep3, tril, tri2, mask2, bd)
    return y, st


def _out_proj_kernel(x_ref, g_ref, w_ref, o_ref):
    o_ref[...] = x_ref[...] + _dot(g_ref[...], w_ref[...])


def _out_proj(x, g, w, *, n_seq, seg):
    rows, d = x.shape
    k = g.shape[1]
    tm, _, _ = _row_tiling(n_seq, seg)
    return pl.pallas_call(
        _out_proj_kernel,
        grid=(rows // tm,),
        in_specs=[
            pl.BlockSpec((tm, d), lambda i: (i, 0)),
            pl.BlockSpec((tm, k), lambda i: (i, 0)),
            pl.BlockSpec((k, d), lambda i: (0, 0), pipeline_mode=pl.Buffered(1)),
        ],
        out_specs=pl.BlockSpec((tm, d), lambda i: (i, 0)),
        out_shape=jax.ShapeDtypeStruct((rows, d), F32),
        compiler_params=_params("arbitrary"),
        name="ssd_out_proj",
    )(x, g, w)


def _row_tiling(n_seq, seg):
    if seg >= ROW_TILE:
        assert seg % ROW_TILE == 0
        return ROW_TILE, 1, seg // ROW_TILE
    n_sub = max(1, min(n_seq, ROW_TILE // seg))
    while n_seq % n_sub:
        n_sub -= 1
    return n_sub * seg, n_sub, 1


def _col_tile(width, pref=COL_TILE):
    tn = pref
    while width % tn:
        tn //= 2
    assert tn >= 128
    return tn


def _nb_spec(n_sub, nj, tn, tps):
    return pl.BlockSpec((n_sub, nj, SUBLANES, tn), lambda i, j: (i // tps, 0, 0, 0))


def _nb_shape(n_seq, nj, tn):
    return jax.ShapeDtypeStruct((n_seq, nj, SUBLANES, tn), F32)


def _nb_merge(nb):
    n_seq, nj, rows, tn = nb.shape
    return nb.transpose(0, 2, 1, 3).reshape(n_seq, rows, nj * tn)


def _pad8(buf):
    return jnp.pad(buf, ((0, 0), (SUBLANES - buf.shape[1], 0), (0, 0)))


def _state_to_kernel(s, groups):
    b, h, p, n = s.shape
    return s.reshape(b, groups, h // groups, p, n).transpose(0, 1, 4, 2, 3).reshape(b, groups, n, (h // groups) * p)


def _state_from_kernel(s, heads):
    b, g, n, w = s.shape
    hpg = heads // g
    return s.reshape(b, g, n, hpg, w // hpg).transpose(0, 1, 3, 4, 2).reshape(b, heads, w // hpg, n)


def _trunk(x, conv_a8, ssd_conv8, ssd_st, ffn8, wts, *, n_seq, seg):
    depth = wts["norm_mix"].shape[0]
    new_a, new_sc, new_st, new_f = [], [], [], []
    for i in range(depth):
        nm = wts["norm_mix"][i][None]
        j = i // 2
        if i % 2 == 0:
            x, nb = _sc_layer(x, conv_a8[j], nm, wts["sc_w_in"][j], wts["sc_conv_w"][j], wts["sc_w_out"][j],
                              n_seq=n_seq, seg=seg)
            new_a.append(nb)
        else:
            heads = wts["ssd_a_log"].shape[1]
            inner = wts["ssd_w_out"].shape[1]
            groups, n = ssd_st.shape[2], ssd_st.shape[3]
            zx, dt, nb = _ssd_in(x, ssd_conv8[j], nm, wts["ssd_w_in"][j], wts["ssd_conv_w"][j],
                                 wts["ssd_conv_b"][j][None], wts["ssd_dt_bias"][j][None],
                                 n_seq=n_seq, seg=seg, inner=inner, heads=heads)
            q = inner // heads
            pad = (-seg) % q
            if pad:
                padr = lambda t: jnp.pad(t.reshape(n_seq, seg, -1), ((0, 0), (0, pad), (0, 0))).reshape(
                    n_seq * (seg + pad), -1)
                zx, dt = padr(zx), padr(dt)
            g, st = _ssd_scan(zx, dt, ssd_st[j], wts["ssd_a_log"][j], wts["ssd_d"][j],
                              wts["ssd_norm_w"][j][None], n_seq=n_seq, seg=seg + pad, inner=inner,
                              groups=groups, n=n, heads=heads)
            if pad:
                g = g.reshape(n_seq, seg + pad, -1)[:, :seg].reshape(n_seq * seg, -1)
            x = _out_proj(x, g, wts["ssd_w_out"][j], n_seq=n_seq, seg=seg)
            new_sc.append(nb)
            new_st.append(st)
        x, nf = _ffn_layer(x, ffn8[i], wts["norm_ffn"][i][None], wts["ffn_w_up"][i], wts["ffn_conv_w"][i],
                           wts["ffn_conv_b"][i][None], wts["ffn_w_down"][i], wts["norm_final"][None],
                           n_seq=n_seq, seg=seg, final_norm=(i == depth - 1))
        new_f.append(nf)
    return x, jnp.stack(new_a), jnp.stack(new_sc), jnp.stack(new_st), jnp.stack(new_f)


def kernel(x_prompt, x_sample, state_conv_a, state_ssd_conv, state_ssd, state_ffn_conv, meta_tokens, norm_mix,
           norm_ffn, norm_final, sc_w_in, sc_conv_w, sc_w_out, ssd_w_in, ssd_conv_w, ssd_conv_b, ssd_dt_bias,
           ssd_a_log, ssd_d, ssd_norm_w, ssd_w_out, ffn_w_up, ffn_conv_w, ffn_conv_b, ffn_w_down):
    b, seq, d = x_prompt.shape
    sb, sseq, _ = x_sample.shape
    n_meta = meta_tokens.shape[0]
    heads = ssd_a_log.shape[1]
    inner = ssd_w_out.shape[1]
    n_state = state_ssd.shape[-1]
    groups = (ssd_conv_w.shape[2] - inner) // (2 * n_state)
    wts = dict(
        norm_mix=norm_mix, norm_ffn=norm_ffn, norm_final=norm_final,
        sc_w_in=sc_w_in.astype(BF16), sc_conv_w=sc_conv_w, sc_w_out=sc_w_out.astype(BF16),
        ssd_w_in=ssd_w_in.astype(BF16), ssd_conv_w=ssd_conv_w, ssd_conv_b=ssd_conv_b, ssd_dt_bias=ssd_dt_bias,
        ssd_a_log=ssd_a_log, ssd_d=ssd_d, ssd_norm_w=ssd_norm_w, ssd_w_out=ssd_w_out.astype(BF16),
        ffn_w_up=ffn_w_up.astype(BF16), ffn_conv_w=ffn_conv_w, ffn_conv_b=ffn_conv_b,
        ffn_w_down=ffn_w_down.astype(BF16))

    def zeros8(like, n_seq):
        return jnp.zeros((like.shape[0], n_seq, SUBLANES, like.shape[-1]), F32)

    zero_st = jnp.zeros((state_ssd.shape[0], 1, groups, n_state, inner // groups), F32)
    _, m_a, m_sc, m_st, m_f = _trunk(
        meta_tokens.astype(F32), zeros8(state_conv_a, 1), zeros8(state_ssd_conv, 1), zero_st,
        zeros8(state_ffn_conv, 1), wts, n_seq=1, seg=n_meta)
    rep = lambda t: jnp.broadcast_to(t, (t.shape[0], b) + t.shape[2:])
    yp, p_a, p_sc, p_st, p_f = _trunk(
        x_prompt.reshape(b * seq, d), rep(m_a), rep(m_sc), rep(m_st), rep(m_f), wts, n_seq=b, seg=seq)
    ys, s_a, s_sc, s_st, s_f = _trunk(
        x_sample.reshape(sb * sseq, d), jax.vmap(_pad8)(state_conv_a), jax.vmap(_pad8)(state_ssd_conv),
        jax.vmap(lambda s: _state_to_kernel(s, groups))(state_ssd.astype(F32)), jax.vmap(_pad8)(state_ffn_conv),
        wts, n_seq=sb, seg=sseq)

    tail = lambda t, like: t[:, :, SUBLANES - like.shape[2]:, :]
    unst = jax.vmap(lambda s: _state_from_kernel(s, heads))
    return (yp.reshape(b, seq, d), ys.reshape(sb, sseq, d),
            tail(p_a, state_conv_a), tail(p_sc, state_ssd_conv), unst(p_st), tail(p_f, state_ffn_conv),
            tail(s_a, state_conv_a), tail(s_sc, state_ssd_conv), unst(s_st), tail(s_f, state_ffn_conv))
```

```python
import functools

import jax
import jax.numpy as jnp
from jax import lax
from jax.experimental import pallas as pl
from jax.experimental.pallas import tpu as pltpu

EPS = 1e-6
F32 = jnp.float32
BF16 = jnp.bfloat16

SUBLANES = 8
VMEM_LIMIT_BYTES = 56 * 1024 * 1024
ROW_TILE = 512
COL_TILE = 512
WIDE_ROW_TILE = 512
WIDE_COL_TILE = 1024
SUB_TILE = 256


def _dot(a, b):
    return jnp.dot(a, b, preferred_element_type=F32)


def _params(*sem):
    return pltpu.CompilerParams(dimension_semantics=sem, vmem_limit_bytes=VMEM_LIMIT_BYTES)


def _rms(x, w):
    return x * lax.rsqrt(jnp.mean(x * x, axis=-1, keepdims=True) + EPS) * w


def _silu(x):
    return x * (0.5 + 0.5 * jnp.tanh(0.5 * x))


def _softplus(x):
    return jnp.maximum(x, 0.0) + jnp.log1p(jnp.exp(-jnp.abs(x)))


def _stage(u_ref, idx, u, st_ref, carry_ref, nb_ref, cols, *, n_sub, seg, tps, chunk=None):
    i = pl.program_id(0)
    j = pl.program_id(1) if chunk is None else chunk
    for s in range(n_sub):
        us = u[s * seg:(s + 1) * seg]
        halo = st_ref[s, :, cols]
        if tps > 1:
            halo = jnp.where((i % tps) == 0, halo, carry_ref[j, :, cols])
        base = s * (seg + SUBLANES)
        u_ref[idx + (pl.ds(base, SUBLANES), slice(None))] = halo
        u_ref[idx + (pl.ds(base + SUBLANES, seg), slice(None))] = us
        nb_ref[s, j, :, cols] = us[seg - SUBLANES:seg]
    if tps > 1:
        carry_ref[j, :, cols] = u[u.shape[0] - SUBLANES:]


def _staged_conv(u_ref, idx, w, *, n_sub, seg):
    width = w.shape[0]
    ys = []
    for s in range(n_sub):
        base = s * (seg + SUBLANES) + SUBLANES
        y = None
        for d in range(width):
            term = u_ref[idx + (pl.ds(base - d, seg), slice(None))] * w[width - 1 - d:width - d]
            y = term if y is None else y + term
        ys.append(y)
    return ys[0] if n_sub == 1 else jnp.concatenate(ys, axis=0)


def _staged(u_ref, idx, *, n_sub, seg):
    parts = [u_ref[idx + (pl.ds(s * (seg + SUBLANES) + SUBLANES, seg), slice(None))] for s in range(n_sub)]
    return parts[0] if n_sub == 1 else jnp.concatenate(parts, axis=0)


def _clear_carries(carry_refs):
    @pl.when(pl.program_id(0) == 0)
    def _():
        for ref in carry_refs:
            ref[...] = jnp.zeros(ref.shape, F32)


def _sub_chunks(tn):
    step = min(SUB_TILE, tn)
    return [slice(k, k + step) for k in range(0, tn, step)]


def _u_scratch(parts, n_sub, seg, tn):
    step = min(SUB_TILE, tn)
    return [pltpu.VMEM((2, n_sub * (seg + SUBLANES), step), F32) for _ in range(parts * (tn // step))]


def _sc_kernel(x_ref, nw_ref, wb_ref, wc_ref, wv_ref, cw_ref, st_ref, w2_ref, o_ref, nb_ref,
               h_ref, *scratch, n_sub, seg, tps):
    j = pl.program_id(1)
    carry_refs, stage_refs = (scratch[:1], scratch[1:]) if tps > 1 else ((), scratch)
    subs = _sub_chunks(wb_ref.shape[1])
    u_refs, b_refs = stage_refs[:len(subs)], stage_refs[len(subs):]

    @pl.when(j == 0)
    def _():
        x = x_ref[...]
        h_ref[...] = _rms(x, nw_ref[...]).astype(BF16)
        o_ref[...] = x
        _clear_carries(carry_refs)

    h = h_ref[...]
    slot = j % 2
    carry = carry_refs[0] if tps > 1 else None
    g_prev, cs_prev = None, None
    for k, cs in enumerate(subs):
        b_refs[k][slot] = _dot(h, wb_ref[:, cs])
        _stage(u_refs[k], (slot,), _dot(h, wc_ref[:, cs]) * _dot(h, wv_ref[:, cs]), st_ref, carry, nb_ref, cs,
               n_sub=n_sub, seg=seg, tps=tps)
        if g_prev is not None:
            o_ref[...] += _dot(g_prev, w2_ref[cs_prev, :])
        y = _staged_conv(u_refs[k], (slot,), cw_ref[:, cs], n_sub=n_sub, seg=seg)
        g_prev, cs_prev = (b_refs[k][slot] * y).astype(BF16), cs
    o_ref[...] += _dot(g_prev, w2_ref[cs_prev, :])


def _sc_layer(x, st8, norm_w, w_in, conv_w, w_out, *, n_seq, seg):
    rows, d = x.shape
    tm, n_sub, tps = _row_tiling(n_seq, seg)
    tn = _col_tile(d)
    nj = d // tn
    kern = functools.partial(_sc_kernel, n_sub=n_sub, seg=min(seg, tm), tps=tps)
    step = min(SUB_TILE, tn)
    scratch = [pltpu.VMEM((tm, d), BF16)]
    if tps > 1:
        scratch.append(pltpu.VMEM((nj, SUBLANES, tn), F32))
    scratch += _u_scratch(1, n_sub, min(seg, tm), tn)
    scratch += [pltpu.VMEM((2, tm, step), F32) for _ in range(tn // step)]
    out, nb = pl.pallas_call(
        kern,
        grid=(rows // tm, nj),
        in_specs=[
            pl.BlockSpec((tm, d), lambda i, j: (i, 0)),
            pl.BlockSpec((1, d), lambda i, j: (0, 0)),
            pl.BlockSpec((d, tn), lambda i, j: (0, j)),
            pl.BlockSpec((d, tn), lambda i, j: (0, nj + j)),
            pl.BlockSpec((d, tn), lambda i, j: (0, 2 * nj + j)),
            pl.BlockSpec((conv_w.shape[0], tn), lambda i, j: (0, j)),
            pl.BlockSpec((n_sub, SUBLANES, tn), lambda i, j: (i // tps, 0, j)),
            pl.BlockSpec((tn, d), lambda i, j: (j, 0)),
        ],
        out_specs=[
            pl.BlockSpec((tm, d), lambda i, j: (i, 0)),
            _nb_spec(n_sub, nj, tn, tps),
        ],
        out_shape=[jax.ShapeDtypeStruct((rows, d), F32), _nb_shape(n_seq, nj, tn)],
        scratch_shapes=scratch,
        compiler_params=_params("arbitrary", "arbitrary"),
        name="short_conv_mixer",
    )(x, norm_w, w_in, w_in, w_in, conv_w, st8, w_out)
    return out, _nb_merge(nb)


def _ffn_kernel(x_ref, nw_ref, wa_ref, wv_ref, cwa_ref, cwv_ref, ba_ref, bv_ref, sta_ref, stv_ref, w2_ref,
                fw_ref, o_ref, nba_ref, nbv_ref, h_ref, *scratch, n_sub, seg, tps, final_norm):
    j = pl.program_id(1)
    carry_refs, stage_refs = (scratch[:2], scratch[2:]) if tps > 1 else ((), scratch)
    ca, cv = carry_refs if tps > 1 else (None, None)
    subs = _sub_chunks(wa_ref.shape[1])
    ua_refs, uv_refs = stage_refs[:len(subs)], stage_refs[len(subs):]
    stage = functools.partial(_stage, n_sub=n_sub, seg=seg, tps=tps)
    conv = functools.partial(_staged_conv, n_sub=n_sub, seg=seg)

    @pl.when(j == 0)
    def _():
        x = x_ref[...]
        h_ref[...] = _rms(x, nw_ref[...]).astype(BF16)
        o_ref[...] = x
        _clear_carries(carry_refs)

    h = h_ref[...]
    slot = j % 2
    g_prev, cs_prev = None, None
    for k, cs in enumerate(subs):
        stage(ua_refs[k], (slot,), _dot(h, wa_ref[:, cs]), sta_ref, ca, nba_ref, cs)
        stage(uv_refs[k], (slot,), _dot(h, wv_ref[:, cs]), stv_ref, cv, nbv_ref, cs)
        if g_prev is not None:
            o_ref[...] += _dot(g_prev, w2_ref[cs_prev, :])
        a = conv(ua_refs[k], (slot,), cwa_ref[:, cs]) + ba_ref[:, cs]
        v = conv(uv_refs[k], (slot,), cwv_ref[:, cs]) + bv_ref[:, cs]
        g_prev, cs_prev = (_silu(a) * v).astype(BF16), cs
    o_ref[...] += _dot(g_prev, w2_ref[cs_prev, :])

    if final_norm:
        @pl.when(j == pl.num_programs(1) - 1)
        def _():
            o_ref[...] = _rms(o_ref[...], fw_ref[...])


def _ffn_layer(x, st8, norm_w, w_up, conv_w, conv_b, w_down, final_w, *, n_seq, seg, final_norm):
    rows, d = x.shape
    dff = w_down.shape[0]
    tm, n_sub, tps = _row_tiling(n_seq, seg)
    tn = _col_tile(dff)
    nj = dff // tn
    kern = functools.partial(_ffn_kernel, n_sub=n_sub, seg=min(seg, tm), tps=tps, final_norm=final_norm)
    scratch = [pltpu.VMEM((tm, d), BF16)]
    if tps > 1:
        scratch += [pltpu.VMEM((nj, SUBLANES, tn), F32), pltpu.VMEM((nj, SUBLANES, tn), F32)]
    scratch += _u_scratch(2, n_sub, min(seg, tm), tn)
    width = conv_w.shape[0]
    out, nba, nbv = pl.pallas_call(
        kern,
        grid=(rows // tm, nj),
        in_specs=[
            pl.BlockSpec((tm, d), lambda i, j: (i, 0)),
            pl.BlockSpec((1, d), lambda i, j: (0, 0)),
            pl.BlockSpec((d, tn), lambda i, j: (0, j)),
            pl.BlockSpec((d, tn), lambda i, j: (0, nj + j)),
            pl.BlockSpec((width, tn), lambda i, j: (0, j)),
            pl.BlockSpec((width, tn), lambda i, j: (0, nj + j)),
            pl.BlockSpec((1, tn), lambda i, j: (0, j)),
            pl.BlockSpec((1, tn), lambda i, j: (0, nj + j)),
            pl.BlockSpec((n_sub, SUBLANES, tn), lambda i, j: (i // tps, 0, j)),
            pl.BlockSpec((n_sub, SUBLANES, tn), lambda i, j: (i // tps, 0, nj + j)),
            pl.BlockSpec((tn, d), lambda i, j: (j, 0)),
            pl.BlockSpec((1, d), lambda i, j: (0, 0)),
        ],
        out_specs=[
            pl.BlockSpec((tm, d), lambda i, j: (i, 0)),
            _nb_spec(n_sub, nj, tn, tps),
            _nb_spec(n_sub, nj, tn, tps),
        ],
        out_shape=[jax.ShapeDtypeStruct((rows, d), F32), _nb_shape(n_seq, nj, tn), _nb_shape(n_seq, nj, tn)],
        scratch_shapes=scratch,
        compiler_params=_params("arbitrary", "arbitrary"),
        name="conv_ffn",
    )(x, norm_w, w_up, w_up, conv_w, conv_w, conv_b, conv_b, st8, st8, w_down, final_w)
    return out, jnp.concatenate([_nb_merge(nba), _nb_merge(nbv)], axis=-1)


def _ssd_in_kernel(x_ref, nw_ref, w_ref, cw_ref, cb_ref, st_ref, wdt_ref, dtb_ref, z_ref, o_ref, nb_ref, dt_ref,
                   h_ref, *scratch, n_sub, seg, tps, njx):
    s = pl.program_id(1)
    carry_refs, bufs = (scratch[:1], scratch[1:]) if tps > 1 else ((), scratch)
    carry = carry_refs[0] if tps > 1 else None
    tn = w_ref.shape[1]
    wide = [slice(k, k + tn // len(bufs)) for k in range(0, tn, tn // len(bufs))]

    @pl.when(s == 0)
    def _():
        h_ref[...] = _rms(x_ref[...], nw_ref[...]).astype(BF16)
        _clear_carries(carry_refs)

    @pl.when(s < njx)
    def _():
        h = h_ref[...]
        for k, cs in enumerate(wide):
            _stage(bufs[k], (), _dot(h, w_ref[:, cs]), st_ref, carry, nb_ref, cs,
                   n_sub=n_sub, seg=seg, tps=tps, chunk=s)
            y = _staged_conv(bufs[k], (), cw_ref[:, cs], n_sub=n_sub, seg=seg)
            o_ref[:, cs] = _silu(y + cb_ref[:, cs])

    @pl.when(s >= njx)
    def _():
        h = h_ref[...]
        for cs in _sub_chunks(tn):
            z_ref[:, cs] = _silu(_dot(h, w_ref[:, cs]))

    @pl.when(s == pl.num_programs(1) - 1)
    def _():
        dt_ref[...] = _softplus(_dot(h_ref[...], wdt_ref[...]) + dtb_ref[...])


def _ssd_in(x, st8, norm_w, w_in, conv_w, conv_b, dt_bias, *, n_seq, seg, inner, heads):
    rows, d = x.shape
    width, conv_dim = conv_w.shape
    cols = inner + conv_dim
    tm, n_sub, tps = _row_tiling(n_seq, seg, WIDE_ROW_TILE)
    tn = _col_tile(inner, WIDE_COL_TILE)
    assert conv_dim % tn == 0
    nz = inner // tn
    njx = conv_dim // tn
    assert njx >= 2
    w_blk = lambda s: jnp.where(s < njx, nz + s, s - njx)
    mm_blk = lambda s: jnp.minimum(s, njx - 1)
    cv_blk = mm_blk
    z_blk = lambda s: jnp.maximum(s - njx, 0)
    w_dt = w_in[:, cols:]
    seg_t = min(seg, tm)
    step = min(2 * SUB_TILE, tn)
    kern = functools.partial(_ssd_in_kernel, n_sub=n_sub, seg=seg_t, tps=tps, njx=njx)
    scratch = [pltpu.VMEM((tm, d), BF16)]
    if tps > 1:
        scratch.append(pltpu.VMEM((njx, SUBLANES, tn), F32))
    scratch += [pltpu.VMEM((n_sub * (seg_t + SUBLANES), step), F32) for _ in range(tn // step)]
    zs, xbc, nb, dt = pl.pallas_call(
        kern,
        grid=(rows // tm, nz + njx),
        in_specs=[
            pl.BlockSpec((tm, d), lambda i, s: (i, 0)),
            pl.BlockSpec((1, d), lambda i, s: (0, 0)),
            pl.BlockSpec((d, tn), lambda i, s: (0, w_blk(s))),
            pl.BlockSpec((width, tn), lambda i, s: (0, cv_blk(s))),
            pl.BlockSpec((1, tn), lambda i, s: (0, cv_blk(s))),
            pl.BlockSpec((n_sub, SUBLANES, tn), lambda i, s: (i // tps, 0, mm_blk(s))),
            pl.BlockSpec((d, heads), lambda i, s: (0, 0)),
            pl.BlockSpec((1, heads), lambda i, s: (0, 0)),
        ],
        out_specs=[
            pl.BlockSpec((tm, tn), lambda i, s: (i, z_blk(s))),
            pl.BlockSpec((tm, tn), lambda i, s: (i, cv_blk(s))),
            _nb_spec(n_sub, njx, tn, tps),
            pl.BlockSpec((tm, heads), lambda i, s: (i, 0)),
        ],
        out_shape=[jax.ShapeDtypeStruct((rows, inner), F32), jax.ShapeDtypeStruct((rows, conv_dim), F32),
                   _nb_shape(n_seq, njx, tn), jax.ShapeDtypeStruct((rows, heads), F32)],
        scratch_shapes=scratch,
        compiler_params=_params("arbitrary", "arbitrary"),
        name="ssd_in_proj",
    )(x, norm_w, w_in, conv_w, conv_b, st8, w_dt, dt_bias)
    return zs, xbc, dt, _nb_merge(nb)


def _split3(x):
    hi = x.astype(BF16).astype(F32)
    r = x - hi
    mid = r.astype(BF16).astype(F32)
    lo = (r - mid).astype(BF16).astype(F32)
    return hi, mid, lo


def _dot_f32_lhs(x, sel):
    hi, mid, lo = _split3(x)
    return _dot(hi.astype(BF16), sel) + _dot(mid.astype(BF16), sel) + _dot(lo.astype(BF16), sel)


def _dot_f32_rhs(sel, x):
    hi, mid, lo = _split3(x)
    return _dot(sel, hi.astype(BF16)) + _dot(sel, mid.astype(BF16)) + _dot(sel, lo.astype(BF16))


def _scan_kernel(zs_ref, xs_ref, b_ref, c_ref, dt3_ref, dtt_ref, a3_ref, at_ref, dsk_ref, nw_ref, st0_ref,
                 rep3_ref, tril_ref, tri2_ref, mask2_ref, bd_ref, y_ref, stout_ref, st_ref,
                 *, groups, hpg, q, n):
    c = pl.program_id(1)
    gw = hpg * q
    heads = groups * hpg

    @pl.when(c == 0)
    def _():
        st_ref[...] = st0_ref[0]

    dt3 = dt3_ref[...]
    cs3 = _dot_f32_rhs(tril_ref[...], dt3 * a3_ref[...])
    tail3 = jnp.exp(cs3[q - 1:q] - cs3) * dt3
    hi, mid, lo = _split3(jnp.concatenate([cs3, tail3], axis=0))
    lane = lax.broadcasted_iota(jnp.int32, hi.shape, 1)
    pieces = jnp.where(lane < heads, hi, jnp.where(lane < 2 * heads, mid, lo)).astype(BF16)
    bc = _dot(pieces, rep3_ref[...])
    cs_b, tail_b = bc[0:q], bc[q:2 * q]
    ecs_b = jnp.exp(cs_b)
    dtt = dtt_ref[0]
    cst = _dot_f32_lhs(dtt * at_ref[...], tri2_ref[...])
    mask2 = mask2_ref[...] > 0.0
    bd = bd_ref[...]

    gsl = [slice(g * gw, (g + 1) * gw) for g in range(groups)]
    cgs = [c_ref[:, g * n:(g + 1) * n].astype(BF16) for g in range(groups)]
    bgs = [b_ref[:, g * n:(g + 1) * n].astype(BF16) for g in range(groups)]
    cb2s = [lax.dot_general(cgs[g], jnp.concatenate([bgs[g], bgs[g]], axis=0), (((1,), (1,)), ((), ())),
                            preferred_element_type=F32) for g in range(groups)]
    y_inter = [_dot(cgs[g], st_ref[g].astype(BF16)) for g in range(groups)]
    ds = [lax.dot_general(bgs[g], (xs_ref[:, gsl[g]] * tail_b[:, gsl[g]]).astype(BF16), (((0,), (0,)), ((), ())),
                          preferred_element_type=F32) for g in range(groups)]
    ms, xbds = [], []
    for g in range(groups):
        for k in range(hpg // 2):
            psl = slice(g * gw + 2 * k * q, g * gw + 2 * (k + 1) * q)
            pair = g * (hpg // 2) + k
            seg = cs_b[:, psl] - jnp.broadcast_to(cst[pair:pair + 1], (q, 2 * q))
            w = jnp.where(mask2, jnp.exp(seg), 0.0) * jnp.broadcast_to(dtt[pair:pair + 1], (q, 2 * q))
            ms.append((cb2s[g] * w).astype(BF16))
            xk = xs_ref[:, psl]
            xbds.append((jnp.concatenate([xk, xk], axis=0) * bd).astype(BF16))
    parts = [_dot(m, xbd) for m, xbd in zip(ms, xbds)]
    for g in range(groups):
        st_ref[g] = st_ref[g] * ecs_b[q - 1:q, gsl[g]] + ds[g]
        pg = parts[g * (hpg // 2):(g + 1) * (hpg // 2)]
        y = pg[0] if len(pg) == 1 else jnp.concatenate(pg, axis=1)
        y = y + y_inter[g] * ecs_b[:, gsl[g]]
        y = (y + dsk_ref[:, gsl[g]] * xs_ref[:, gsl[g]]) * zs_ref[:, gsl[g]]
        y = y * lax.rsqrt(jnp.mean(y * y, axis=-1, keepdims=True) + EPS)
        y_ref[:, gsl[g]] = (y * nw_ref[:, gsl[g]]).astype(BF16)

    @pl.when(c == pl.num_programs(1) - 1)
    def _():
        stout_ref[0] = st_ref[...]


def _ssd_scan(zs, xbc, dt, st0, a_log, d_skip, norm_w, *, n_seq, seg, inner, groups, n, heads):
    rows = zs.shape[0]
    p = inner // heads
    q = p
    hpg = heads // groups
    assert seg % q == 0 and hpg % 2 == 0 and inner % (groups * n) == 0
    nc = seg // q
    gn = groups * n
    a = -jnp.exp(a_log.astype(F32))
    a3 = jnp.tile(a.reshape(1, heads), (1, 3))
    a_t = jnp.repeat(a.reshape(heads // 2, 2), q, axis=1)
    dt3 = jnp.tile(dt, (1, 3))
    dtt = dt.reshape(rows // q, q, heads // 2, 2).transpose(0, 2, 3, 1).reshape(rows // q, heads // 2, 2 * q)
    dsk = jnp.repeat(d_skip.astype(F32), p).reshape(1, inner)
    rep3 = jnp.tile(jnp.repeat(jnp.eye(heads, dtype=BF16), p, axis=1), (3, 1))
    tril = jnp.tril(jnp.ones((q, q), BF16))
    triu = jnp.triu(jnp.ones((q, q), F32))
    zero = jnp.zeros((q, q), F32)
    tri2 = jnp.block([[triu, zero], [zero, triu]]).astype(BF16)
    mask2 = jnp.concatenate([jnp.tril(jnp.ones((q, q), F32))] * 2, axis=1)
    one = jnp.ones((q, q), F32)
    bd = jnp.block([[one, zero], [zero, one]])

    kern = functools.partial(_scan_kernel, groups=groups, hpg=hpg, q=q, n=n)
    const = lambda s, c: (0, 0)
    y, st = pl.pallas_call(
        kern,
        grid=(n_seq, nc),
        in_specs=[
            pl.BlockSpec((q, inner), lambda s, c: (s * nc + c, 0)),
            pl.BlockSpec((q, inner), lambda s, c: (s * nc + c, 0)),
            pl.BlockSpec((q, gn), lambda s, c: (s * nc + c, inner // gn)),
            pl.BlockSpec((q, gn), lambda s, c: (s * nc + c, inner // gn + 1)),
            pl.BlockSpec((q, 3 * heads), lambda s, c: (s * nc + c, 0)),
            pl.BlockSpec((1, heads // 2, 2 * q), lambda s, c: (s * nc + c, 0, 0)),
            pl.BlockSpec((1, 3 * heads), const),
            pl.BlockSpec((heads // 2, 2 * q), const),
            pl.BlockSpec((1, inner), const),
            pl.BlockSpec((1, inner), const),
            pl.BlockSpec((1, groups, n, hpg * p), lambda s, c: (s, 0, 0, 0)),
            pl.BlockSpec(rep3.shape, const),
            pl.BlockSpec(tril.shape, const),
            pl.BlockSpec(tri2.shape, const),
            pl.BlockSpec(mask2.shape, const),
            pl.BlockSpec(bd.shape, const),
        ],
        out_specs=[
            pl.BlockSpec((q, inner), lambda s, c: (s * nc + c, 0)),
            pl.BlockSpec((1, groups, n, hpg * p), lambda s, c: (s, 0, 0, 0)),
        ],
        out_shape=[jax.ShapeDtypeStruct((rows, inner), BF16),
                   jax.ShapeDtypeStruct(st0.shape, F32)],
        scratch_shapes=[pltpu.VMEM((groups, n, hpg * p), F32)],
        compiler_params=_params("arbitrary", "arbitrary"),
        name="ssd_scan",
    )(zs, xbc, xbc, xbc, dt3, dtt, a3, a_t, dsk, norm_w, st0, rep3, tril, tri2, mask2, bd)
    return y, st


def _out_proj_kernel(x_ref, g_ref, w_ref, o_ref):
    o_ref[...] = x_ref[...] + _dot(g_ref[...], w_ref[...])


def _out_proj(x, g, w, *, n_seq, seg):
    rows, d = x.shape
    k = g.shape[1]
    tm, _, _ = _row_tiling(n_seq, seg)
    return pl.pallas_call(
        _out_proj_kernel,
        grid=(rows // tm,),
        in_specs=[
            pl.BlockSpec((tm, d), lambda i: (i, 0)),
            pl.BlockSpec((tm, k), lambda i: (i, 0)),
            pl.BlockSpec((k, d), lambda i: (0, 0), pipeline_mode=pl.Buffered(1)),
        ],
        out_specs=pl.BlockSpec((tm, d), lambda i: (i, 0)),
        out_shape=jax.ShapeDtypeStruct((rows, d), F32),
        compiler_params=_params("arbitrary"),
        name="ssd_out_proj",
    )(x, g, w)


def _row_tiling(n_seq, seg, row_tile=ROW_TILE):
    if seg >= row_tile:
        assert seg % row_tile == 0
        return row_tile, 1, seg // row_tile
    n_sub = max(1, min(n_seq, row_tile // seg))
    while n_seq % n_sub:
        n_sub -= 1
    return n_sub * seg, n_sub, 1


def _col_tile(width, pref=COL_TILE):
    tn = pref
    while width % tn:
        tn //= 2
    assert tn >= 128
    return tn


def _nb_spec(n_sub, nj, tn, tps):
    return pl.BlockSpec((n_sub, nj, SUBLANES, tn), lambda i, j: (i // tps, 0, 0, 0))


def _nb_shape(n_seq, nj, tn):
    return jax.ShapeDtypeStruct((n_seq, nj, SUBLANES, tn), F32)


def _nb_merge(nb):
    n_seq, nj, rows, tn = nb.shape
    return nb.transpose(0, 2, 1, 3).reshape(n_seq, rows, nj * tn)


def _pad8(buf):
    return jnp.pad(buf, ((0, 0), (SUBLANES - buf.shape[1], 0), (0, 0)))


def _state_to_kernel(s, groups):
    b, h, p, n = s.shape
    return s.reshape(b, groups, h // groups, p, n).transpose(0, 1, 4, 2, 3).reshape(b, groups, n, (h // groups) * p)


def _state_from_kernel(s, heads):
    b, g, n, w = s.shape
    hpg = heads // g
    return s.reshape(b, g, n, hpg, w // hpg).transpose(0, 1, 3, 4, 2).reshape(b, heads, w // hpg, n)


def _trunk(x, conv_a8, ssd_conv8, ssd_st, ffn8, wts, *, n_seq, seg):
    depth = wts["norm_mix"].shape[0]
    new_a, new_sc, new_st, new_f = [], [], [], []
    for i in range(depth):
        nm = wts["norm_mix"][i][None]
        j = i // 2
        if i % 2 == 0:
            x, nb = _sc_layer(x, conv_a8[j], nm, wts["sc_w_in"][j], wts["sc_conv_w"][j], wts["sc_w_out"][j],
                              n_seq=n_seq, seg=seg)
            new_a.append(nb)
        else:
            heads = wts["ssd_a_log"].shape[1]
            inner = wts["ssd_w_out"].shape[1]
            groups, n = ssd_st.shape[2], ssd_st.shape[3]
            zs, xbc, dt, nb = _ssd_in(x, ssd_conv8[j], nm, wts["ssd_w_in"][j], wts["ssd_conv_w"][j],
                                      wts["ssd_conv_b"][j][None], wts["ssd_dt_bias"][j][None],
                                      n_seq=n_seq, seg=seg, inner=inner, heads=heads)
            q = inner // heads
            pad = (-seg) % q
            if pad:
                padr = lambda t: jnp.pad(t.reshape(n_seq, seg, -1), ((0, 0), (0, pad), (0, 0))).reshape(
                    n_seq * (seg + pad), -1)
                zs, xbc, dt = padr(zs), padr(xbc), padr(dt)
            g, st = _ssd_scan(zs, xbc, dt, ssd_st[j], wts["ssd_a_log"][j], wts["ssd_d"][j],
                              wts["ssd_norm_w"][j][None], n_seq=n_seq, seg=seg + pad, inner=inner,
                              groups=groups, n=n, heads=heads)
            if pad:
                g = g.reshape(n_seq, seg + pad, -1)[:, :seg].reshape(n_seq * seg, -1)
            x = _out_proj(x, g, wts["ssd_w_out"][j], n_seq=n_seq, seg=seg)
            new_sc.append(nb)
            new_st.append(st)
        x, nf = _ffn_layer(x, ffn8[i], wts["norm_ffn"][i][None], wts["ffn_w_up"][i], wts["ffn_conv_w"][i],
                           wts["ffn_conv_b"][i][None], wts["ffn_w_down"][i], wts["norm_final"][None],
                           n_seq=n_seq, seg=seg, final_norm=(i == depth - 1))
        new_f.append(nf)
    return x, jnp.stack(new_a), jnp.stack(new_sc), jnp.stack(new_st), jnp.stack(new_f)


def kernel(x_prompt, x_sample, state_conv_a, state_ssd_conv, state_ssd, state_ffn_conv, meta_tokens, norm_mix,
           norm_ffn, norm_final, sc_w_in, sc_conv_w, sc_w_out, ssd_w_in, ssd_conv_w, ssd_conv_b, ssd_dt_bias,
           ssd_a_log, ssd_d, ssd_norm_w, ssd_w_out, ffn_w_up, ffn_conv_w, ffn_conv_b, ffn_w_down):
    b, seq, d = x_prompt.shape
    sb, sseq, _ = x_sample.shape
    n_meta = meta_tokens.shape[0]
    heads = ssd_a_log.shape[1]
    inner = ssd_w_out.shape[1]
    n_state = state_ssd.shape[-1]
    groups = (ssd_conv_w.shape[2] - inner) // (2 * n_state)
    wts = dict(
        norm_mix=norm_mix, norm_ffn=norm_ffn, norm_final=norm_final,
        sc_w_in=sc_w_in.astype(BF16), sc_conv_w=sc_conv_w, sc_w_out=sc_w_out.astype(BF16),
        ssd_w_in=ssd_w_in.astype(BF16), ssd_conv_w=ssd_conv_w, ssd_conv_b=ssd_conv_b, ssd_dt_bias=ssd_dt_bias,
        ssd_a_log=ssd_a_log, ssd_d=ssd_d, ssd_norm_w=ssd_norm_w, ssd_w_out=ssd_w_out.astype(BF16),
        ffn_w_up=ffn_w_up.astype(BF16), ffn_conv_w=ffn_conv_w, ffn_conv_b=ffn_conv_b,
        ffn_w_down=ffn_w_down.astype(BF16))

    def zeros8(like, n_seq):
        return jnp.zeros((like.shape[0], n_seq, SUBLANES, like.shape[-1]), F32)

    zero_st = jnp.zeros((state_ssd.shape[0], 1, groups, n_state, inner // groups), F32)
    _, m_a, m_sc, m_st, m_f = _trunk(
        meta_tokens.astype(F32), zeros8(state_conv_a, 1), zeros8(state_ssd_conv, 1), zero_st,
        zeros8(state_ffn_conv, 1), wts, n_seq=1, seg=n_meta)
    rep = lambda t: jnp.broadcast_to(t, (t.shape[0], b) + t.shape[2:])
    yp, p_a, p_sc, p_st, p_f = _trunk(
        x_prompt.reshape(b * seq, d), rep(m_a), rep(m_sc), rep(m_st), rep(m_f), wts, n_seq=b, seg=seq)
    ys, s_a, s_sc, s_st, s_f = _trunk(
        x_sample.reshape(sb * sseq, d), jax.vmap(_pad8)(state_conv_a), jax.vmap(_pad8)(state_ssd_conv),
        jax.vmap(lambda s: _state_to_kernel(s, groups))(state_ssd.astype(F32)), jax.vmap(_pad8)(state_ffn_conv),
        wts, n_seq=sb, seg=sseq)

    tail = lambda t, like: t[:, :, SUBLANES - like.shape[2]:, :]
    unst = jax.vmap(lambda s: _state_from_kernel(s, heads))
    return (yp.reshape(b, seq, d), ys.reshape(sb, sseq, d),
            tail(p_a, state_conv_a), tail(p_sc, state_ssd_conv), unst(p_st), tail(p_f, state_ffn_conv),
            tail(s_a, state_conv_a), tail(s_sc, state_ssd_conv), unst(s_st), tail(s_f, state_ffn_conv))
```

```python
import functools

import jax
import jax.numpy as jnp
from jax import lax
from jax.experimental import pallas as pl
from jax.experimental.pallas import tpu as pltpu

EPS = 1e-6
F32 = jnp.float32
BF16 = jnp.bfloat16

SUBLANES = 8
VMEM_LIMIT_BYTES = 56 * 1024 * 1024
ROW_TILE = 512
COL_TILE = 512
WIDE_ROW_TILE = 1024
WIDE_COL_TILE = 1024
SUB_TILE = 256


def _dot(a, b):
    return jnp.dot(a, b, preferred_element_type=F32)


def _params(*sem):
    return pltpu.CompilerParams(dimension_semantics=sem, vmem_limit_bytes=VMEM_LIMIT_BYTES)


def _rms(x, w):
    return x * lax.rsqrt(jnp.mean(x * x, axis=-1, keepdims=True) + EPS) * w


def _silu(x):
    return x * (0.5 + 0.5 * jnp.tanh(0.5 * x))


def _softplus(x):
    return jnp.maximum(x, 0.0) + jnp.log1p(jnp.exp(-jnp.abs(x)))


def _stage(u_ref, idx, u, st_ref, carry_ref, nb_ref, cols, *, n_sub, seg, tps, chunk=None):
    i = pl.program_id(0)
    j = pl.program_id(1) if chunk is None else chunk
    for s in range(n_sub):
        us = u[s * seg:(s + 1) * seg]
        halo = st_ref[s, :, cols]
        if tps > 1:
            halo = jnp.where((i % tps) == 0, halo, carry_ref[j, :, cols])
        base = s * (seg + SUBLANES)
        u_ref[idx + (pl.ds(base, SUBLANES), slice(None))] = halo
        u_ref[idx + (pl.ds(base + SUBLANES, seg), slice(None))] = us
        nb_ref[s, j, :, cols] = us[seg - SUBLANES:seg]
    if tps > 1:
        carry_ref[j, :, cols] = u[u.shape[0] - SUBLANES:]


def _staged_conv(u_ref, idx, w, *, n_sub, seg):
    width = w.shape[0]
    ys = []
    for s in range(n_sub):
        base = s * (seg + SUBLANES) + SUBLANES
        y = None
        for d in range(width):
            term = u_ref[idx + (pl.ds(base - d, seg), slice(None))] * w[width - 1 - d:width - d]
            y = term if y is None else y + term
        ys.append(y)
    return ys[0] if n_sub == 1 else jnp.concatenate(ys, axis=0)


def _staged(u_ref, idx, *, n_sub, seg):
    parts = [u_ref[idx + (pl.ds(s * (seg + SUBLANES) + SUBLANES, seg), slice(None))] for s in range(n_sub)]
    return parts[0] if n_sub == 1 else jnp.concatenate(parts, axis=0)


def _clear_carries(carry_refs):
    @pl.when(pl.program_id(0) == 0)
    def _():
        for ref in carry_refs:
            ref[...] = jnp.zeros(ref.shape, F32)


def _sub_chunks(tn):
    step = min(SUB_TILE, tn)
    return [slice(k, k + step) for k in range(0, tn, step)]


def _u_scratch(parts, n_sub, seg, tn):
    step = min(SUB_TILE, tn)
    return [pltpu.VMEM((2, n_sub * (seg + SUBLANES), step), F32) for _ in range(parts * (tn // step))]


def _sc_kernel(x_ref, nw_ref, wb_ref, wc_ref, wv_ref, cw_ref, st_ref, w2_ref, o_ref, nb_ref,
               h_ref, *scratch, n_sub, seg, tps):
    j = pl.program_id(1)
    carry_refs, stage_refs = (scratch[:1], scratch[1:]) if tps > 1 else ((), scratch)
    subs = _sub_chunks(wb_ref.shape[1])
    u_refs, b_refs = stage_refs[:len(subs)], stage_refs[len(subs):]

    @pl.when(j == 0)
    def _():
        x = x_ref[...]
        h_ref[...] = _rms(x, nw_ref[...]).astype(BF16)
        o_ref[...] = x
        _clear_carries(carry_refs)

    h = h_ref[...]
    slot = j % 2
    carry = carry_refs[0] if tps > 1 else None
    g_prev, cs_prev = None, None
    for k, cs in enumerate(subs):
        b_refs[k][slot] = _dot(h, wb_ref[:, cs])
        _stage(u_refs[k], (slot,), _dot(h, wc_ref[:, cs]) * _dot(h, wv_ref[:, cs]), st_ref, carry, nb_ref, cs,
               n_sub=n_sub, seg=seg, tps=tps)
        if g_prev is not None:
            o_ref[...] += _dot(g_prev, w2_ref[cs_prev, :])
        y = _staged_conv(u_refs[k], (slot,), cw_ref[:, cs], n_sub=n_sub, seg=seg)
        g_prev, cs_prev = (b_refs[k][slot] * y).astype(BF16), cs
    o_ref[...] += _dot(g_prev, w2_ref[cs_prev, :])


def _sc_layer(x, st8, norm_w, w_in, conv_w, w_out, *, n_seq, seg):
    rows, d = x.shape
    tm, n_sub, tps = _row_tiling(n_seq, seg)
    tn = _col_tile(d)
    nj = d // tn
    kern = functools.partial(_sc_kernel, n_sub=n_sub, seg=min(seg, tm), tps=tps)
    step = min(SUB_TILE, tn)
    scratch = [pltpu.VMEM((tm, d), BF16)]
    if tps > 1:
        scratch.append(pltpu.VMEM((nj, SUBLANES, tn), F32))
    scratch += _u_scratch(1, n_sub, min(seg, tm), tn)
    scratch += [pltpu.VMEM((2, tm, step), F32) for _ in range(tn // step)]
    out, nb = pl.pallas_call(
        kern,
        grid=(rows // tm, nj),
        in_specs=[
            pl.BlockSpec((tm, d), lambda i, j: (i, 0)),
            pl.BlockSpec((1, d), lambda i, j: (0, 0)),
            pl.BlockSpec((d, tn), lambda i, j: (0, j)),
            pl.BlockSpec((d, tn), lambda i, j: (0, nj + j)),
            pl.BlockSpec((d, tn), lambda i, j: (0, 2 * nj + j)),
            pl.BlockSpec((conv_w.shape[0], tn), lambda i, j: (0, j)),
            pl.BlockSpec((n_sub, SUBLANES, tn), lambda i, j: (i // tps, 0, j)),
            pl.BlockSpec((tn, d), lambda i, j: (j, 0)),
        ],
        out_specs=[
            pl.BlockSpec((tm, d), lambda i, j: (i, 0)),
            _nb_spec(n_sub, nj, tn, tps),
        ],
        out_shape=[jax.ShapeDtypeStruct((rows, d), F32), _nb_shape(n_seq, nj, tn)],
        scratch_shapes=scratch,
        compiler_params=_params("arbitrary", "arbitrary"),
        name="short_conv_mixer",
    )(x, norm_w, w_in, w_in, w_in, conv_w, st8, w_out)
    return out, _nb_merge(nb)


def _ffn_kernel(x_ref, nw_ref, wa_ref, wv_ref, cwa_ref, cwv_ref, ba_ref, bv_ref, sta_ref, stv_ref, w2_ref,
                fw_ref, o_ref, nba_ref, nbv_ref, h_ref, *scratch, n_sub, seg, tps, final_norm):
    j = pl.program_id(1)
    carry_refs, stage_refs = (scratch[:2], scratch[2:]) if tps > 1 else ((), scratch)
    ca, cv = carry_refs if tps > 1 else (None, None)
    subs = _sub_chunks(wa_ref.shape[1])
    ua_refs, uv_refs = stage_refs[:len(subs)], stage_refs[len(subs):]
    stage = functools.partial(_stage, n_sub=n_sub, seg=seg, tps=tps)
    conv = functools.partial(_staged_conv, n_sub=n_sub, seg=seg)

    @pl.when(j == 0)
    def _():
        x = x_ref[...]
        h_ref[...] = _rms(x, nw_ref[...]).astype(BF16)
        o_ref[...] = x
        _clear_carries(carry_refs)

    h = h_ref[...]
    slot = j % 2
    g_prev, cs_prev = None, None
    for k, cs in enumerate(subs):
        stage(ua_refs[k], (slot,), _dot(h, wa_ref[:, cs]), sta_ref, ca, nba_ref, cs)
        stage(uv_refs[k], (slot,), _dot(h, wv_ref[:, cs]), stv_ref, cv, nbv_ref, cs)
        if g_prev is not None:
            o_ref[...] += _dot(g_prev, w2_ref[cs_prev, :])
        a = conv(ua_refs[k], (slot,), cwa_ref[:, cs]) + ba_ref[:, cs]
        v = conv(uv_refs[k], (slot,), cwv_ref[:, cs]) + bv_ref[:, cs]
        g_prev, cs_prev = (_silu(a) * v).astype(BF16), cs
    o_ref[...] += _dot(g_prev, w2_ref[cs_prev, :])

    if final_norm:
        @pl.when(j == pl.num_programs(1) - 1)
        def _():
            o_ref[...] = _rms(o_ref[...], fw_ref[...])


def _ffn_layer(x, st8, norm_w, w_up, conv_w, conv_b, w_down, final_w, *, n_seq, seg, final_norm):
    rows, d = x.shape
    dff = w_down.shape[0]
    tm, n_sub, tps = _row_tiling(n_seq, seg)
    tn = _col_tile(dff)
    nj = dff // tn
    kern = functools.partial(_ffn_kernel, n_sub=n_sub, seg=min(seg, tm), tps=tps, final_norm=final_norm)
    scratch = [pltpu.VMEM((tm, d), BF16)]
    if tps > 1:
        scratch += [pltpu.VMEM((nj, SUBLANES, tn), F32), pltpu.VMEM((nj, SUBLANES, tn), F32)]
    scratch += _u_scratch(2, n_sub, min(seg, tm), tn)
    width = conv_w.shape[0]
    out, nba, nbv = pl.pallas_call(
        kern,
        grid=(rows // tm, nj),
        in_specs=[
            pl.BlockSpec((tm, d), lambda i, j: (i, 0)),
            pl.BlockSpec((1, d), lambda i, j: (0, 0)),
            pl.BlockSpec((d, tn), lambda i, j: (0, j)),
            pl.BlockSpec((d, tn), lambda i, j: (0, nj + j)),
            pl.BlockSpec((width, tn), lambda i, j: (0, j)),
            pl.BlockSpec((width, tn), lambda i, j: (0, nj + j)),
            pl.BlockSpec((1, tn), lambda i, j: (0, j)),
            pl.BlockSpec((1, tn), lambda i, j: (0, nj + j)),
            pl.BlockSpec((n_sub, SUBLANES, tn), lambda i, j: (i // tps, 0, j)),
            pl.BlockSpec((n_sub, SUBLANES, tn), lambda i, j: (i // tps, 0, nj + j)),
            pl.BlockSpec((tn, d), lambda i, j: (j, 0)),
            pl.BlockSpec((1, d), lambda i, j: (0, 0)),
        ],
        out_specs=[
            pl.BlockSpec((tm, d), lambda i, j: (i, 0)),
            _nb_spec(n_sub, nj, tn, tps),
            _nb_spec(n_sub, nj, tn, tps),
        ],
        out_shape=[jax.ShapeDtypeStruct((rows, d), F32), _nb_shape(n_seq, nj, tn), _nb_shape(n_seq, nj, tn)],
        scratch_shapes=scratch,
        compiler_params=_params("arbitrary", "arbitrary"),
        name="conv_ffn",
    )(x, norm_w, w_up, w_up, conv_w, conv_w, conv_b, conv_b, st8, st8, w_down, final_w)
    return out, jnp.concatenate([_nb_merge(nba), _nb_merge(nbv)], axis=-1)


def _ssd_in_kernel(x_ref, nw_ref, w_ref, cw_ref, cb_ref, st_ref, wdt_ref, dtb_ref, z_ref, o_ref, nb_ref, dt_ref,
                   h_ref, *scratch, n_sub, seg, tps, njx):
    s = pl.program_id(1)
    carry_refs, bufs = (scratch[:1], scratch[1:]) if tps > 1 else ((), scratch)
    carry = carry_refs[0] if tps > 1 else None
    tn = w_ref.shape[1]
    wide = [slice(k, k + tn // len(bufs)) for k in range(0, tn, tn // len(bufs))]

    @pl.when(s == 0)
    def _():
        h_ref[...] = _rms(x_ref[...], nw_ref[...]).astype(BF16)
        _clear_carries(carry_refs)

    @pl.when(s < njx)
    def _():
        h = h_ref[...]
        for k, cs in enumerate(wide):
            _stage(bufs[k], (), _dot(h, w_ref[:, cs]), st_ref, carry, nb_ref, cs,
                   n_sub=n_sub, seg=seg, tps=tps, chunk=s)
            y = _staged_conv(bufs[k], (), cw_ref[:, cs], n_sub=n_sub, seg=seg)
            o_ref[:, cs] = _silu(y + cb_ref[:, cs])

    @pl.when(s >= njx)
    def _():
        h = h_ref[...]
        for cs in _sub_chunks(tn):
            z_ref[:, cs] = _silu(_dot(h, w_ref[:, cs]))

    @pl.when(s == pl.num_programs(1) - 1)
    def _():
        dt_ref[...] = _softplus(_dot(h_ref[...], wdt_ref[...]) + dtb_ref[...])


def _ssd_in(x, st8, norm_w, w_in, conv_w, conv_b, dt_bias, *, n_seq, seg, inner, heads):
    rows, d = x.shape
    width, conv_dim = conv_w.shape
    cols = inner + conv_dim
    tm, n_sub, tps = _row_tiling(n_seq, seg, WIDE_ROW_TILE if seg >= WIDE_ROW_TILE else ROW_TILE)
    tn = _col_tile(inner, WIDE_COL_TILE)
    assert conv_dim % tn == 0
    nz = inner // tn
    njx = conv_dim // tn
    assert njx >= 2
    w_blk = lambda s: jnp.where(s < njx, nz + s, s - njx)
    mm_blk = lambda s: jnp.minimum(s, njx - 1)
    cv_blk = mm_blk
    z_blk = lambda s: jnp.maximum(s - njx, 0)
    w_dt = w_in[:, cols:]
    seg_t = min(seg, tm)
    step = min(2 * SUB_TILE, tn)
    kern = functools.partial(_ssd_in_kernel, n_sub=n_sub, seg=seg_t, tps=tps, njx=njx)
    scratch = [pltpu.VMEM((tm, d), BF16)]
    if tps > 1:
        scratch.append(pltpu.VMEM((njx, SUBLANES, tn), F32))
    scratch += [pltpu.VMEM((n_sub * (seg_t + SUBLANES), step), F32) for _ in range(tn // step)]
    zs, xbc, nb, dt = pl.pallas_call(
        kern,
        grid=(rows // tm, nz + njx),
        in_specs=[
            pl.BlockSpec((tm, d), lambda i, s: (i, 0)),
            pl.BlockSpec((1, d), lambda i, s: (0, 0)),
            pl.BlockSpec((d, tn), lambda i, s: (0, w_blk(s))),
            pl.BlockSpec((width, tn), lambda i, s: (0, cv_blk(s))),
            pl.BlockSpec((1, tn), lambda i, s: (0, cv_blk(s))),
            pl.BlockSpec((n_sub, SUBLANES, tn), lambda i, s: (i // tps, 0, mm_blk(s))),
            pl.BlockSpec((d, heads), lambda i, s: (0, 0)),
            pl.BlockSpec((1, heads), lambda i, s: (0, 0)),
        ],
        out_specs=[
            pl.BlockSpec((tm, tn), lambda i, s: (i, z_blk(s))),
            pl.BlockSpec((tm, tn), lambda i, s: (i, cv_blk(s))),
            _nb_spec(n_sub, njx, tn, tps),
            pl.BlockSpec((tm, heads), lambda i, s: (i, 0)),
        ],
        out_shape=[jax.ShapeDtypeStruct((rows, inner), F32), jax.ShapeDtypeStruct((rows, conv_dim), F32),
                   _nb_shape(n_seq, njx, tn), jax.ShapeDtypeStruct((rows, heads), F32)],
        scratch_shapes=scratch,
        compiler_params=_params("arbitrary", "arbitrary"),
        name="ssd_in_proj",
    )(x, norm_w, w_in, conv_w, conv_b, st8, w_dt, dt_bias)
    return zs, xbc, dt, _nb_merge(nb)


def _split3(x):
    hi = x.astype(BF16).astype(F32)
    r = x - hi
    mid = r.astype(BF16).astype(F32)
    lo = (r - mid).astype(BF16).astype(F32)
    return hi, mid, lo


def _dot_f32_lhs(x, sel):
    hi, mid, lo = _split3(x)
    return _dot(hi.astype(BF16), sel) + _dot(mid.astype(BF16), sel) + _dot(lo.astype(BF16), sel)


def _dot_f32_rhs(sel, x):
    hi, mid, lo = _split3(x)
    return _dot(sel, hi.astype(BF16)) + _dot(sel, mid.astype(BF16)) + _dot(sel, lo.astype(BF16))


def _scan_kernel(zs_ref, xs_ref, b_ref, c_ref, dt3_ref, dtt_ref, a3_ref, at_ref, dsk_ref, nw_ref, st0_ref,
                 rep3_ref, tril_ref, tri2_ref, mask2_ref, bd_ref, y_ref, stout_ref, st_ref,
                 *, groups, hpg, q, n):
    c = pl.program_id(1)
    gw = hpg * q
    heads = groups * hpg

    @pl.when(c == 0)
    def _():
        for g in range(groups):
            st_ref[g] = st0_ref[0, g].T

    dt3 = dt3_ref[...]
    cs3 = _dot_f32_rhs(tril_ref[...], dt3 * a3_ref[...])
    tail3 = jnp.exp(cs3[q - 1:q] - cs3) * dt3
    hi, mid, lo = _split3(jnp.concatenate([cs3, tail3], axis=0))
    lane = lax.broadcasted_iota(jnp.int32, hi.shape, 1)
    pieces = jnp.where(lane < heads, hi, jnp.where(lane < 2 * heads, mid, lo)).astype(BF16)
    bc = _dot(pieces, rep3_ref[...])
    cs_b, tail_b = bc[0:q], bc[q:2 * q]
    ecs_b = jnp.exp(cs_b)
    dtt = dtt_ref[0]
    cst = _dot_f32_lhs(dtt * at_ref[...], tri2_ref[...])
    mask2 = mask2_ref[...] > 0.0
    bd = bd_ref[...]

    gsl = [slice(g * gw, (g + 1) * gw) for g in range(groups)]
    cgs = [c_ref[:, g * n:(g + 1) * n].astype(BF16) for g in range(groups)]
    bgs = [b_ref[:, g * n:(g + 1) * n].astype(BF16) for g in range(groups)]
    cb2s = [lax.dot_general(cgs[g], jnp.concatenate([bgs[g], bgs[g]], axis=0), (((1,), (1,)), ((), ())),
                            preferred_element_type=F32) for g in range(groups)]
    y_inter = [_dot(cgs[g], st_ref[g].astype(BF16)) for g in range(groups)]
    ds = [lax.dot_general(bgs[g], (xs_ref[:, gsl[g]] * tail_b[:, gsl[g]]).astype(BF16), (((0,), (0,)), ((), ())),
                          preferred_element_type=F32) for g in range(groups)]
    ms, xbds = [], []
    for g in range(groups):
        for k in range(hpg // 2):
            psl = slice(g * gw + 2 * k * q, g * gw + 2 * (k + 1) * q)
            pair = g * (hpg // 2) + k
            seg = cs_b[:, psl] - jnp.broadcast_to(cst[pair:pair + 1], (q, 2 * q))
            w = jnp.where(mask2, jnp.exp(seg), 0.0) * jnp.broadcast_to(dtt[pair:pair + 1], (q, 2 * q))
            ms.append((cb2s[g] * w).astype(BF16))
            xk = xs_ref[:, psl]
            xbds.append((jnp.concatenate([xk, xk], axis=0) * bd).astype(BF16))
    parts = [_dot(m, xbd) for m, xbd in zip(ms, xbds)]
    for g in range(groups):
        st_ref[g] = st_ref[g] * ecs_b[q - 1:q, gsl[g]] + ds[g]
        pg = parts[g * (hpg // 2):(g + 1) * (hpg // 2)]
        y = pg[0] if len(pg) == 1 else jnp.concatenate(pg, axis=1)
        y = y + y_inter[g] * ecs_b[:, gsl[g]]
        y = (y + dsk_ref[:, gsl[g]] * xs_ref[:, gsl[g]]) * zs_ref[:, gsl[g]]
        y = y * lax.rsqrt(jnp.mean(y * y, axis=-1, keepdims=True) + EPS)
        y_ref[:, gsl[g]] = (y * nw_ref[:, gsl[g]]).astype(BF16)

    @pl.when(c == pl.num_programs(1) - 1)
    def _():
        for g in range(groups):
            stout_ref[0, g] = st_ref[g].T


def _ssd_scan(zs, xbc, dt, st0, a_log, d_skip, norm_w, *, n_seq, seg, inner, groups, n, heads):
    rows = zs.shape[0]
    p = inner // heads
    q = p
    hpg = heads // groups
    assert seg % q == 0 and hpg % 2 == 0 and inner % (groups * n) == 0
    nc = seg // q
    gn = groups * n
    a = -jnp.exp(a_log.astype(F32))
    a3 = jnp.tile(a.reshape(1, heads), (1, 3))
    a_t = jnp.repeat(a.reshape(heads // 2, 2), q, axis=1)
    dt3 = jnp.tile(dt, (1, 3))
    dtt = dt.reshape(rows // q, q, heads // 2, 2).transpose(0, 2, 3, 1).reshape(rows // q, heads // 2, 2 * q)
    dsk = jnp.repeat(d_skip.astype(F32), p).reshape(1, inner)
    rep3 = jnp.tile(jnp.repeat(jnp.eye(heads, dtype=BF16), p, axis=1), (3, 1))
    tril = jnp.tril(jnp.ones((q, q), BF16))
    triu = jnp.triu(jnp.ones((q, q), F32))
    zero = jnp.zeros((q, q), F32)
    tri2 = jnp.block([[triu, zero], [zero, triu]]).astype(BF16)
    mask2 = jnp.concatenate([jnp.tril(jnp.ones((q, q), F32))] * 2, axis=1)
    one = jnp.ones((q, q), F32)
    bd = jnp.block([[one, zero], [zero, one]])

    kern = functools.partial(_scan_kernel, groups=groups, hpg=hpg, q=q, n=n)
    const = lambda s, c: (0, 0)
    y, st = pl.pallas_call(
        kern,
        grid=(n_seq, nc),
        in_specs=[
            pl.BlockSpec((q, inner), lambda s, c: (s * nc + c, 0)),
            pl.BlockSpec((q, inner), lambda s, c: (s * nc + c, 0)),
            pl.BlockSpec((q, gn), lambda s, c: (s * nc + c, inner // gn)),
            pl.BlockSpec((q, gn), lambda s, c: (s * nc + c, inner // gn + 1)),
            pl.BlockSpec((q, 3 * heads), lambda s, c: (s * nc + c, 0)),
            pl.BlockSpec((1, heads // 2, 2 * q), lambda s, c: (s * nc + c, 0, 0)),
            pl.BlockSpec((1, 3 * heads), const),
            pl.BlockSpec((heads // 2, 2 * q), const),
            pl.BlockSpec((1, inner), const),
            pl.BlockSpec((1, inner), const),
            pl.BlockSpec((1, groups, hpg * p, n), lambda s, c: (s, 0, 0, 0)),
            pl.BlockSpec(rep3.shape, const),
            pl.BlockSpec(tril.shape, const),
            pl.BlockSpec(tri2.shape, const),
            pl.BlockSpec(mask2.shape, const),
            pl.BlockSpec(bd.shape, const),
        ],
        out_specs=[
            pl.BlockSpec((q, inner), lambda s, c: (s * nc + c, 0)),
            pl.BlockSpec((1, groups, hpg * p, n), lambda s, c: (s, 0, 0, 0)),
        ],
        out_shape=[jax.ShapeDtypeStruct((rows, inner), BF16),
                   jax.ShapeDtypeStruct(st0.shape, F32)],
        scratch_shapes=[pltpu.VMEM((groups, n, hpg * p), F32)],
        compiler_params=_params("arbitrary", "arbitrary"),
        name="ssd_scan",
    )(zs, xbc, xbc, xbc, dt3, dtt, a3, a_t, dsk, norm_w, st0, rep3, tril, tri2, mask2, bd)
    return y, st


def _out_proj_kernel(x_ref, g_ref, w_ref, o_ref):
    o_ref[...] = x_ref[...] + _dot(g_ref[...], w_ref[...])


def _out_proj(x, g, w, *, n_seq, seg):
    rows, d = x.shape
    k = g.shape[1]
    tm, _, _ = _row_tiling(n_seq, seg)
    return pl.pallas_call(
        _out_proj_kernel,
        grid=(rows // tm,),
        in_specs=[
            pl.BlockSpec((tm, d), lambda i: (i, 0)),
            pl.BlockSpec((tm, k), lambda i: (i, 0)),
            pl.BlockSpec((k, d), lambda i: (0, 0), pipeline_mode=pl.Buffered(1)),
        ],
        out_specs=pl.BlockSpec((tm, d), lambda i: (i, 0)),
        out_shape=jax.ShapeDtypeStruct((rows, d), F32),
        compiler_params=_params("arbitrary"),
        name="ssd_out_proj",
    )(x, g, w)


def _row_tiling(n_seq, seg, row_tile=ROW_TILE):
    if seg >= row_tile:
        assert seg % row_tile == 0
        return row_tile, 1, seg // row_tile
    n_sub = max(1, min(n_seq, row_tile // seg))
    while n_seq % n_sub:
        n_sub -= 1
    return n_sub * seg, n_sub, 1


def _col_tile(width, pref=COL_TILE):
    tn = pref
    while width % tn:
        tn //= 2
    assert tn >= 128
    return tn


def _nb_spec(n_sub, nj, tn, tps):
    return pl.BlockSpec((n_sub, nj, SUBLANES, tn), lambda i, j: (i // tps, 0, 0, 0))


def _nb_shape(n_seq, nj, tn):
    return jax.ShapeDtypeStruct((n_seq, nj, SUBLANES, tn), F32)


def _nb_merge(nb):
    n_seq, nj, rows, tn = nb.shape
    return nb.transpose(0, 2, 1, 3).reshape(n_seq, rows, nj * tn)


def _pad8(buf):
    return jnp.pad(buf, ((0, 0), (SUBLANES - buf.shape[1], 0), (0, 0)))


def _state_to_kernel(s, groups):
    b, h, p, n = s.shape
    return s.reshape(b, groups, (h // groups) * p, n)


def _state_from_kernel(s, heads):
    b, g, w, n = s.shape
    return s.reshape(b, heads, (g * w) // heads, n)


def _trunk(x, conv_a8, ssd_conv8, ssd_st, ffn8, wts, *, n_seq, seg):
    depth = wts["norm_mix"].shape[0]
    new_a, new_sc, new_st, new_f = [], [], [], []
    for i in range(depth):
        nm = wts["norm_mix"][i][None]
        j = i // 2
        if i % 2 == 0:
            x, nb = _sc_layer(x, conv_a8[j], nm, wts["sc_w_in"][j], wts["sc_conv_w"][j], wts["sc_w_out"][j],
                              n_seq=n_seq, seg=seg)
            new_a.append(nb)
        else:
            heads = wts["ssd_a_log"].shape[1]
            inner = wts["ssd_w_out"].shape[1]
            groups, n = ssd_st.shape[2], ssd_st.shape[4]
            zs, xbc, dt, nb = _ssd_in(x, ssd_conv8[j], nm, wts["ssd_w_in"][j], wts["ssd_conv_w"][j],
                                      wts["ssd_conv_b"][j][None], wts["ssd_dt_bias"][j][None],
                                      n_seq=n_seq, seg=seg, inner=inner, heads=heads)
            q = inner // heads
            pad = (-seg) % q
            if pad:
                padr = lambda t: jnp.pad(t.reshape(n_seq, seg, -1), ((0, 0), (0, pad), (0, 0))).reshape(
                    n_seq * (seg + pad), -1)
                zs, xbc, dt = padr(zs), padr(xbc), padr(dt)
            g, st = _ssd_scan(zs, xbc, dt, ssd_st[j], wts["ssd_a_log"][j], wts["ssd_d"][j],
                              wts["ssd_norm_w"][j][None], n_seq=n_seq, seg=seg + pad, inner=inner,
                              groups=groups, n=n, heads=heads)
            if pad:
                g = g.reshape(n_seq, seg + pad, -1)[:, :seg].reshape(n_seq * seg, -1)
            x = _out_proj(x, g, wts["ssd_w_out"][j], n_seq=n_seq, seg=seg)
            new_sc.append(nb)
            new_st.append(st)
        x, nf = _ffn_layer(x, ffn8[i], wts["norm_ffn"][i][None], wts["ffn_w_up"][i], wts["ffn_conv_w"][i],
                           wts["ffn_conv_b"][i][None], wts["ffn_w_down"][i], wts["norm_final"][None],
                           n_seq=n_seq, seg=seg, final_norm=(i == depth - 1))
        new_f.append(nf)
    return x, jnp.stack(new_a), jnp.stack(new_sc), jnp.stack(new_st), jnp.stack(new_f)


def kernel(x_prompt, x_sample, state_conv_a, state_ssd_conv, state_ssd, state_ffn_conv, meta_tokens, norm_mix,
           norm_ffn, norm_final, sc_w_in, sc_conv_w, sc_w_out, ssd_w_in, ssd_conv_w, ssd_conv_b, ssd_dt_bias,
           ssd_a_log, ssd_d, ssd_norm_w, ssd_w_out, ffn_w_up, ffn_conv_w, ffn_conv_b, ffn_w_down):
    b, seq, d = x_prompt.shape
    sb, sseq, _ = x_sample.shape
    n_meta = meta_tokens.shape[0]
    heads = ssd_a_log.shape[1]
    inner = ssd_w_out.shape[1]
    n_state = state_ssd.shape[-1]
    groups = (ssd_conv_w.shape[2] - inner) // (2 * n_state)
    wts = dict(
        norm_mix=norm_mix, norm_ffn=norm_ffn, norm_final=norm_final,
        sc_w_in=sc_w_in.astype(BF16), sc_conv_w=sc_conv_w, sc_w_out=sc_w_out.astype(BF16),
        ssd_w_in=ssd_w_in.astype(BF16), ssd_conv_w=ssd_conv_w, ssd_conv_b=ssd_conv_b, ssd_dt_bias=ssd_dt_bias,
        ssd_a_log=ssd_a_log, ssd_d=ssd_d, ssd_norm_w=ssd_norm_w, ssd_w_out=ssd_w_out.astype(BF16),
        ffn_w_up=ffn_w_up.astype(BF16), ffn_conv_w=ffn_conv_w, ffn_conv_b=ffn_conv_b,
        ffn_w_down=ffn_w_down.astype(BF16))

    def zeros8(like, n_seq):
        return jnp.zeros((like.shape[0], n_seq, SUBLANES, like.shape[-1]), F32)

    zero_st = jnp.zeros((state_ssd.shape[0], 1, groups, inner // groups, n_state), F32)
    _, m_a, m_sc, m_st, m_f = _trunk(
        meta_tokens.astype(F32), zeros8(state_conv_a, 1), zeros8(state_ssd_conv, 1), zero_st,
        zeros8(state_ffn_conv, 1), wts, n_seq=1, seg=n_meta)
    rep = lambda t: jnp.broadcast_to(t, (t.shape[0], b) + t.shape[2:])
    yp, p_a, p_sc, p_st, p_f = _trunk(
        x_prompt.reshape(b * seq, d), rep(m_a), rep(m_sc), rep(m_st), rep(m_f), wts, n_seq=b, seg=seq)
    ys, s_a, s_sc, s_st, s_f = _trunk(
        x_sample.reshape(sb * sseq, d), jax.vmap(_pad8)(state_conv_a), jax.vmap(_pad8)(state_ssd_conv),
        jax.vmap(lambda s: _state_to_kernel(s, groups))(state_ssd.astype(F32)), jax.vmap(_pad8)(state_ffn_conv),
        wts, n_seq=sb, seg=sseq)

    tail = lambda t, like: t[:, :, SUBLANES - like.shape[2]:, :]
    unst = jax.vmap(lambda s: _state_from_kernel(s, heads))
    return (yp.reshape(b, seq, d), ys.reshape(sb, sseq, d),
            tail(p_a, state_conv_a), tail(p_sc, state_ssd_conv), unst(p_st), tail(p_f, state_ffn_conv),
            tail(s_a, state_conv_a), tail(s_sc, state_ssd_conv), unst(s_st), tail(s_f, state_ffn_conv))
```

```python
import functools

import jax
import jax.numpy as jnp
from jax import lax
from jax.experimental import pallas as pl
from jax.experimental.pallas import tpu as pltpu

EPS = 1e-6
F32 = jnp.float32
BF16 = jnp.bfloat16

SUBLANES = 8
VMEM_LIMIT_BYTES = 58 * 1024 * 1024
ROW_TILE = 512
COL_TILE = 512
WIDE_ROW_TILE = 1024
WIDE_COL_TILE = 1024
SUB_TILE = 256


def _dot(a, b):
    return jnp.dot(a, b, preferred_element_type=F32)


def _params(*sem):
    return pltpu.CompilerParams(dimension_semantics=sem, vmem_limit_bytes=VMEM_LIMIT_BYTES)


def _rms(x, w):
    return x * lax.rsqrt(jnp.mean(x * x, axis=-1, keepdims=True) + EPS) * w


def _silu(x):
    return x * (0.5 + 0.5 * jnp.tanh(0.5 * x))


def _softplus(x):
    return jnp.maximum(x, 0.0) + jnp.log1p(jnp.exp(-jnp.abs(x)))


def _stage(u_ref, idx, u, st_ref, carry_ref, nb_ref, cols, *, n_sub, seg, tps, chunk=None):
    i = pl.program_id(0)
    j = pl.program_id(1) if chunk is None else chunk
    for s in range(n_sub):
        us = u[s * seg:(s + 1) * seg]
        halo = st_ref[s, :, cols]
        if tps > 1:
            halo = jnp.where((i % tps) == 0, halo, carry_ref[j, :, cols])
        base = s * (seg + SUBLANES)
        u_ref[idx + (pl.ds(base, SUBLANES), slice(None))] = halo
        u_ref[idx + (pl.ds(base + SUBLANES, seg), slice(None))] = us
        nb_ref[s, j, :, cols] = us[seg - SUBLANES:seg]
    if tps > 1:
        carry_ref[j, :, cols] = u[u.shape[0] - SUBLANES:]


def _staged_conv(u_ref, idx, w, *, n_sub, seg):
    width = w.shape[0]
    ys = []
    for s in range(n_sub):
        base = s * (seg + SUBLANES) + SUBLANES
        y = None
        for d in range(width):
            term = u_ref[idx + (pl.ds(base - d, seg), slice(None))] * w[width - 1 - d:width - d]
            y = term if y is None else y + term
        ys.append(y)
    return ys[0] if n_sub == 1 else jnp.concatenate(ys, axis=0)


def _staged(u_ref, idx, *, n_sub, seg):
    parts = [u_ref[idx + (pl.ds(s * (seg + SUBLANES) + SUBLANES, seg), slice(None))] for s in range(n_sub)]
    return parts[0] if n_sub == 1 else jnp.concatenate(parts, axis=0)


def _clear_carries(carry_refs):
    @pl.when(pl.program_id(0) == 0)
    def _():
        for ref in carry_refs:
            ref[...] = jnp.zeros(ref.shape, F32)


def _sub_chunks(tn):
    step = min(SUB_TILE, tn)
    return [slice(k, k + step) for k in range(0, tn, step)]


def _u_scratch(parts, n_sub, seg, tn):
    step = min(SUB_TILE, tn)
    return [pltpu.VMEM((2, n_sub * (seg + SUBLANES), step), F32) for _ in range(parts * (tn // step))]


def _sc_kernel(x_ref, nw_ref, wb_ref, wc_ref, wv_ref, cw_ref, st_ref, w2_ref, o_ref, nb_ref,
               h_ref, *scratch, n_sub, seg, tps):
    j = pl.program_id(1)
    carry_refs, stage_refs = (scratch[:1], scratch[1:]) if tps > 1 else ((), scratch)
    subs = _sub_chunks(wb_ref.shape[1])
    u_refs, b_refs = stage_refs[:len(subs)], stage_refs[len(subs):]

    @pl.when(j == 0)
    def _():
        x = x_ref[...]
        h_ref[...] = _rms(x, nw_ref[...]).astype(BF16)
        o_ref[...] = x
        _clear_carries(carry_refs)

    h = h_ref[...]
    slot = j % 2
    carry = carry_refs[0] if tps > 1 else None
    g_prev, cs_prev = None, None
    for k, cs in enumerate(subs):
        b_refs[k][slot] = _dot(h, wb_ref[:, cs])
        _stage(u_refs[k], (slot,), _dot(h, wc_ref[:, cs]) * _dot(h, wv_ref[:, cs]), st_ref, carry, nb_ref, cs,
               n_sub=n_sub, seg=seg, tps=tps)
        if g_prev is not None:
            o_ref[...] += _dot(g_prev, w2_ref[cs_prev, :])
        y = _staged_conv(u_refs[k], (slot,), cw_ref[:, cs], n_sub=n_sub, seg=seg)
        g_prev, cs_prev = (b_refs[k][slot] * y).astype(BF16), cs
    o_ref[...] += _dot(g_prev, w2_ref[cs_prev, :])


def _sc_layer(x, st8, norm_w, w_in, conv_w, w_out, *, n_seq, seg):
    rows, d = x.shape
    tm, n_sub, tps = _row_tiling(n_seq, seg)
    x_spec = pl.BlockSpec((tm, d), lambda i, j: (i, 0))
    tn = _col_tile(d)
    nj = d // tn
    kern = functools.partial(_sc_kernel, n_sub=n_sub, seg=min(seg, tm), tps=tps)
    step = min(SUB_TILE, tn)
    scratch = [pltpu.VMEM((tm, d), BF16)]
    if tps > 1:
        scratch.append(pltpu.VMEM((nj, SUBLANES, tn), F32))
    scratch += _u_scratch(1, n_sub, min(seg, tm), tn)
    scratch += [pltpu.VMEM((2, tm, step), F32) for _ in range(tn // step)]
    out, nb = pl.pallas_call(
        kern,
        grid=(rows // tm, nj),
        in_specs=[
            x_spec,
            pl.BlockSpec((1, d), lambda i, j: (0, 0)),
            pl.BlockSpec((d, tn), lambda i, j: (0, j)),
            pl.BlockSpec((d, tn), lambda i, j: (0, nj + j)),
            pl.BlockSpec((d, tn), lambda i, j: (0, 2 * nj + j)),
            pl.BlockSpec((conv_w.shape[0], tn), lambda i, j: (0, j)),
            pl.BlockSpec((n_sub, SUBLANES, tn), lambda i, j: (i // tps, 0, j)),
            pl.BlockSpec((tn, d), lambda i, j: (j, 0)),
        ],
        out_specs=[
            pl.BlockSpec((tm, d), lambda i, j: (i, 0)),
            _nb_spec(n_sub, nj, tn, tps),
        ],
        out_shape=[jax.ShapeDtypeStruct((rows, d), F32), _nb_shape(n_seq, nj, tn)],
        scratch_shapes=scratch,
        compiler_params=_params("arbitrary", "arbitrary"),
        name="short_conv_mixer",
    )(x, norm_w, w_in, w_in, w_in, conv_w, st8, w_out)
    return out, _nb_merge(nb)


def _ffn_kernel(x_ref, nw_ref, wa_ref, wv_ref, cwa_ref, cwv_ref, ba_ref, bv_ref, sta_ref, stv_ref, w2_ref,
                fw_ref, o_ref, nba_ref, nbv_ref, h_ref, *scratch, n_sub, seg, tps, final_norm):
    j = pl.program_id(1)
    carry_refs, stage_refs = (scratch[:2], scratch[2:]) if tps > 1 else ((), scratch)
    ca, cv = carry_refs if tps > 1 else (None, None)
    subs = _sub_chunks(wa_ref.shape[1])
    ua_refs, uv_refs = stage_refs[:len(subs)], stage_refs[len(subs):]
    stage = functools.partial(_stage, n_sub=n_sub, seg=seg, tps=tps)
    conv = functools.partial(_staged_conv, n_sub=n_sub, seg=seg)

    @pl.when(j == 0)
    def _():
        x = x_ref[...]
        h_ref[...] = _rms(x, nw_ref[...]).astype(BF16)
        o_ref[...] = x
        _clear_carries(carry_refs)

    h = h_ref[...]
    slot = j % 2
    g_prev, cs_prev = None, None
    for k, cs in enumerate(subs):
        stage(ua_refs[k], (slot,), _dot(h, wa_ref[:, cs]), sta_ref, ca, nba_ref, cs)
        stage(uv_refs[k], (slot,), _dot(h, wv_ref[:, cs]), stv_ref, cv, nbv_ref, cs)
        if g_prev is not None:
            o_ref[...] += _dot(g_prev, w2_ref[cs_prev, :])
        a = conv(ua_refs[k], (slot,), cwa_ref[:, cs]) + ba_ref[:, cs]
        v = conv(uv_refs[k], (slot,), cwv_ref[:, cs]) + bv_ref[:, cs]
        g_prev, cs_prev = (_silu(a) * v).astype(BF16), cs
    o_ref[...] += _dot(g_prev, w2_ref[cs_prev, :])

    if final_norm:
        @pl.when(j == pl.num_programs(1) - 1)
        def _():
            o_ref[...] = _rms(o_ref[...], fw_ref[...])


def _ffn_layer(x, st8, norm_w, w_up, conv_w, conv_b, w_down, final_w, *, n_seq, seg, final_norm):
    rows, d = x.shape
    dff = w_down.shape[0]
    tm, n_sub, tps, x_spec = _residual_tiling(n_seq, seg, d)
    tn = _col_tile(dff)
    nj = dff // tn
    kern = functools.partial(_ffn_kernel, n_sub=n_sub, seg=min(seg, tm), tps=tps, final_norm=final_norm)
    scratch = [pltpu.VMEM((tm, d), BF16)]
    if tps > 1:
        scratch += [pltpu.VMEM((nj, SUBLANES, tn), F32), pltpu.VMEM((nj, SUBLANES, tn), F32)]
    scratch += _u_scratch(2, n_sub, min(seg, tm), tn)
    width = conv_w.shape[0]
    out, nba, nbv = pl.pallas_call(
        kern,
        grid=(rows // tm, nj),
        in_specs=[
            x_spec,
            pl.BlockSpec((1, d), lambda i, j: (0, 0)),
            pl.BlockSpec((d, tn), lambda i, j: (0, j)),
            pl.BlockSpec((d, tn), lambda i, j: (0, nj + j)),
            pl.BlockSpec((width, tn), lambda i, j: (0, j)),
            pl.BlockSpec((width, tn), lambda i, j: (0, nj + j)),
            pl.BlockSpec((1, tn), lambda i, j: (0, j)),
            pl.BlockSpec((1, tn), lambda i, j: (0, nj + j)),
            pl.BlockSpec((n_sub, SUBLANES, tn), lambda i, j: (i // tps, 0, j)),
            pl.BlockSpec((n_sub, SUBLANES, tn), lambda i, j: (i // tps, 0, nj + j)),
            pl.BlockSpec((tn, d), lambda i, j: (j, 0)),
            pl.BlockSpec((1, d), lambda i, j: (0, 0)),
        ],
        out_specs=[
            pl.BlockSpec((tm, d), lambda i, j: (i, 0)),
            _nb_spec(n_sub, nj, tn, tps),
            _nb_spec(n_sub, nj, tn, tps),
        ],
        out_shape=[jax.ShapeDtypeStruct((rows, d), F32), _nb_shape(n_seq, nj, tn), _nb_shape(n_seq, nj, tn)],
        scratch_shapes=scratch,
        compiler_params=_params("arbitrary", "arbitrary"),
        name="conv_ffn",
    )(x, norm_w, w_up, w_up, conv_w, conv_w, conv_b, conv_b, st8, st8, w_down, final_w)
    return out, jnp.concatenate([_nb_merge(nba), _nb_merge(nbv)], axis=-1)


def _ssd_in_kernel(x_ref, nw_ref, w_ref, cw_ref, cb_ref, st_ref, wdt_ref, dtb_ref, z_ref, o_ref, nb_ref, dt_ref,
                   h_ref, *scratch, n_sub, seg, tps, njx):
    s = pl.program_id(1)
    carry_refs, bufs = (scratch[:1], scratch[1:]) if tps > 1 else ((), scratch)
    carry = carry_refs[0] if tps > 1 else None
    tn = w_ref.shape[1]
    wide = [slice(k, k + tn // len(bufs)) for k in range(0, tn, tn // len(bufs))]

    @pl.when(s == 0)
    def _():
        h_ref[...] = _rms(x_ref[...], nw_ref[...]).astype(BF16)
        _clear_carries(carry_refs)

    @pl.when(s < njx)
    def _():
        h = h_ref[...]
        for k, cs in enumerate(wide):
            _stage(bufs[k], (), _dot(h, w_ref[:, cs]), st_ref, carry, nb_ref, cs,
                   n_sub=n_sub, seg=seg, tps=tps, chunk=s)
            y = _staged_conv(bufs[k], (), cw_ref[:, cs], n_sub=n_sub, seg=seg)
            o_ref[:, cs] = _silu(y + cb_ref[:, cs])

    @pl.when(s >= njx)
    def _():
        h = h_ref[...]
        for cs in _sub_chunks(tn):
            z_ref[:, cs] = _silu(_dot(h, w_ref[:, cs]))

    @pl.when(s == pl.num_programs(1) - 1)
    def _():
        dt_ref[...] = _softplus(_dot(h_ref[...], wdt_ref[...]) + dtb_ref[...])


def _ssd_in(x, st8, norm_w, w_in, conv_w, conv_b, dt_bias, *, n_seq, seg, inner, heads):
    rows, d = x.shape
    width, conv_dim = conv_w.shape
    cols = inner + conv_dim
    tm, n_sub, tps = _row_tiling(n_seq, seg, WIDE_ROW_TILE if seg >= WIDE_ROW_TILE else ROW_TILE)
    tn = _col_tile(inner, WIDE_COL_TILE)
    assert conv_dim % tn == 0
    nz = inner // tn
    njx = conv_dim // tn
    assert njx >= 2
    w_blk = lambda s: jnp.where(s < njx, nz + s, s - njx)
    mm_blk = lambda s: jnp.minimum(s, njx - 1)
    cv_blk = mm_blk
    z_blk = lambda s: jnp.maximum(s - njx, 0)
    w_dt = w_in[:, cols:]
    seg_t = min(seg, tm)
    step = min(2 * SUB_TILE, tn)
    kern = functools.partial(_ssd_in_kernel, n_sub=n_sub, seg=seg_t, tps=tps, njx=njx)
    scratch = [pltpu.VMEM((tm, d), BF16)]
    if tps > 1:
        scratch.append(pltpu.VMEM((njx, SUBLANES, tn), F32))
    scratch += [pltpu.VMEM((n_sub * (seg_t + SUBLANES), step), F32) for _ in range(tn // step)]
    zs, xbc, nb, dt = pl.pallas_call(
        kern,
        grid=(rows // tm, nz + njx),
        in_specs=[
            pl.BlockSpec((tm, d), lambda i, s: (i, 0)),
            pl.BlockSpec((1, d), lambda i, s: (0, 0)),
            pl.BlockSpec((d, tn), lambda i, s: (0, w_blk(s))),
            pl.BlockSpec((width, tn), lambda i, s: (0, cv_blk(s))),
            pl.BlockSpec((1, tn), lambda i, s: (0, cv_blk(s))),
            pl.BlockSpec((n_sub, SUBLANES, tn), lambda i, s: (i // tps, 0, mm_blk(s))),
            pl.BlockSpec((d, heads), lambda i, s: (0, 0)),
            pl.BlockSpec((1, heads), lambda i, s: (0, 0)),
        ],
        out_specs=[
            pl.BlockSpec((tm, tn), lambda i, s: (i, z_blk(s))),
            pl.BlockSpec((tm, tn), lambda i, s: (i, cv_blk(s))),
            _nb_spec(n_sub, njx, tn, tps),
            pl.BlockSpec((tm, heads), lambda i, s: (i, 0)),
        ],
        out_shape=[jax.ShapeDtypeStruct((rows, inner), F32), jax.ShapeDtypeStruct((rows, conv_dim), F32),
                   _nb_shape(n_seq, njx, tn), jax.ShapeDtypeStruct((rows, heads), F32)],
        scratch_shapes=scratch,
        compiler_params=_params("arbitrary", "arbitrary"),
        name="ssd_in_proj",
    )(x, norm_w, w_in, conv_w, conv_b, st8, w_dt, dt_bias)
    return zs, xbc, dt, _nb_merge(nb)


def _split3(x):
    hi = x.astype(BF16).astype(F32)
    r = x - hi
    mid = r.astype(BF16).astype(F32)
    lo = (r - mid).astype(BF16).astype(F32)
    return hi, mid, lo


def _dot_f32_lhs(x, sel):
    hi, mid, lo = _split3(x)
    return _dot(hi.astype(BF16), sel) + _dot(mid.astype(BF16), sel) + _dot(lo.astype(BF16), sel)


def _dot_f32_rhs(sel, x):
    hi, mid, lo = _split3(x)
    return _dot(sel, hi.astype(BF16)) + _dot(sel, mid.astype(BF16)) + _dot(sel, lo.astype(BF16))


def _scan_kernel(zs_ref, xs_ref, b_ref, c_ref, dt3_ref, dtt_ref, a3_ref, at_ref, dsk_ref, nw_ref, st0_ref,
                 rep3_ref, tril_ref, tri2_ref, mask2_ref, bd_ref, y_ref, stout_ref, st_ref,
                 *, groups, hpg, q, n):
    c = pl.program_id(1)
    gw = hpg * q
    heads = groups * hpg

    @pl.when(c == 0)
    def _():
        for g in range(groups):
            st_ref[g] = st0_ref[0, g].T

    dt3 = dt3_ref[...]
    cs3 = _dot_f32_rhs(tril_ref[...], dt3 * a3_ref[...])
    tail3 = jnp.exp(cs3[q - 1:q] - cs3) * dt3
    hi, mid, lo = _split3(jnp.concatenate([cs3, tail3], axis=0))
    lane = lax.broadcasted_iota(jnp.int32, hi.shape, 1)
    pieces = jnp.where(lane < heads, hi, jnp.where(lane < 2 * heads, mid, lo)).astype(BF16)
    bc = _dot(pieces, rep3_ref[...])
    cs_b, tail_b = bc[0:q], bc[q:2 * q]
    ecs_b = jnp.exp(cs_b)
    dtt = dtt_ref[0]
    cst = _dot_f32_lhs(dtt * at_ref[...], tri2_ref[...])
    mask2 = mask2_ref[...] > 0.0
    bd = bd_ref[...]

    gsl = [slice(g * gw, (g + 1) * gw) for g in range(groups)]
    cgs = [c_ref[:, g * n:(g + 1) * n].astype(BF16) for g in range(groups)]
    bgs = [b_ref[:, g * n:(g + 1) * n].astype(BF16) for g in range(groups)]
    cb2s = [lax.dot_general(cgs[g], jnp.concatenate([bgs[g], bgs[g]], axis=0), (((1,), (1,)), ((), ())),
                            preferred_element_type=F32) for g in range(groups)]
    y_inter = [_dot(cgs[g], st_ref[g].astype(BF16)) for g in range(groups)]
    ds = [lax.dot_general(bgs[g], (xs_ref[:, gsl[g]] * tail_b[:, gsl[g]]).astype(BF16), (((0,), (0,)), ((), ())),
                          preferred_element_type=F32) for g in range(groups)]
    ms, xbds = [], []
    for g in range(groups):
        for k in range(hpg // 2):
            psl = slice(g * gw + 2 * k * q, g * gw + 2 * (k + 1) * q)
            pair = g * (hpg // 2) + k
            seg = cs_b[:, psl] - jnp.broadcast_to(cst[pair:pair + 1], (q, 2 * q))
            w = jnp.where(mask2, jnp.exp(seg), 0.0) * jnp.broadcast_to(dtt[pair:pair + 1], (q, 2 * q))
            ms.append((cb2s[g] * w).astype(BF16))
            xk = xs_ref[:, psl]
            xbds.append((jnp.concatenate([xk, xk], axis=0) * bd).astype(BF16))
    parts = [_dot(m, xbd) for m, xbd in zip(ms, xbds)]
    for g in range(groups):
        st_ref[g] = st_ref[g] * ecs_b[q - 1:q, gsl[g]] + ds[g]
        pg = parts[g * (hpg // 2):(g + 1) * (hpg // 2)]
        y = pg[0] if len(pg) == 1 else jnp.concatenate(pg, axis=1)
        y = y + y_inter[g] * ecs_b[:, gsl[g]]
        y = (y + dsk_ref[:, gsl[g]] * xs_ref[:, gsl[g]]) * zs_ref[:, gsl[g]]
        y = y * lax.rsqrt(jnp.mean(y * y, axis=-1, keepdims=True) + EPS)
        y_ref[:, gsl[g]] = (y * nw_ref[:, gsl[g]]).astype(BF16)

    @pl.when(c == pl.num_programs(1) - 1)
    def _():
        for g in range(groups):
            stout_ref[0, g] = st_ref[g].T


def _ssd_scan(zs, xbc, dt, st0, a_log, d_skip, norm_w, *, n_seq, seg, inner, groups, n, heads):
    rows = zs.shape[0]
    p = inner // heads
    q = p
    hpg = heads // groups
    assert seg % q == 0 and hpg % 2 == 0 and inner % (groups * n) == 0
    nc = seg // q
    gn = groups * n
    a = -jnp.exp(a_log.astype(F32))
    a3 = jnp.tile(a.reshape(1, heads), (1, 3))
    a_t = jnp.repeat(a.reshape(heads // 2, 2), q, axis=1)
    dt3 = jnp.tile(dt, (1, 3))
    dtt = dt.reshape(rows // q, q, heads // 2, 2).transpose(0, 2, 3, 1).reshape(rows // q, heads // 2, 2 * q)
    dsk = jnp.repeat(d_skip.astype(F32), p).reshape(1, inner)
    rep3 = jnp.tile(jnp.repeat(jnp.eye(heads, dtype=BF16), p, axis=1), (3, 1))
    tril = jnp.tril(jnp.ones((q, q), BF16))
    triu = jnp.triu(jnp.ones((q, q), F32))
    zero = jnp.zeros((q, q), F32)
    tri2 = jnp.block([[triu, zero], [zero, triu]]).astype(BF16)
    mask2 = jnp.concatenate([jnp.tril(jnp.ones((q, q), F32))] * 2, axis=1)
    one = jnp.ones((q, q), F32)
    bd = jnp.block([[one, zero], [zero, one]])

    kern = functools.partial(_scan_kernel, groups=groups, hpg=hpg, q=q, n=n)
    const = lambda s, c: (0, 0)
    y, st = pl.pallas_call(
        kern,
        grid=(n_seq, nc),
        in_specs=[
            pl.BlockSpec((q, inner), lambda s, c: (s * nc + c, 0)),
            pl.BlockSpec((q, inner), lambda s, c: (s * nc + c, 0)),
            pl.BlockSpec((q, gn), lambda s, c: (s * nc + c, inner // gn)),
            pl.BlockSpec((q, gn), lambda s, c: (s * nc + c, inner // gn + 1)),
            pl.BlockSpec((q, 3 * heads), lambda s, c: (s * nc + c, 0)),
            pl.BlockSpec((1, heads // 2, 2 * q), lambda s, c: (s * nc + c, 0, 0)),
            pl.BlockSpec((1, 3 * heads), const),
            pl.BlockSpec((heads // 2, 2 * q), const),
            pl.BlockSpec((1, inner), const),
            pl.BlockSpec((1, inner), const),
            pl.BlockSpec((1, groups, hpg * p, n), lambda s, c: (s, 0, 0, 0)),
            pl.BlockSpec(rep3.shape, const),
            pl.BlockSpec(tril.shape, const),
            pl.BlockSpec(tri2.shape, const),
            pl.BlockSpec(mask2.shape, const),
            pl.BlockSpec(bd.shape, const),
        ],
        out_specs=[
            pl.BlockSpec((q, inner), lambda s, c: (s * nc + c, 0)),
            pl.BlockSpec((1, groups, hpg * p, n), lambda s, c: (s, 0, 0, 0)),
        ],
        out_shape=[jax.ShapeDtypeStruct((rows, inner), BF16),
                   jax.ShapeDtypeStruct(st0.shape, F32)],
        scratch_shapes=[pltpu.VMEM((groups, n, hpg * p), F32)],
        compiler_params=_params("arbitrary", "arbitrary"),
        name="ssd_scan",
    )(zs, xbc, xbc, xbc, dt3, dtt, a3, a_t, dsk, norm_w, st0, rep3, tril, tri2, mask2, bd)
    return y, st


def _out_proj_kernel(x_ref, g_ref, w_ref, o_ref):
    o_ref[...] = x_ref[...] + _dot(g_ref[...], w_ref[...])


def _out_proj(x, g, w, *, n_seq, seg):
    rows, d = x.shape
    k = g.shape[1]
    tm, _, _ = _row_tiling(n_seq, seg)
    return pl.pallas_call(
        _out_proj_kernel,
        grid=(rows // tm,),
        in_specs=[
            pl.BlockSpec((tm, d), lambda i: (i, 0)),
            pl.BlockSpec((tm, k), lambda i: (i, 0)),
            pl.BlockSpec((k, d), lambda i: (0, 0), pipeline_mode=pl.Buffered(1)),
        ],
        out_specs=pl.BlockSpec((tm, d), lambda i: (i, 0)),
        out_shape=jax.ShapeDtypeStruct((rows, d), F32),
        compiler_params=_params("arbitrary"),
        name="ssd_out_proj",
    )(x, g, w)


def _row_tiling(n_seq, seg, row_tile=ROW_TILE):
    if seg >= row_tile:
        assert seg % row_tile == 0
        return row_tile, 1, seg // row_tile
    n_sub = max(1, min(n_seq, row_tile // seg))
    while n_seq % n_sub:
        n_sub -= 1
    return n_sub * seg, n_sub, 1


def _residual_tiling(n_seq, seg, d):
    wide = seg >= WIDE_ROW_TILE
    tm, n_sub, tps = _row_tiling(n_seq, seg, WIDE_ROW_TILE if wide else ROW_TILE)
    mode = dict(pipeline_mode=pl.Buffered(1)) if wide else {}
    return tm, n_sub, tps, pl.BlockSpec((tm, d), lambda i, j: (i, 0), **mode)


def _col_tile(width, pref=COL_TILE):
    tn = pref
    while width % tn:
        tn //= 2
    assert tn >= 128
    return tn


def _nb_spec(n_sub, nj, tn, tps):
    return pl.BlockSpec((n_sub, nj, SUBLANES, tn), lambda i, j: (i // tps, 0, 0, 0))


def _nb_shape(n_seq, nj, tn):
    return jax.ShapeDtypeStruct((n_seq, nj, SUBLANES, tn), F32)


def _nb_merge(nb):
    n_seq, nj, rows, tn = nb.shape
    return nb.transpose(0, 2, 1, 3).reshape(n_seq, rows, nj * tn)


def _pad8(buf):
    return jnp.pad(buf, ((0, 0), (SUBLANES - buf.shape[1], 0), (0, 0)))


def _state_to_kernel(s, groups):
    b, h, p, n = s.shape
    return s.reshape(b, groups, (h // groups) * p, n)


def _state_from_kernel(s, heads):
    b, g, w, n = s.shape
    return s.reshape(b, heads, (g * w) // heads, n)


def _trunk(x, conv_a8, ssd_conv8, ssd_st, ffn8, wts, *, n_seq, seg):
    depth = wts["norm_mix"].shape[0]
    new_a, new_sc, new_st, new_f = [], [], [], []
    for i in range(depth):
        nm = wts["norm_mix"][i][None]
        j = i // 2
        if i % 2 == 0:
            x, nb = _sc_layer(x, conv_a8[j], nm, wts["sc_w_in"][j], wts["sc_conv_w"][j], wts["sc_w_out"][j],
                              n_seq=n_seq, seg=seg)
            new_a.append(nb)
        else:
            heads = wts["ssd_a_log"].shape[1]
            inner = wts["ssd_w_out"].shape[1]
            groups, n = ssd_st.shape[2], ssd_st.shape[4]
            zs, xbc, dt, nb = _ssd_in(x, ssd_conv8[j], nm, wts["ssd_w_in"][j], wts["ssd_conv_w"][j],
                                      wts["ssd_conv_b"][j][None], wts["ssd_dt_bias"][j][None],
                                      n_seq=n_seq, seg=seg, inner=inner, heads=heads)
            q = inner // heads
            pad = (-seg) % q
            if pad:
                padr = lambda t: jnp.pad(t.reshape(n_seq, seg, -1), ((0, 0), (0, pad), (0, 0))).reshape(
                    n_seq * (seg + pad), -1)
                zs, xbc, dt = padr(zs), padr(xbc), padr(dt)
            g, st = _ssd_scan(zs, xbc, dt, ssd_st[j], wts["ssd_a_log"][j], wts["ssd_d"][j],
                              wts["ssd_norm_w"][j][None], n_seq=n_seq, seg=seg + pad, inner=inner,
                              groups=groups, n=n, heads=heads)
            if pad:
                g = g.reshape(n_seq, seg + pad, -1)[:, :seg].reshape(n_seq * seg, -1)
            x = _out_proj(x, g, wts["ssd_w_out"][j], n_seq=n_seq, seg=seg)
            new_sc.append(nb)
            new_st.append(st)
        x, nf = _ffn_layer(x, ffn8[i], wts["norm_ffn"][i][None], wts["ffn_w_up"][i], wts["ffn_conv_w"][i],
                           wts["ffn_conv_b"][i][None], wts["ffn_w_down"][i], wts["norm_final"][None],
                           n_seq=n_seq, seg=seg, final_norm=(i == depth - 1))
        new_f.append(nf)
    return x, jnp.stack(new_a), jnp.stack(new_sc), jnp.stack(new_st), jnp.stack(new_f)


def kernel(x_prompt, x_sample, state_conv_a, state_ssd_conv, state_ssd, state_ffn_conv, meta_tokens, norm_mix,
           norm_ffn, norm_final, sc_w_in, sc_conv_w, sc_w_out, ssd_w_in, ssd_conv_w, ssd_conv_b, ssd_dt_bias,
           ssd_a_log, ssd_d, ssd_norm_w, ssd_w_out, ffn_w_up, ffn_conv_w, ffn_conv_b, ffn_w_down):
    b, seq, d = x_prompt.shape
    sb, sseq, _ = x_sample.shape
    n_meta = meta_tokens.shape[0]
    heads = ssd_a_log.shape[1]
    inner = ssd_w_out.shape[1]
    n_state = state_ssd.shape[-1]
    groups = (ssd_conv_w.shape[2] - inner) // (2 * n_state)
    wts = dict(
        norm_mix=norm_mix, norm_ffn=norm_ffn, norm_final=norm_final,
        sc_w_in=sc_w_in.astype(BF16), sc_conv_w=sc_conv_w, sc_w_out=sc_w_out.astype(BF16),
        ssd_w_in=ssd_w_in.astype(BF16), ssd_conv_w=ssd_conv_w, ssd_conv_b=ssd_conv_b, ssd_dt_bias=ssd_dt_bias,
        ssd_a_log=ssd_a_log, ssd_d=ssd_d, ssd_norm_w=ssd_norm_w, ssd_w_out=ssd_w_out.astype(BF16),
        ffn_w_up=ffn_w_up.astype(BF16), ffn_conv_w=ffn_conv_w, ffn_conv_b=ffn_conv_b,
        ffn_w_down=ffn_w_down.astype(BF16))

    def zeros8(like, n_seq):
        return jnp.zeros((like.shape[0], n_seq, SUBLANES, like.shape[-1]), F32)

    zero_st = jnp.zeros((state_ssd.shape[0], 1, groups, inner // groups, n_state), F32)
    _, m_a, m_sc, m_st, m_f = _trunk(
        meta_tokens.astype(F32), zeros8(state_conv_a, 1), zeros8(state_ssd_conv, 1), zero_st,
        zeros8(state_ffn_conv, 1), wts, n_seq=1, seg=n_meta)
    rep = lambda t: jnp.broadcast_to(t, (t.shape[0], b) + t.shape[2:])
    yp, p_a, p_sc, p_st, p_f = _trunk(
        x_prompt.reshape(b * seq, d), rep(m_a), rep(m_sc), rep(m_st), rep(m_f), wts, n_seq=b, seg=seq)
    ys, s_a, s_sc, s_st, s_f = _trunk(
        x_sample.reshape(sb * sseq, d), jax.vmap(_pad8)(state_conv_a), jax.vmap(_pad8)(state_ssd_conv),
        jax.vmap(lambda s: _state_to_kernel(s, groups))(state_ssd.astype(F32)), jax.vmap(_pad8)(state_ffn_conv),
        wts, n_seq=sb, seg=sseq)

    tail = lambda t, like: t[:, :, SUBLANES - like.shape[2]:, :]
    unst = jax.vmap(lambda s: _state_from_kernel(s, heads))
    return (yp.reshape(b, seq, d), ys.reshape(sb, sseq, d),
            tail(p_a, state_conv_a), tail(p_sc, state_ssd_conv), unst(p_st), tail(p_f, state_ffn_conv),
            tail(s_a, state_conv_a), tail(s_sc, state_ssd_conv), unst(s_st), tail(s_f, state_ffn_conv))
```

```python
import functools

import jax
import jax.numpy as jnp
from jax import lax
from jax.experimental import pallas as pl
from jax.experimental.pallas import tpu as pltpu

EPS = 1e-6
F32 = jnp.float32
BF16 = jnp.bfloat16

SUBLANES = 8
VMEM_LIMIT_BYTES = 58 * 1024 * 1024
ROW_TILE = 512
COL_TILE = 512
WIDE_ROW_TILE = 1024
WIDE_COL_TILE = 1024
SUB_TILE = 256


def _dot(a, b):
    return jnp.dot(a, b, preferred_element_type=F32)


def _params(*sem):
    return pltpu.CompilerParams(dimension_semantics=sem, vmem_limit_bytes=VMEM_LIMIT_BYTES)


def _rms(x, w):
    return x * lax.rsqrt(jnp.mean(x * x, axis=-1, keepdims=True) + EPS) * w


def _silu(x, scale=None):
    half = 0.5 * x
    gate = 1.0 + jnp.tanh(half)
    return half * gate if scale is None else (half * scale) * gate


def _softplus(x):
    return jnp.maximum(x, 0.0) + jnp.log1p(jnp.exp(-jnp.abs(x)))


def _stage(u_ref, idx, u, st_ref, carry_ref, nb_ref, cols, *, n_sub, seg, tps, chunk=None):
    i = pl.program_id(0)
    j = pl.program_id(1) if chunk is None else chunk
    for s in range(n_sub):
        us = u[s * seg:(s + 1) * seg]
        halo = st_ref[s, :, cols]
        if tps > 1:
            halo = jnp.where((i % tps) == 0, halo, carry_ref[j, :, cols])
        base = s * (seg + SUBLANES)
        u_ref[idx + (pl.ds(base, SUBLANES), slice(None))] = halo
        u_ref[idx + (pl.ds(base + SUBLANES, seg), slice(None))] = us
        nb_ref[s, j, :, cols] = us[seg - SUBLANES:seg]
    if tps > 1:
        carry_ref[j, :, cols] = u[u.shape[0] - SUBLANES:]


def _staged_conv(u_ref, idx, w, *, n_sub, seg):
    width = w.shape[0]
    ys = []
    for s in range(n_sub):
        base = s * (seg + SUBLANES) + SUBLANES
        y = None
        for d in range(width):
            term = u_ref[idx + (pl.ds(base - d, seg), slice(None))] * w[width - 1 - d:width - d]
            y = term if y is None else y + term
        ys.append(y)
    return ys[0] if n_sub == 1 else jnp.concatenate(ys, axis=0)


def _staged(u_ref, idx, *, n_sub, seg):
    parts = [u_ref[idx + (pl.ds(s * (seg + SUBLANES) + SUBLANES, seg), slice(None))] for s in range(n_sub)]
    return parts[0] if n_sub == 1 else jnp.concatenate(parts, axis=0)


def _clear_carries(carry_refs):
    @pl.when(pl.program_id(0) == 0)
    def _():
        for ref in carry_refs:
            ref[...] = jnp.zeros(ref.shape, F32)


def _sub_chunks(tn):
    step = min(SUB_TILE, tn)
    return [slice(k, k + step) for k in range(0, tn, step)]


def _u_scratch(parts, n_sub, seg, tn):
    step = min(SUB_TILE, tn)
    return [pltpu.VMEM((2, n_sub * (seg + SUBLANES), step), F32) for _ in range(parts * (tn // step))]


def _sc_kernel(x_ref, nw_ref, wb_ref, wc_ref, wv_ref, cw_ref, st_ref, w2_ref, o_ref, nb_ref,
               h_ref, *scratch, n_sub, seg, tps):
    j = pl.program_id(1)
    carry_refs, stage_refs = (scratch[:1], scratch[1:]) if tps > 1 else ((), scratch)
    subs = _sub_chunks(wb_ref.shape[1])
    u_refs, b_refs = stage_refs[:len(subs)], stage_refs[len(subs):]

    @pl.when(j == 0)
    def _():
        x = x_ref[...]
        h_ref[...] = _rms(x, nw_ref[...]).astype(BF16)
        o_ref[...] = x
        _clear_carries(carry_refs)

    h = h_ref[...]
    slot = j % 2
    carry = carry_refs[0] if tps > 1 else None
    g_prev, cs_prev = None, None
    for k, cs in enumerate(subs):
        b_refs[k][slot] = _dot(h, wb_ref[:, cs])
        _stage(u_refs[k], (slot,), _dot(h, wc_ref[:, cs]) * _dot(h, wv_ref[:, cs]), st_ref, carry, nb_ref, cs,
               n_sub=n_sub, seg=seg, tps=tps)
        if g_prev is not None:
            o_ref[...] += _dot(g_prev, w2_ref[cs_prev, :])
        y = _staged_conv(u_refs[k], (slot,), cw_ref[:, cs], n_sub=n_sub, seg=seg)
        g_prev, cs_prev = (b_refs[k][slot] * y).astype(BF16), cs
    o_ref[...] += _dot(g_prev, w2_ref[cs_prev, :])


def _sc_layer(x, st8, norm_w, w_in, conv_w, w_out, *, n_seq, seg):
    rows, d = x.shape
    tm, n_sub, tps, x_spec = _residual_tiling(n_seq, seg, d)
    tn = _col_tile(d)
    nj = d // tn
    kern = functools.partial(_sc_kernel, n_sub=n_sub, seg=min(seg, tm), tps=tps)
    step = min(SUB_TILE, tn)
    scratch = [pltpu.VMEM((tm, d), BF16)]
    if tps > 1:
        scratch.append(pltpu.VMEM((nj, SUBLANES, tn), F32))
    scratch += _u_scratch(1, n_sub, min(seg, tm), tn)
    scratch += [pltpu.VMEM((2, tm, step), F32) for _ in range(tn // step)]
    out, nb = pl.pallas_call(
        kern,
        grid=(rows // tm, nj),
        in_specs=[
            x_spec,
            pl.BlockSpec((1, d), lambda i, j: (0, 0)),
            pl.BlockSpec((d, tn), lambda i, j: (0, j)),
            pl.BlockSpec((d, tn), lambda i, j: (0, nj + j)),
            pl.BlockSpec((d, tn), lambda i, j: (0, 2 * nj + j)),
            pl.BlockSpec((conv_w.shape[0], tn), lambda i, j: (0, j)),
            pl.BlockSpec((n_sub, SUBLANES, tn), lambda i, j: (i // tps, 0, j)),
            pl.BlockSpec((tn, d), lambda i, j: (j, 0)),
        ],
        out_specs=[
            pl.BlockSpec((tm, d), lambda i, j: (i, 0)),
            _nb_spec(n_sub, nj, tn, tps),
        ],
        out_shape=[jax.ShapeDtypeStruct((rows, d), F32), _nb_shape(n_seq, nj, tn)],
        scratch_shapes=scratch,
        compiler_params=_params("arbitrary", "arbitrary"),
        name="short_conv_mixer",
    )(x, norm_w, w_in, w_in, w_in, conv_w, st8, w_out)
    return out, _nb_merge(nb)


def _ffn_kernel(x_ref, nw_ref, wa_ref, wv_ref, cwa_ref, cwv_ref, ba_ref, bv_ref, sta_ref, stv_ref, w2_ref,
                fw_ref, o_ref, nba_ref, nbv_ref, h_ref, *scratch, n_sub, seg, tps, final_norm):
    j = pl.program_id(1)
    carry_refs, stage_refs = (scratch[:2], scratch[2:]) if tps > 1 else ((), scratch)
    ca, cv = carry_refs if tps > 1 else (None, None)
    subs = _sub_chunks(wa_ref.shape[1])
    ua_refs, uv_refs = stage_refs[:len(subs)], stage_refs[len(subs):]
    stage = functools.partial(_stage, n_sub=n_sub, seg=seg, tps=tps)
    conv = functools.partial(_staged_conv, n_sub=n_sub, seg=seg)

    @pl.when(j == 0)
    def _():
        x = x_ref[...]
        h_ref[...] = _rms(x, nw_ref[...]).astype(BF16)
        o_ref[...] = x
        _clear_carries(carry_refs)

    h = h_ref[...]
    slot = j % 2
    g_prev, cs_prev = None, None
    for k, cs in enumerate(subs):
        stage(ua_refs[k], (slot,), _dot(h, wa_ref[:, cs]), sta_ref, ca, nba_ref, cs)
        stage(uv_refs[k], (slot,), _dot(h, wv_ref[:, cs]), stv_ref, cv, nbv_ref, cs)
        if g_prev is not None:
            o_ref[...] += _dot(g_prev, w2_ref[cs_prev, :])
        a = conv(ua_refs[k], (slot,), cwa_ref[:, cs]) + ba_ref[:, cs]
        v = conv(uv_refs[k], (slot,), cwv_ref[:, cs]) + bv_ref[:, cs]
        g_prev, cs_prev = _silu(a, v).astype(BF16), cs
    o_ref[...] += _dot(g_prev, w2_ref[cs_prev, :])

    if final_norm:
        @pl.when(j == pl.num_programs(1) - 1)
        def _():
            o_ref[...] = _rms(o_ref[...], fw_ref[...])


def _ffn_layer(x, st8, norm_w, w_up, conv_w, conv_b, w_down, final_w, *, n_seq, seg, final_norm):
    rows, d = x.shape
    dff = w_down.shape[0]
    tm, n_sub, tps, x_spec = _residual_tiling(n_seq, seg, d)
    tn = _col_tile(dff)
    nj = dff // tn
    kern = functools.partial(_ffn_kernel, n_sub=n_sub, seg=min(seg, tm), tps=tps, final_norm=final_norm)
    scratch = [pltpu.VMEM((tm, d), BF16)]
    if tps > 1:
        scratch += [pltpu.VMEM((nj, SUBLANES, tn), F32), pltpu.VMEM((nj, SUBLANES, tn), F32)]
    scratch += _u_scratch(2, n_sub, min(seg, tm), tn)
    width = conv_w.shape[0]
    out, nba, nbv = pl.pallas_call(
        kern,
        grid=(rows // tm, nj),
        in_specs=[
            x_spec,
            pl.BlockSpec((1, d), lambda i, j: (0, 0)),
            pl.BlockSpec((d, tn), lambda i, j: (0, j)),
            pl.BlockSpec((d, tn), lambda i, j: (0, nj + j)),
            pl.BlockSpec((width, tn), lambda i, j: (0, j)),
            pl.BlockSpec((width, tn), lambda i, j: (0, nj + j)),
            pl.BlockSpec((1, tn), lambda i, j: (0, j)),
            pl.BlockSpec((1, tn), lambda i, j: (0, nj + j)),
            pl.BlockSpec((n_sub, SUBLANES, tn), lambda i, j: (i // tps, 0, j)),
            pl.BlockSpec((n_sub, SUBLANES, tn), lambda i, j: (i // tps, 0, nj + j)),
            pl.BlockSpec((tn, d), lambda i, j: (j, 0)),
            pl.BlockSpec((1, d), lambda i, j: (0, 0)),
        ],
        out_specs=[
            pl.BlockSpec((tm, d), lambda i, j: (i, 0)),
            _nb_spec(n_sub, nj, tn, tps),
            _nb_spec(n_sub, nj, tn, tps),
        ],
        out_shape=[jax.ShapeDtypeStruct((rows, d), F32), _nb_shape(n_seq, nj, tn), _nb_shape(n_seq, nj, tn)],
        scratch_shapes=scratch,
        compiler_params=_params("arbitrary", "arbitrary"),
        name="conv_ffn",
    )(x, norm_w, w_up, w_up, conv_w, conv_w, conv_b, conv_b, st8, st8, w_down, final_w)
    return out, jnp.concatenate([_nb_merge(nba), _nb_merge(nbv)], axis=-1)


def _ssd_in_kernel(x_ref, nw_ref, w_ref, cw_ref, cb_ref, st_ref, wdt_ref, dtb_ref, z_ref, o_ref, nb_ref, dt_ref,
                   h_ref, *scratch, n_sub, seg, tps, njx):
    s = pl.program_id(1)
    carry_refs, bufs = (scratch[:1], scratch[1:]) if tps > 1 else ((), scratch)
    carry = carry_refs[0] if tps > 1 else None
    tn = w_ref.shape[1]
    wide = [slice(k, k + tn // len(bufs)) for k in range(0, tn, tn // len(bufs))]

    @pl.when(s == 0)
    def _():
        h_ref[...] = _rms(x_ref[...], nw_ref[...]).astype(BF16)
        _clear_carries(carry_refs)

    @pl.when(s < njx)
    def _():
        h = h_ref[...]
        for k, cs in enumerate(wide):
            _stage(bufs[k], (), _dot(h, w_ref[:, cs]), st_ref, carry, nb_ref, cs,
                   n_sub=n_sub, seg=seg, tps=tps, chunk=s)
            y = _staged_conv(bufs[k], (), cw_ref[:, cs], n_sub=n_sub, seg=seg)
            o_ref[:, cs] = _silu(y + cb_ref[:, cs])

    @pl.when(s >= njx)
    def _():
        h = h_ref[...]
        for cs in _sub_chunks(tn):
            z_ref[:, cs] = _silu(_dot(h, w_ref[:, cs]))

    @pl.when(s == pl.num_programs(1) - 1)
    def _():
        dt_ref[...] = _softplus(_dot(h_ref[...], wdt_ref[...]) + dtb_ref[...])


def _ssd_in(x, st8, norm_w, w_in, conv_w, conv_b, dt_bias, *, n_seq, seg, inner, heads):
    rows, d = x.shape
    width, conv_dim = conv_w.shape
    cols = inner + conv_dim
    tm, n_sub, tps = _row_tiling(n_seq, seg, WIDE_ROW_TILE if seg >= WIDE_ROW_TILE else ROW_TILE)
    tn = _col_tile(inner, WIDE_COL_TILE)
    assert conv_dim % tn == 0
    nz = inner // tn
    njx = conv_dim // tn
    assert njx >= 2
    w_blk = lambda s: jnp.where(s < njx, nz + s, s - njx)
    mm_blk = lambda s: jnp.minimum(s, njx - 1)
    cv_blk = mm_blk
    z_blk = lambda s: jnp.maximum(s - njx, 0)
    w_dt = w_in[:, cols:]
    seg_t = min(seg, tm)
    step = min(2 * SUB_TILE, tn)
    kern = functools.partial(_ssd_in_kernel, n_sub=n_sub, seg=seg_t, tps=tps, njx=njx)
    scratch = [pltpu.VMEM((tm, d), BF16)]
    if tps > 1:
        scratch.append(pltpu.VMEM((njx, SUBLANES, tn), F32))
    scratch += [pltpu.VMEM((n_sub * (seg_t + SUBLANES), step), F32) for _ in range(tn // step)]
    zs, xbc, nb, dt = pl.pallas_call(
        kern,
        grid=(rows // tm, nz + njx),
        in_specs=[
            pl.BlockSpec((tm, d), lambda i, s: (i, 0)),
            pl.BlockSpec((1, d), lambda i, s: (0, 0)),
            pl.BlockSpec((d, tn), lambda i, s: (0, w_blk(s))),
            pl.BlockSpec((width, tn), lambda i, s: (0, cv_blk(s))),
            pl.BlockSpec((1, tn), lambda i, s: (0, cv_blk(s))),
            pl.BlockSpec((n_sub, SUBLANES, tn), lambda i, s: (i // tps, 0, mm_blk(s))),
            pl.BlockSpec((d, heads), lambda i, s: (0, 0)),
            pl.BlockSpec((1, heads), lambda i, s: (0, 0)),
        ],
        out_specs=[
            pl.BlockSpec((tm, tn), lambda i, s: (i, z_blk(s))),
            pl.BlockSpec((tm, tn), lambda i, s: (i, cv_blk(s))),
            _nb_spec(n_sub, njx, tn, tps),
            pl.BlockSpec((tm, heads), lambda i, s: (i, 0)),
        ],
        out_shape=[jax.ShapeDtypeStruct((rows, inner), F32), jax.ShapeDtypeStruct((rows, conv_dim), F32),
                   _nb_shape(n_seq, njx, tn), jax.ShapeDtypeStruct((rows, heads), F32)],
        scratch_shapes=scratch,
        compiler_params=_params("arbitrary", "arbitrary"),
        name="ssd_in_proj",
    )(x, norm_w, w_in, conv_w, conv_b, st8, w_dt, dt_bias)
    return zs, xbc, dt, _nb_merge(nb)


def _split3(x):
    hi = x.astype(BF16).astype(F32)
    r = x - hi
    mid = r.astype(BF16).astype(F32)
    lo = (r - mid).astype(BF16).astype(F32)
    return hi, mid, lo


def _dot_f32_lhs(x, sel):
    hi, mid, lo = _split3(x)
    return _dot(hi.astype(BF16), sel) + _dot(mid.astype(BF16), sel) + _dot(lo.astype(BF16), sel)


def _dot_f32_rhs(sel, x):
    hi, mid, lo = _split3(x)
    return _dot(sel, hi.astype(BF16)) + _dot(sel, mid.astype(BF16)) + _dot(sel, lo.astype(BF16))


def _scan_kernel(zs_ref, xs_ref, b_ref, c_ref, dt3_ref, dtt_ref, a3_ref, at_ref, dsk_ref, nw_ref, st0_ref,
                 rep3_ref, tril_ref, tri2_ref, mask2_ref, bd_ref, y_ref, stout_ref, st_ref,
                 *, groups, hpg, q, n):
    c = pl.program_id(1)
    gw = hpg * q
    heads = groups * hpg

    @pl.when(c == 0)
    def _():
        for g in range(groups):
            st_ref[g] = st0_ref[0, g].T

    dt3 = dt3_ref[...]
    cs3 = _dot_f32_rhs(tril_ref[...], dt3 * a3_ref[...])
    tail3 = jnp.exp(cs3[q - 1:q] - cs3) * dt3
    hi, mid, lo = _split3(jnp.concatenate([cs3, tail3], axis=0))
    lane = lax.broadcasted_iota(jnp.int32, hi.shape, 1)
    pieces = jnp.where(lane < heads, hi, jnp.where(lane < 2 * heads, mid, lo)).astype(BF16)
    gsl = [slice(g * gw, (g + 1) * gw) for g in range(groups)]
    bcs = [_dot(pieces, rep3_ref[:, gsl[g]]) for g in range(groups)]
    cs_b = [bc[0:q] for bc in bcs]
    tail_b = [bc[q:2 * q] for bc in bcs]
    ecs_b = [jnp.exp(c) for c in cs_b]
    dtt = dtt_ref[0]
    cst = _dot_f32_lhs(dtt * at_ref[...], tri2_ref[...])
    mask2 = mask2_ref[...] > 0.0
    bd = bd_ref[...]

    cgs = [c_ref[:, g * n:(g + 1) * n].astype(BF16) for g in range(groups)]
    bgs = [b_ref[:, g * n:(g + 1) * n].astype(BF16) for g in range(groups)]
    cb2s = [lax.dot_general(cgs[g], jnp.concatenate([bgs[g], bgs[g]], axis=0), (((1,), (1,)), ((), ())),
                            preferred_element_type=F32) for g in range(groups)]
    y_inter = [_dot(cgs[g], st_ref[g].astype(BF16)) for g in range(groups)]
    ds = [lax.dot_general(bgs[g], (xs_ref[:, gsl[g]] * tail_b[g]).astype(BF16), (((0,), (0,)), ((), ())),
                          preferred_element_type=F32) for g in range(groups)]
    ms, xbds = [], []
    for g in range(groups):
        for k in range(hpg // 2):
            psl = slice(g * gw + 2 * k * q, g * gw + 2 * (k + 1) * q)
            pair = g * (hpg // 2) + k
            seg = cs_b[g][:, 2 * k * q:2 * (k + 1) * q] - jnp.broadcast_to(cst[pair:pair + 1], (q, 2 * q))
            w = jnp.where(mask2, jnp.exp(seg), 0.0) * jnp.broadcast_to(dtt[pair:pair + 1], (q, 2 * q))
            ms.append((cb2s[g] * w).astype(BF16))
            xk = xs_ref[:, psl]
            xbds.append((jnp.concatenate([xk, xk], axis=0) * bd).astype(BF16))
    parts = [_dot(m, xbd) for m, xbd in zip(ms, xbds)]
    for g in range(groups):
        st_ref[g] = st_ref[g] * ecs_b[g][q - 1:q] + ds[g]
        pg = parts[g * (hpg // 2):(g + 1) * (hpg // 2)]
        y = pg[0] if len(pg) == 1 else jnp.concatenate(pg, axis=1)
        y = y + y_inter[g] * ecs_b[g]
        y = (y + dsk_ref[:, gsl[g]] * xs_ref[:, gsl[g]]) * zs_ref[:, gsl[g]]
        y = y * lax.rsqrt(jnp.mean(y * y, axis=-1, keepdims=True) + EPS)
        y_ref[:, gsl[g]] = (y * nw_ref[:, gsl[g]]).astype(BF16)

    @pl.when(c == pl.num_programs(1) - 1)
    def _():
        for g in range(groups):
            stout_ref[0, g] = st_ref[g].T


def _ssd_scan(zs, xbc, dt, st_all, layer, a_log, d_skip, norm_w, *, n_seq, seg, inner, groups, n, heads):
    rows = zs.shape[0]
    p = inner // heads
    q = p
    hpg = heads // groups
    assert seg % q == 0 and hpg % 2 == 0 and inner % (groups * n) == 0
    nc = seg // q
    gn = groups * n
    a = -jnp.exp(a_log.astype(F32))
    a3 = jnp.tile(a.reshape(1, heads), (1, 3))
    a_t = jnp.repeat(a.reshape(heads // 2, 2), q, axis=1)
    dt3 = jnp.tile(dt, (1, 3))
    dtt = dt.reshape(rows // q, q, heads // 2, 2).transpose(0, 2, 3, 1).reshape(rows // q, heads // 2, 2 * q)
    dsk = jnp.repeat(d_skip.astype(F32), p).reshape(1, inner)
    rep3 = jnp.tile(jnp.repeat(jnp.eye(heads, dtype=BF16), p, axis=1), (3, 1))
    tril = jnp.tril(jnp.ones((q, q), BF16))
    triu = jnp.triu(jnp.ones((q, q), F32))
    zero = jnp.zeros((q, q), F32)
    tri2 = jnp.block([[triu, zero], [zero, triu]]).astype(BF16)
    mask2 = jnp.concatenate([jnp.tril(jnp.ones((q, q), F32))] * 2, axis=1)
    one = jnp.ones((q, q), F32)
    bd = jnp.block([[one, zero], [zero, one]])

    kern = functools.partial(_scan_kernel, groups=groups, hpg=hpg, q=q, n=n)
    const = lambda s, c: (0, 0)
    operands = [zs, xbc, xbc, xbc, dt3, dtt, a3, a_t, dsk, norm_w, st_all, rep3, tril, tri2, mask2, bd]
    y, st = pl.pallas_call(
        kern,
        grid=(n_seq, nc),
        in_specs=[
            pl.BlockSpec((q, inner), lambda s, c: (s * nc + c, 0)),
            pl.BlockSpec((q, inner), lambda s, c: (s * nc + c, 0)),
            pl.BlockSpec((q, gn), lambda s, c: (s * nc + c, inner // gn)),
            pl.BlockSpec((q, gn), lambda s, c: (s * nc + c, inner // gn + 1)),
            pl.BlockSpec((q, 3 * heads), lambda s, c: (s * nc + c, 0)),
            pl.BlockSpec((1, heads // 2, 2 * q), lambda s, c: (s * nc + c, 0, 0)),
            pl.BlockSpec((1, 3 * heads), const),
            pl.BlockSpec((heads // 2, 2 * q), const),
            pl.BlockSpec((1, inner), const),
            pl.BlockSpec((1, inner), const),
            pl.BlockSpec((None, 1, groups, hpg * p, n), lambda s, c: (layer, s, 0, 0, 0)),
            pl.BlockSpec(rep3.shape, const),
            pl.BlockSpec(tril.shape, const),
            pl.BlockSpec(tri2.shape, const),
            pl.BlockSpec(mask2.shape, const),
            pl.BlockSpec(bd.shape, const),
        ],
        out_specs=[
            pl.BlockSpec((q, inner), lambda s, c: (s * nc + c, 0)),
            pl.BlockSpec((None, 1, groups, hpg * p, n), lambda s, c: (layer, s, 0, 0, 0)),
        ],
        out_shape=[jax.ShapeDtypeStruct((rows, inner), BF16),
                   jax.ShapeDtypeStruct(st_all.shape, F32)],
        scratch_shapes=[pltpu.VMEM((groups, n, hpg * p), F32)],
        input_output_aliases={10: 1},
        compiler_params=_params("arbitrary", "arbitrary"),
        name="ssd_scan",
    )(*operands)
    return y, st


def _out_proj_kernel(x_ref, g_ref, w_ref, o_ref):
    o_ref[...] = x_ref[...] + _dot(g_ref[...], w_ref[...])


def _out_proj(x, g, w, *, n_seq, seg):
    rows, d = x.shape
    k = g.shape[1]
    tm, _, _ = _row_tiling(n_seq, seg)
    return pl.pallas_call(
        _out_proj_kernel,
        grid=(rows // tm,),
        in_specs=[
            pl.BlockSpec((tm, d), lambda i: (i, 0)),
            pl.BlockSpec((tm, k), lambda i: (i, 0)),
            pl.BlockSpec((k, d), lambda i: (0, 0), pipeline_mode=pl.Buffered(1)),
        ],
        out_specs=pl.BlockSpec((tm, d), lambda i: (i, 0)),
        out_shape=jax.ShapeDtypeStruct((rows, d), F32),
        compiler_params=_params("arbitrary"),
        name="ssd_out_proj",
    )(x, g, w)


def _row_tiling(n_seq, seg, row_tile=ROW_TILE):
    if seg >= row_tile:
        assert seg % row_tile == 0
        return row_tile, 1, seg // row_tile
    n_sub = max(1, min(n_seq, row_tile // seg))
    while n_seq % n_sub:
        n_sub -= 1
    return n_sub * seg, n_sub, 1


def _residual_tiling(n_seq, seg, d):
    wide = seg >= WIDE_ROW_TILE
    tm, n_sub, tps = _row_tiling(n_seq, seg, WIDE_ROW_TILE if wide else ROW_TILE)
    mode = dict(pipeline_mode=pl.Buffered(1)) if wide else {}
    return tm, n_sub, tps, pl.BlockSpec((tm, d), lambda i, j: (i, 0), **mode)


def _col_tile(width, pref=COL_TILE):
    tn = pref
    while width % tn:
        tn //= 2
    assert tn >= 128
    return tn


def _nb_spec(n_sub, nj, tn, tps):
    return pl.BlockSpec((n_sub, nj, SUBLANES, tn), lambda i, j: (i // tps, 0, 0, 0))


def _nb_shape(n_seq, nj, tn):
    return jax.ShapeDtypeStruct((n_seq, nj, SUBLANES, tn), F32)


def _nb_merge(nb):
    n_seq, nj, rows, tn = nb.shape
    return nb.transpose(0, 2, 1, 3).reshape(n_seq, rows, nj * tn)


def _pad8(buf):
    return jnp.pad(buf, ((0, 0), (SUBLANES - buf.shape[1], 0), (0, 0)))


def _state_to_kernel(s, groups):
    b, h, p, n = s.shape
    return s.reshape(b, groups, (h // groups) * p, n)


def _state_from_kernel(s, heads):
    b, g, w, n = s.shape
    return s.reshape(b, heads, (g * w) // heads, n)


def _trunk(x, conv_a8, ssd_conv8, ssd_st, ffn8, wts, *, n_seq, seg):
    depth = wts["norm_mix"].shape[0]
    new_a, new_sc, new_f = [], [], []
    for i in range(depth):
        nm = wts["norm_mix"][i][None]
        j = i // 2
        if i % 2 == 0:
            x, nb = _sc_layer(x, conv_a8[j], nm, wts["sc_w_in"][j], wts["sc_conv_w"][j], wts["sc_w_out"][j],
                              n_seq=n_seq, seg=seg)
            new_a.append(nb)
        else:
            heads = wts["ssd_a_log"].shape[1]
            inner = wts["ssd_w_out"].shape[1]
            groups, n = ssd_st.shape[2], ssd_st.shape[4]
            zs, xbc, dt, nb = _ssd_in(x, ssd_conv8[j], nm, wts["ssd_w_in"][j], wts["ssd_conv_w"][j],
                                      wts["ssd_conv_b"][j][None], wts["ssd_dt_bias"][j][None],
                                      n_seq=n_seq, seg=seg, inner=inner, heads=heads)
            q = inner // heads
            pad = (-seg) % q
            if pad:
                padr = lambda t: jnp.pad(t.reshape(n_seq, seg, -1), ((0, 0), (0, pad), (0, 0))).reshape(
                    n_seq * (seg + pad), -1)
                zs, xbc, dt = padr(zs), padr(xbc), padr(dt)
            g, ssd_st = _ssd_scan(zs, xbc, dt, ssd_st, j, wts["ssd_a_log"][j], wts["ssd_d"][j],
                                  wts["ssd_norm_w"][j][None], n_seq=n_seq, seg=seg + pad, inner=inner,
                                  groups=groups, n=n, heads=heads)
            if pad:
                g = g.reshape(n_seq, seg + pad, -1)[:, :seg].reshape(n_seq * seg, -1)
            x = _out_proj(x, g, wts["ssd_w_out"][j], n_seq=n_seq, seg=seg)
            new_sc.append(nb)
        x, nf = _ffn_layer(x, ffn8[i], wts["norm_ffn"][i][None], wts["ffn_w_up"][i], wts["ffn_conv_w"][i],
                           wts["ffn_conv_b"][i][None], wts["ffn_w_down"][i], wts["norm_final"][None],
                           n_seq=n_seq, seg=seg, final_norm=(i == depth - 1))
        new_f.append(nf)
    return x, jnp.stack(new_a), jnp.stack(new_sc), ssd_st, jnp.stack(new_f)


def kernel(x_prompt, x_sample, state_conv_a, state_ssd_conv, state_ssd, state_ffn_conv, meta_tokens, norm_mix,
           norm_ffn, norm_final, sc_w_in, sc_conv_w, sc_w_out, ssd_w_in, ssd_conv_w, ssd_conv_b, ssd_dt_bias,
           ssd_a_log, ssd_d, ssd_norm_w, ssd_w_out, ffn_w_up, ffn_conv_w, ffn_conv_b, ffn_w_down):
    b, seq, d = x_prompt.shape
    sb, sseq, _ = x_sample.shape
    n_meta = meta_tokens.shape[0]
    heads = ssd_a_log.shape[1]
    inner = ssd_w_out.shape[1]
    n_state = state_ssd.shape[-1]
    groups = (ssd_conv_w.shape[2] - inner) // (2 * n_state)
    wts = dict(
        norm_mix=norm_mix, norm_ffn=norm_ffn, norm_final=norm_final,
        sc_w_in=sc_w_in.astype(BF16), sc_conv_w=sc_conv_w, sc_w_out=sc_w_out.astype(BF16),
        ssd_w_in=ssd_w_in.astype(BF16), ssd_conv_w=ssd_conv_w, ssd_conv_b=ssd_conv_b, ssd_dt_bias=ssd_dt_bias,
        ssd_a_log=ssd_a_log, ssd_d=ssd_d, ssd_norm_w=ssd_norm_w, ssd_w_out=ssd_w_out.astype(BF16),
        ffn_w_up=ffn_w_up.astype(BF16), ffn_conv_w=ffn_conv_w, ffn_conv_b=ffn_conv_b,
        ffn_w_down=ffn_w_down.astype(BF16))

    def zeros8(like, n_seq):
        return jnp.zeros((like.shape[0], n_seq, SUBLANES, like.shape[-1]), F32)

    zero_st = jnp.zeros((state_ssd.shape[0], 1, groups, inner // groups, n_state), F32)
    _, m_a, m_sc, m_st, m_f = _trunk(
        meta_tokens.astype(F32), zeros8(state_conv_a, 1), zeros8(state_ssd_conv, 1), zero_st,
        zeros8(state_ffn_conv, 1), wts, n_seq=1, seg=n_meta)
    rep = lambda t: jnp.broadcast_to(t, (t.shape[0], b) + t.shape[2:])
    yp, p_a, p_sc, p_st, p_f = _trunk(
        x_prompt.reshape(b * seq, d), rep(m_a), rep(m_sc), rep(m_st), rep(m_f), wts, n_seq=b, seg=seq)
    ys, s_a, s_sc, s_st, s_f = _trunk(
        x_sample.reshape(sb * sseq, d), jax.vmap(_pad8)(state_conv_a), jax.vmap(_pad8)(state_ssd_conv),
        jax.vmap(lambda s: _state_to_kernel(s, groups))(state_ssd.astype(F32)), jax.vmap(_pad8)(state_ffn_conv),
        wts, n_seq=sb, seg=sseq)

    tail = lambda t, like: t[:, :, SUBLANES - like.shape[2]:, :]
    unst = jax.vmap(lambda s: _state_from_kernel(s, heads))
    return (yp.reshape(b, seq, d), ys.reshape(sb, sseq, d),
            tail(p_a, state_conv_a), tail(p_sc, state_ssd_conv), unst(p_st), tail(p_f, state_ffn_conv),
            tail(s_a, state_conv_a), tail(s_sc, state_ssd_conv), unst(s_st), tail(s_f, state_ffn_conv))
```

```python
import functools

import jax
import jax.numpy as jnp
from jax import lax
from jax.experimental import pallas as pl
from jax.experimental.pallas import tpu as pltpu

EPS = 1e-6
F32 = jnp.float32
BF16 = jnp.bfloat16

SUBLANES = 8
VMEM_LIMIT_BYTES = 58 * 1024 * 1024
ROW_TILE = 512
COL_TILE = 512
WIDE_ROW_TILE = 1024
WIDE_COL_TILE = 1024
SUB_TILE = 256


def _dot(a, b):
    return jnp.dot(a, b, preferred_element_type=F32)


def _params(*sem):
    return pltpu.CompilerParams(dimension_semantics=sem, vmem_limit_bytes=VMEM_LIMIT_BYTES)


def _rms(x, w):
    return x * lax.rsqrt(jnp.mean(x * x, axis=-1, keepdims=True) + EPS) * w


def _silu(x, scale=None):
    half = 0.5 * x
    gate = 1.0 + jnp.tanh(half)
    return half * gate if scale is None else (half * scale) * gate


def _softplus(x):
    return jnp.maximum(x, 0.0) + jnp.log1p(jnp.exp(-jnp.abs(x)))


def _stage(u_ref, idx, u, st_ref, carry_ref, nb_ref, cols, *, n_sub, seg, tps, chunk=None):
    i = pl.program_id(0)
    j = pl.program_id(1) if chunk is None else chunk
    for s in range(n_sub):
        us = u[s * seg:(s + 1) * seg]
        halo = st_ref[s, :, cols]
        if tps > 1:
            halo = jnp.where((i % tps) == 0, halo, carry_ref[j, :, cols])
        base = s * (seg + SUBLANES)
        u_ref[idx + (pl.ds(base, SUBLANES), slice(None))] = halo
        u_ref[idx + (pl.ds(base + SUBLANES, seg), slice(None))] = us
        nb_ref[s, j, :, cols] = us[seg - SUBLANES:seg]
    if tps > 1:
        carry_ref[j, :, cols] = u[u.shape[0] - SUBLANES:]


def _staged_conv(u_ref, idx, w, *, n_sub, seg):
    width = w.shape[0]
    ys = []
    for s in range(n_sub):
        base = s * (seg + SUBLANES) + SUBLANES
        y = None
        for d in range(width):
            term = u_ref[idx + (pl.ds(base - d, seg), slice(None))] * w[width - 1 - d:width - d]
            y = term if y is None else y + term
        ys.append(y)
    return ys[0] if n_sub == 1 else jnp.concatenate(ys, axis=0)


def _staged(u_ref, idx, *, n_sub, seg):
    parts = [u_ref[idx + (pl.ds(s * (seg + SUBLANES) + SUBLANES, seg), slice(None))] for s in range(n_sub)]
    return parts[0] if n_sub == 1 else jnp.concatenate(parts, axis=0)


def _clear_carries(carry_refs):
    @pl.when(pl.program_id(0) == 0)
    def _():
        for ref in carry_refs:
            ref[...] = jnp.zeros(ref.shape, F32)


def _sub_chunks(tn):
    step = min(SUB_TILE, tn)
    return [slice(k, k + step) for k in range(0, tn, step)]


def _u_scratch(parts, n_sub, seg, tn):
    step = min(SUB_TILE, tn)
    return [pltpu.VMEM((2, n_sub * (seg + SUBLANES), step), F32) for _ in range(parts * (tn // step))]


def _sc_kernel(x_ref, nw_ref, wb_ref, wc_ref, wv_ref, cw_ref, st_ref, w2_ref, o_ref, nb_ref,
               h_ref, *scratch, n_sub, seg, tps):
    j = pl.program_id(1)
    carry_refs, stage_refs = (scratch[:1], scratch[1:]) if tps > 1 else ((), scratch)
    subs = _sub_chunks(wb_ref.shape[1])
    u_refs, b_refs = stage_refs[:len(subs)], stage_refs[len(subs):]

    @pl.when(j == 0)
    def _():
        x = x_ref[...]
        h_ref[...] = _rms(x, nw_ref[...]).astype(BF16)
        o_ref[...] = x
        _clear_carries(carry_refs)

    h = h_ref[...]
    slot = j % 2
    carry = carry_refs[0] if tps > 1 else None
    g_prev, cs_prev = None, None
    for k, cs in enumerate(subs):
        b_refs[k][slot] = _dot(h, wb_ref[:, cs])
        _stage(u_refs[k], (slot,), _dot(h, wc_ref[:, cs]) * _dot(h, wv_ref[:, cs]), st_ref, carry, nb_ref, cs,
               n_sub=n_sub, seg=seg, tps=tps)
        if g_prev is not None:
            o_ref[...] += _dot(g_prev, w2_ref[cs_prev, :])
        y = _staged_conv(u_refs[k], (slot,), cw_ref[:, cs], n_sub=n_sub, seg=seg)
        g_prev, cs_prev = (b_refs[k][slot] * y).astype(BF16), cs
    o_ref[...] += _dot(g_prev, w2_ref[cs_prev, :])


def _sc_layer(x, st8, norm_w, w_in, conv_w, w_out, *, n_seq, seg):
    rows, d = x.shape
    tm, n_sub, tps = _row_tiling(n_seq, seg)
    x_spec = pl.BlockSpec((tm, d), lambda i, j: (i, 0))
    tn = _col_tile(d)
    nj = d // tn
    kern = functools.partial(_sc_kernel, n_sub=n_sub, seg=min(seg, tm), tps=tps)
    step = min(SUB_TILE, tn)
    scratch = [pltpu.VMEM((tm, d), BF16)]
    if tps > 1:
        scratch.append(pltpu.VMEM((nj, SUBLANES, tn), F32))
    scratch += _u_scratch(1, n_sub, min(seg, tm), tn)
    scratch += [pltpu.VMEM((2, tm, step), F32) for _ in range(tn // step)]
    out, nb = pl.pallas_call(
        kern,
        grid=(rows // tm, nj),
        in_specs=[
            x_spec,
            pl.BlockSpec((1, d), lambda i, j: (0, 0)),
            pl.BlockSpec((d, tn), lambda i, j: (0, j)),
            pl.BlockSpec((d, tn), lambda i, j: (0, nj + j)),
            pl.BlockSpec((d, tn), lambda i, j: (0, 2 * nj + j)),
            pl.BlockSpec((conv_w.shape[0], tn), lambda i, j: (0, j)),
            pl.BlockSpec((n_sub, SUBLANES, tn), lambda i, j: (i // tps, 0, j)),
            pl.BlockSpec((tn, d), lambda i, j: (j, 0)),
        ],
        out_specs=[
            pl.BlockSpec((tm, d), lambda i, j: (i, 0)),
            _nb_spec(n_sub, nj, tn, tps),
        ],
        out_shape=[jax.ShapeDtypeStruct((rows, d), F32), _nb_shape(n_seq, nj, tn)],
        scratch_shapes=scratch,
        compiler_params=_params("arbitrary", "arbitrary"),
        name="short_conv_mixer",
    )(x, norm_w, w_in, w_in, w_in, conv_w, st8, w_out)
    return out, _nb_merge(nb)


def _ffn_kernel(x_ref, nw_ref, wa_ref, wv_ref, cwa_ref, cwv_ref, ba_ref, bv_ref, sta_ref, stv_ref, w2_ref,
                fw_ref, o_ref, nba_ref, nbv_ref, h_ref, *scratch, n_sub, seg, tps, final_norm):
    j = pl.program_id(1)
    carry_refs, stage_refs = (scratch[:2], scratch[2:]) if tps > 1 else ((), scratch)
    ca, cv = carry_refs if tps > 1 else (None, None)
    subs = _sub_chunks(wa_ref.shape[1])
    ua_refs, uv_refs = stage_refs[:len(subs)], stage_refs[len(subs):]
    stage = functools.partial(_stage, n_sub=n_sub, seg=seg, tps=tps)
    conv = functools.partial(_staged_conv, n_sub=n_sub, seg=seg)

    @pl.when(j == 0)
    def _():
        x = x_ref[...]
        h_ref[...] = _rms(x, nw_ref[...]).astype(BF16)
        o_ref[...] = x
        _clear_carries(carry_refs)

    h = h_ref[...]
    slot = j % 2
    g_prev, cs_prev = None, None
    for k, cs in enumerate(subs):
        stage(ua_refs[k], (slot,), _dot(h, wa_ref[:, cs]), sta_ref, ca, nba_ref, cs)
        stage(uv_refs[k], (slot,), _dot(h, wv_ref[:, cs]), stv_ref, cv, nbv_ref, cs)
        if g_prev is not None:
            o_ref[...] += _dot(g_prev, w2_ref[cs_prev, :])
        a = conv(ua_refs[k], (slot,), cwa_ref[:, cs]) + ba_ref[:, cs]
        v = conv(uv_refs[k], (slot,), cwv_ref[:, cs]) + bv_ref[:, cs]
        g_prev, cs_prev = _silu(a, v).astype(BF16), cs
    o_ref[...] += _dot(g_prev, w2_ref[cs_prev, :])

    if final_norm:
        @pl.when(j == pl.num_programs(1) - 1)
        def _():
            o_ref[...] = _rms(o_ref[...], fw_ref[...])


def _ffn_layer(x, st8, norm_w, w_up, conv_w, conv_b, w_down, final_w, *, n_seq, seg, final_norm):
    rows, d = x.shape
    dff = w_down.shape[0]
    tm, n_sub, tps, x_spec = _residual_tiling(n_seq, seg, d)
    tn = _col_tile(dff)
    nj = dff // tn
    kern = functools.partial(_ffn_kernel, n_sub=n_sub, seg=min(seg, tm), tps=tps, final_norm=final_norm)
    scratch = [pltpu.VMEM((tm, d), BF16)]
    if tps > 1:
        scratch += [pltpu.VMEM((nj, SUBLANES, tn), F32), pltpu.VMEM((nj, SUBLANES, tn), F32)]
    scratch += _u_scratch(2, n_sub, min(seg, tm), tn)
    width = conv_w.shape[0]
    out, nba, nbv = pl.pallas_call(
        kern,
        grid=(rows // tm, nj),
        in_specs=[
            x_spec,
            pl.BlockSpec((1, d), lambda i, j: (0, 0)),
            pl.BlockSpec((d, tn), lambda i, j: (0, j)),
            pl.BlockSpec((d, tn), lambda i, j: (0, nj + j)),
            pl.BlockSpec((width, tn), lambda i, j: (0, j)),
            pl.BlockSpec((width, tn), lambda i, j: (0, nj + j)),
            pl.BlockSpec((1, tn), lambda i, j: (0, j)),
            pl.BlockSpec((1, tn), lambda i, j: (0, nj + j)),
            pl.BlockSpec((n_sub, SUBLANES, tn), lambda i, j: (i // tps, 0, j)),
            pl.BlockSpec((n_sub, SUBLANES, tn), lambda i, j: (i // tps, 0, nj + j)),
            pl.BlockSpec((tn, d), lambda i, j: (j, 0)),
            pl.BlockSpec((1, d), lambda i, j: (0, 0)),
        ],
        out_specs=[
            pl.BlockSpec((tm, d), lambda i, j: (i, 0)),
            _nb_spec(n_sub, nj, tn, tps),
            _nb_spec(n_sub, nj, tn, tps),
        ],
        out_shape=[jax.ShapeDtypeStruct((rows, d), F32), _nb_shape(n_seq, nj, tn), _nb_shape(n_seq, nj, tn)],
        scratch_shapes=scratch,
        compiler_params=_params("arbitrary", "arbitrary"),
        name="conv_ffn",
    )(x, norm_w, w_up, w_up, conv_w, conv_w, conv_b, conv_b, st8, st8, w_down, final_w)
    return out, jnp.concatenate([_nb_merge(nba), _nb_merge(nbv)], axis=-1)


def _ssd_in_kernel(x_ref, nw_ref, w_ref, cw_ref, cb_ref, st_ref, wdt_ref, dtb_ref, z_ref, o_ref, nb_ref, dt_ref,
                   h_ref, *scratch, n_sub, seg, tps, njx):
    s = pl.program_id(1)
    carry_refs, bufs = (scratch[:1], scratch[1:]) if tps > 1 else ((), scratch)
    carry = carry_refs[0] if tps > 1 else None
    tn = w_ref.shape[1]
    wide = [slice(k, k + tn // len(bufs)) for k in range(0, tn, tn // len(bufs))]

    @pl.when(s == 0)
    def _():
        h_ref[...] = _rms(x_ref[...], nw_ref[...]).astype(BF16)
        _clear_carries(carry_refs)

    @pl.when(s < njx)
    def _():
        h = h_ref[...]
        for k, cs in enumerate(wide):
            _stage(bufs[k], (), _dot(h, w_ref[:, cs]), st_ref, carry, nb_ref, cs,
                   n_sub=n_sub, seg=seg, tps=tps, chunk=s)
            y = _staged_conv(bufs[k], (), cw_ref[:, cs], n_sub=n_sub, seg=seg)
            o_ref[:, cs] = _silu(y + cb_ref[:, cs])

    @pl.when(s >= njx)
    def _():
        h = h_ref[...]
        for cs in _sub_chunks(tn):
            z_ref[:, cs] = _silu(_dot(h, w_ref[:, cs]))

    @pl.when(s == pl.num_programs(1) - 1)
    def _():
        dt_ref[...] = _softplus(_dot(h_ref[...], wdt_ref[...]) + dtb_ref[...])


def _ssd_in(x, st8, norm_w, w_in, conv_w, conv_b, dt_bias, *, n_seq, seg, inner, heads):
    rows, d = x.shape
    width, conv_dim = conv_w.shape
    cols = inner + conv_dim
    tm, n_sub, tps = _row_tiling(n_seq, seg, WIDE_ROW_TILE if seg >= WIDE_ROW_TILE else ROW_TILE)
    tn = _col_tile(inner, WIDE_COL_TILE)
    assert conv_dim % tn == 0
    nz = inner // tn
    njx = conv_dim // tn
    assert njx >= 2
    w_blk = lambda s: jnp.where(s < njx, nz + s, s - njx)
    mm_blk = lambda s: jnp.minimum(s, njx - 1)
    cv_blk = mm_blk
    z_blk = lambda s: jnp.maximum(s - njx, 0)
    w_dt = w_in[:, cols:]
    seg_t = min(seg, tm)
    step = min(2 * SUB_TILE, tn)
    kern = functools.partial(_ssd_in_kernel, n_sub=n_sub, seg=seg_t, tps=tps, njx=njx)
    scratch = [pltpu.VMEM((tm, d), BF16)]
    if tps > 1:
        scratch.append(pltpu.VMEM((njx, SUBLANES, tn), F32))
    scratch += [pltpu.VMEM((n_sub * (seg_t + SUBLANES), step), F32) for _ in range(tn // step)]
    zs, xbc, nb, dt = pl.pallas_call(
        kern,
        grid=(rows // tm, nz + njx),
        in_specs=[
            pl.BlockSpec((tm, d), lambda i, s: (i, 0)),
            pl.BlockSpec((1, d), lambda i, s: (0, 0)),
            pl.BlockSpec((d, tn), lambda i, s: (0, w_blk(s))),
            pl.BlockSpec((width, tn), lambda i, s: (0, cv_blk(s))),
            pl.BlockSpec((1, tn), lambda i, s: (0, cv_blk(s))),
            pl.BlockSpec((n_sub, SUBLANES, tn), lambda i, s: (i // tps, 0, mm_blk(s))),
            pl.BlockSpec((d, heads), lambda i, s: (0, 0)),
            pl.BlockSpec((1, heads), lambda i, s: (0, 0)),
        ],
        out_specs=[
            pl.BlockSpec((tm, tn), lambda i, s: (i, z_blk(s))),
            pl.BlockSpec((tm, tn), lambda i, s: (i, cv_blk(s))),
            _nb_spec(n_sub, njx, tn, tps),
            pl.BlockSpec((tm, heads), lambda i, s: (i, 0)),
        ],
        out_shape=[jax.ShapeDtypeStruct((rows, inner), F32), jax.ShapeDtypeStruct((rows, conv_dim), F32),
                   _nb_shape(n_seq, njx, tn), jax.ShapeDtypeStruct((rows, heads), F32)],
        scratch_shapes=scratch,
        compiler_params=_params("arbitrary", "arbitrary"),
        name="ssd_in_proj",
    )(x, norm_w, w_in, conv_w, conv_b, st8, w_dt, dt_bias)
    return zs, xbc, dt, _nb_merge(nb)


def _split3(x):
    hi = x.astype(BF16).astype(F32)
    r = x - hi
    mid = r.astype(BF16).astype(F32)
    lo = (r - mid).astype(BF16).astype(F32)
    return hi, mid, lo


def _dot_f32_lhs(x, sel):
    hi, mid, lo = _split3(x)
    return _dot(hi.astype(BF16), sel) + _dot(mid.astype(BF16), sel) + _dot(lo.astype(BF16), sel)


def _dot_f32_rhs(sel, x):
    hi, mid, lo = _split3(x)
    return _dot(sel, hi.astype(BF16)) + _dot(sel, mid.astype(BF16)) + _dot(sel, lo.astype(BF16))


def _scan_kernel(zs_ref, xs_ref, b_ref, c_ref, dt3_ref, dtt_ref, a3_ref, at_ref, dsk_ref, nw_ref, st0_ref,
                 rep3_ref, tril_ref, tri2_ref, mask2_ref, bd_ref, y_ref, stout_ref, st_ref,
                 *, groups, hpg, q, n):
    c = pl.program_id(1)
    gw = hpg * q
    heads = groups * hpg

    @pl.when(c == 0)
    def _():
        for g in range(groups):
            st_ref[g] = st0_ref[0, g].T

    dt3 = dt3_ref[...]
    cs3 = _dot_f32_rhs(tril_ref[...], dt3 * a3_ref[...])
    tail3 = jnp.exp(cs3[q - 1:q] - cs3) * dt3
    hi, mid, lo = _split3(jnp.concatenate([cs3, tail3], axis=0))
    lane = lax.broadcasted_iota(jnp.int32, hi.shape, 1)
    pieces = jnp.where(lane < heads, hi, jnp.where(lane < 2 * heads, mid, lo)).astype(BF16)
    gsl = [slice(g * gw, (g + 1) * gw) for g in range(groups)]
    bcs = [_dot(pieces, rep3_ref[:, gsl[g]]) for g in range(groups)]
    cs_b = [bc[0:q] for bc in bcs]
    tail_b = [bc[q:2 * q] for bc in bcs]
    ecs_b = [jnp.exp(c) for c in cs_b]
    dtt = dtt_ref[0]
    cst = _dot_f32_lhs(dtt * at_ref[...], tri2_ref[...])
    mask2 = mask2_ref[...] > 0.0
    bd = bd_ref[...]

    cgs = [c_ref[:, g * n:(g + 1) * n].astype(BF16) for g in range(groups)]
    bgs = [b_ref[:, g * n:(g + 1) * n].astype(BF16) for g in range(groups)]
    cb2s = [lax.dot_general(cgs[g], jnp.concatenate([bgs[g], bgs[g]], axis=0), (((1,), (1,)), ((), ())),
                            preferred_element_type=F32) for g in range(groups)]
    y_inter = [_dot(cgs[g], st_ref[g].astype(BF16)) for g in range(groups)]
    ds = [lax.dot_general(bgs[g], (xs_ref[:, gsl[g]] * tail_b[g]).astype(BF16), (((0,), (0,)), ((), ())),
                          preferred_element_type=F32) for g in range(groups)]
    ms, xbds = [], []
    for g in range(groups):
        for k in range(hpg // 2):
            psl = slice(g * gw + 2 * k * q, g * gw + 2 * (k + 1) * q)
            pair = g * (hpg // 2) + k
            seg = cs_b[g][:, 2 * k * q:2 * (k + 1) * q] - jnp.broadcast_to(cst[pair:pair + 1], (q, 2 * q))
            w = jnp.where(mask2, jnp.exp(seg), 0.0) * jnp.broadcast_to(dtt[pair:pair + 1], (q, 2 * q))
            ms.append((cb2s[g] * w).astype(BF16))
            xk = xs_ref[:, psl]
            xbds.append((jnp.concatenate([xk, xk], axis=0) * bd).astype(BF16))
    parts = [_dot(m, xbd) for m, xbd in zip(ms, xbds)]
    for g in range(groups):
        st_ref[g] = st_ref[g] * ecs_b[g][q - 1:q] + ds[g]
        pg = parts[g * (hpg // 2):(g + 1) * (hpg // 2)]
        y = pg[0] if len(pg) == 1 else jnp.concatenate(pg, axis=1)
        y = y + y_inter[g] * ecs_b[g]
        y = (y + dsk_ref[:, gsl[g]] * xs_ref[:, gsl[g]]) * zs_ref[:, gsl[g]]
        y = y * lax.rsqrt(jnp.mean(y * y, axis=-1, keepdims=True) + EPS)
        y_ref[:, gsl[g]] = (y * nw_ref[:, gsl[g]]).astype(BF16)

    @pl.when(c == pl.num_programs(1) - 1)
    def _():
        for g in range(groups):
            stout_ref[0, g] = st_ref[g].T


def _ssd_scan(zs, xbc, dt, st_all, layer, a_log, d_skip, norm_w, *, n_seq, seg, inner, groups, n, heads):
    rows = zs.shape[0]
    p = inner // heads
    q = p
    hpg = heads // groups
    assert seg % q == 0 and hpg % 2 == 0 and inner % (groups * n) == 0
    nc = seg // q
    gn = groups * n
    a = -jnp.exp(a_log.astype(F32))
    a3 = jnp.tile(a.reshape(1, heads), (1, 3))
    a_t = jnp.repeat(a.reshape(heads // 2, 2), q, axis=1)
    dt3 = jnp.tile(dt, (1, 3))
    dtt = dt.reshape(rows // q, q, heads // 2, 2).transpose(0, 2, 3, 1).reshape(rows // q, heads // 2, 2 * q)
    dsk = jnp.repeat(d_skip.astype(F32), p).reshape(1, inner)
    rep3 = jnp.tile(jnp.repeat(jnp.eye(heads, dtype=BF16), p, axis=1), (3, 1))
    tril = jnp.tril(jnp.ones((q, q), BF16))
    triu = jnp.triu(jnp.ones((q, q), F32))
    zero = jnp.zeros((q, q), F32)
    tri2 = jnp.block([[triu, zero], [zero, triu]]).astype(BF16)
    mask2 = jnp.concatenate([jnp.tril(jnp.ones((q, q), F32))] * 2, axis=1)
    one = jnp.ones((q, q), F32)
    bd = jnp.block([[one, zero], [zero, one]])

    kern = functools.partial(_scan_kernel, groups=groups, hpg=hpg, q=q, n=n)
    const = lambda s, c: (0, 0)
    operands = [zs, xbc, xbc, xbc, dt3, dtt, a3, a_t, dsk, norm_w, st_all, rep3, tril, tri2, mask2, bd]
    y, st = pl.pallas_call(
        kern,
        grid=(n_seq, nc),
        in_specs=[
            pl.BlockSpec((q, inner), lambda s, c: (s * nc + c, 0)),
            pl.BlockSpec((q, inner), lambda s, c: (s * nc + c, 0)),
            pl.BlockSpec((q, gn), lambda s, c: (s * nc + c, inner // gn)),
            pl.BlockSpec((q, gn), lambda s, c: (s * nc + c, inner // gn + 1)),
            pl.BlockSpec((q, 3 * heads), lambda s, c: (s * nc + c, 0)),
            pl.BlockSpec((1, heads // 2, 2 * q), lambda s, c: (s * nc + c, 0, 0)),
            pl.BlockSpec((1, 3 * heads), const),
            pl.BlockSpec((heads // 2, 2 * q), const),
            pl.BlockSpec((1, inner), const),
            pl.BlockSpec((1, inner), const),
            pl.BlockSpec((None, 1, groups, hpg * p, n), lambda s, c: (layer, s, 0, 0, 0)),
            pl.BlockSpec(rep3.shape, const),
            pl.BlockSpec(tril.shape, const),
            pl.BlockSpec(tri2.shape, const),
            pl.BlockSpec(mask2.shape, const),
            pl.BlockSpec(bd.shape, const),
        ],
        out_specs=[
            pl.BlockSpec((q, inner), lambda s, c: (s * nc + c, 0)),
            pl.BlockSpec((None, 1, groups, hpg * p, n), lambda s, c: (layer, s, 0, 0, 0)),
        ],
        out_shape=[jax.ShapeDtypeStruct((rows, inner), BF16),
                   jax.ShapeDtypeStruct(st_all.shape, F32)],
        scratch_shapes=[pltpu.VMEM((groups, n, hpg * p), F32)],
        input_output_aliases={10: 1},
        compiler_params=_params("arbitrary", "arbitrary"),
        name="ssd_scan",
    )(*operands)
    return y, st


def _out_proj_kernel(x_ref, g_ref, w_ref, o_ref):
    o_ref[...] = x_ref[...] + _dot(g_ref[...], w_ref[...])


def _out_proj(x, g, w, *, n_seq, seg):
    rows, d = x.shape
    k = g.shape[1]
    tm, _, _ = _row_tiling(n_seq, seg)
    return pl.pallas_call(
        _out_proj_kernel,
        grid=(rows // tm,),
        in_specs=[
            pl.BlockSpec((tm, d), lambda i: (i, 0)),
            pl.BlockSpec((tm, k), lambda i: (i, 0)),
            pl.BlockSpec((k, d), lambda i: (0, 0), pipeline_mode=pl.Buffered(1)),
        ],
        out_specs=pl.BlockSpec((tm, d), lambda i: (i, 0)),
        out_shape=jax.ShapeDtypeStruct((rows, d), F32),
        compiler_params=_params("arbitrary"),
        name="ssd_out_proj",
    )(x, g, w)


def _row_tiling(n_seq, seg, row_tile=ROW_TILE):
    if seg >= row_tile:
        assert seg % row_tile == 0
        return row_tile, 1, seg // row_tile
    n_sub = max(1, min(n_seq, row_tile // seg))
    while n_seq % n_sub:
        n_sub -= 1
    return n_sub * seg, n_sub, 1


def _residual_tiling(n_seq, seg, d):
    wide = seg >= WIDE_ROW_TILE
    tm, n_sub, tps = _row_tiling(n_seq, seg, WIDE_ROW_TILE if wide else ROW_TILE)
    mode = dict(pipeline_mode=pl.Buffered(1)) if wide else {}
    return tm, n_sub, tps, pl.BlockSpec((tm, d), lambda i, j: (i, 0), **mode)


def _col_tile(width, pref=COL_TILE):
    tn = pref
    while width % tn:
        tn //= 2
    assert tn >= 128
    return tn


def _nb_spec(n_sub, nj, tn, tps):
    return pl.BlockSpec((n_sub, nj, SUBLANES, tn), lambda i, j: (i // tps, 0, 0, 0))


def _nb_shape(n_seq, nj, tn):
    return jax.ShapeDtypeStruct((n_seq, nj, SUBLANES, tn), F32)


def _nb_merge(nb):
    n_seq, nj, rows, tn = nb.shape
    return nb.transpose(0, 2, 1, 3).reshape(n_seq, rows, nj * tn)


def _pad8(buf):
    return jnp.pad(buf, ((0, 0), (SUBLANES - buf.shape[1], 0), (0, 0)))


def _state_to_kernel(s, groups):
    b, h, p, n = s.shape
    return s.reshape(b, groups, (h // groups) * p, n)


def _state_from_kernel(s, heads):
    b, g, w, n = s.shape
    return s.reshape(b, heads, (g * w) // heads, n)


def _trunk(x, conv_a8, ssd_conv8, ssd_st, ffn8, wts, *, n_seq, seg):
    depth = wts["norm_mix"].shape[0]
    new_a, new_sc, new_f = [], [], []
    for i in range(depth):
        nm = wts["norm_mix"][i][None]
        j = i // 2
        if i % 2 == 0:
            x, nb = _sc_layer(x, conv_a8[j], nm, wts["sc_w_in"][j], wts["sc_conv_w"][j], wts["sc_w_out"][j],
                              n_seq=n_seq, seg=seg)
            new_a.append(nb)
        else:
            heads = wts["ssd_a_log"].shape[1]
            inner = wts["ssd_w_out"][j].shape[0]
            groups, n = ssd_st.shape[2], ssd_st.shape[4]
            zs, xbc, dt, nb = _ssd_in(x, ssd_conv8[j], nm, wts["ssd_w_in"][j], wts["ssd_conv_w"][j],
                                      wts["ssd_conv_b"][j][None], wts["ssd_dt_bias"][j][None],
                                      n_seq=n_seq, seg=seg, inner=inner, heads=heads)
            q = inner // heads
            pad = (-seg) % q
            if pad:
                padr = lambda t: jnp.pad(t.reshape(n_seq, seg, -1), ((0, 0), (0, pad), (0, 0))).reshape(
                    n_seq * (seg + pad), -1)
                zs, xbc, dt = padr(zs), padr(xbc), padr(dt)
            g, ssd_st = _ssd_scan(zs, xbc, dt, ssd_st, j, wts["ssd_a_log"][j], wts["ssd_d"][j],
                                  wts["ssd_norm_w"][j][None], n_seq=n_seq, seg=seg + pad, inner=inner,
                                  groups=groups, n=n, heads=heads)
            if pad:
                g = g.reshape(n_seq, seg + pad, -1)[:, :seg].reshape(n_seq * seg, -1)
            x = _out_proj(x, g, wts["ssd_w_out"][j], n_seq=n_seq, seg=seg)
            new_sc.append(nb)
        x, nf = _ffn_layer(x, ffn8[i], wts["norm_ffn"][i][None], wts["ffn_w_up"][i], wts["ffn_conv_w"][i],
                           wts["ffn_conv_b"][i][None], wts["ffn_w_down"][i], wts["norm_final"][None],
                           n_seq=n_seq, seg=seg, final_norm=(i == depth - 1))
        new_f.append(nf)
    return x, jnp.stack(new_a), jnp.stack(new_sc), ssd_st, jnp.stack(new_f)


def kernel(x_prompt, x_sample, state_conv_a, state_ssd_conv, state_ssd, state_ffn_conv, meta_tokens, norm_mix,
           norm_ffn, norm_final, sc_w_in, sc_conv_w, sc_w_out, ssd_w_in, ssd_conv_w, ssd_conv_b, ssd_dt_bias,
           ssd_a_log, ssd_d, ssd_norm_w, ssd_w_out, ffn_w_up, ffn_conv_w, ffn_conv_b, ffn_w_down):
    b, seq, d = x_prompt.shape
    sb, sseq, _ = x_sample.shape
    n_meta = meta_tokens.shape[0]
    heads = ssd_a_log.shape[1]
    inner = ssd_w_out.shape[1]
    n_state = state_ssd.shape[-1]
    groups = (ssd_conv_w.shape[2] - inner) // (2 * n_state)
    per_layer = lambda w: [w[l].astype(BF16) for l in range(w.shape[0])]
    wts = dict(
        norm_mix=norm_mix, norm_ffn=norm_ffn, norm_final=norm_final,
        sc_w_in=per_layer(sc_w_in), sc_conv_w=sc_conv_w, sc_w_out=per_layer(sc_w_out),
        ssd_w_in=per_layer(ssd_w_in), ssd_conv_w=ssd_conv_w, ssd_conv_b=ssd_conv_b, ssd_dt_bias=ssd_dt_bias,
        ssd_a_log=ssd_a_log, ssd_d=ssd_d, ssd_norm_w=ssd_norm_w, ssd_w_out=per_layer(ssd_w_out),
        ffn_w_up=per_layer(ffn_w_up), ffn_conv_w=ffn_conv_w, ffn_conv_b=ffn_conv_b,
        ffn_w_down=per_layer(ffn_w_down))

    def zeros8(like, n_seq):
        return jnp.zeros((like.shape[0], n_seq, SUBLANES, like.shape[-1]), F32)

    zero_st = jnp.zeros((state_ssd.shape[0], 1, groups, inner // groups, n_state), F32)
    _, m_a, m_sc, m_st, m_f = _trunk(
        meta_tokens.astype(F32), zeros8(state_conv_a, 1), zeros8(state_ssd_conv, 1), zero_st,
        zeros8(state_ffn_conv, 1), wts, n_seq=1, seg=n_meta)
    rep = lambda t: jnp.broadcast_to(t, (t.shape[0], b) + t.shape[2:])
    yp, p_a, p_sc, p_st, p_f = _trunk(
        x_prompt.reshape(b * seq, d), rep(m_a), rep(m_sc), rep(m_st), rep(m_f), wts, n_seq=b, seg=seq)
    ys, s_a, s_sc, s_st, s_f = _trunk(
        x_sample.reshape(sb * sseq, d), jax.vmap(_pad8)(state_conv_a), jax.vmap(_pad8)(state_ssd_conv),
        jax.vmap(lambda s: _state_to_kernel(s, groups))(state_ssd.astype(F32)), jax.vmap(_pad8)(state_ffn_conv),
        wts, n_seq=sb, seg=sseq)

    tail = lambda t, like: t[:, :, SUBLANES - like.shape[2]:, :]
    unst = jax.vmap(lambda s: _state_from_kernel(s, heads))
    return (yp.reshape(b, seq, d), ys.reshape(sb, sseq, d),
            tail(p_a, state_conv_a), tail(p_sc, state_ssd_conv), unst(p_st), tail(p_f, state_ffn_conv),
            tail(s_a, state_conv_a), tail(s_sc, state_ssd_conv), unst(s_st), tail(s_f, state_ffn_conv))
```

```python
import functools

import jax
import jax.numpy as jnp
from jax import lax
from jax.experimental import pallas as pl
from jax.experimental.pallas import tpu as pltpu

EPS = 1e-6
F32 = jnp.float32
BF16 = jnp.bfloat16

SUBLANES = 8
VMEM_LIMIT_BYTES = 58 * 1024 * 1024
ROW_TILE = 512
COL_TILE = 512
WIDE_ROW_TILE = 1024
WIDE_COL_TILE = 1024
SUB_TILE = 256


def _dot(a, b):
    return jnp.dot(a, b, preferred_element_type=F32)


def _params(*sem):
    return pltpu.CompilerParams(dimension_semantics=sem, vmem_limit_bytes=VMEM_LIMIT_BYTES)


def _rms(x, w):
    return x * lax.rsqrt(jnp.mean(x * x, axis=-1, keepdims=True) + EPS) * w


def _silu(x, scale=None):
    half = 0.5 * x
    gate = 1.0 + jnp.tanh(half)
    return half * gate if scale is None else (half * scale) * gate


def _softplus(x):
    return jnp.maximum(x, 0.0) + jnp.log1p(jnp.exp(-jnp.abs(x)))


def _stage(u_ref, idx, u, st_ref, carry_ref, nb_ref, cols, *, n_sub, seg, tps, chunk=None):
    i = pl.program_id(0)
    j = pl.program_id(1) if chunk is None else chunk
    for s in range(n_sub):
        us = u[s * seg:(s + 1) * seg]
        halo = st_ref[s, :, cols]
        if tps > 1:
            halo = jnp.where((i % tps) == 0, halo, carry_ref[j, :, cols])
        base = s * (seg + SUBLANES)
        u_ref[idx + (pl.ds(base, SUBLANES), slice(None))] = halo
        u_ref[idx + (pl.ds(base + SUBLANES, seg), slice(None))] = us
        nb_ref[s, j, :, cols] = us[seg - SUBLANES:seg]
    if tps > 1:
        carry_ref[j, :, cols] = u[u.shape[0] - SUBLANES:]


def _staged_conv(u_ref, idx, w, *, n_sub, seg):
    width = w.shape[0]
    ys = []
    for s in range(n_sub):
        base = s * (seg + SUBLANES) + SUBLANES
        y = None
        for d in range(width):
            term = u_ref[idx + (pl.ds(base - d, seg), slice(None))] * w[width - 1 - d:width - d]
            y = term if y is None else y + term
        ys.append(y)
    return ys[0] if n_sub == 1 else jnp.concatenate(ys, axis=0)


def _staged(u_ref, idx, *, n_sub, seg):
    parts = [u_ref[idx + (pl.ds(s * (seg + SUBLANES) + SUBLANES, seg), slice(None))] for s in range(n_sub)]
    return parts[0] if n_sub == 1 else jnp.concatenate(parts, axis=0)


def _clear_carries(carry_refs):
    @pl.when(pl.program_id(0) == 0)
    def _():
        for ref in carry_refs:
            ref[...] = jnp.zeros(ref.shape, F32)


def _sub_chunks(tn):
    step = min(SUB_TILE, tn)
    return [slice(k, k + step) for k in range(0, tn, step)]


def _u_scratch(parts, n_sub, seg, tn):
    step = min(SUB_TILE, tn)
    return [pltpu.VMEM((2, n_sub * (seg + SUBLANES), step), F32) for _ in range(parts * (tn // step))]


def _sc_kernel(x_ref, nw_ref, wb_ref, wc_ref, wv_ref, cw_ref, st_ref, w2_ref, o_ref, nb_ref,
               h_ref, *scratch, n_sub, seg, tps):
    j = pl.program_id(1)
    carry_refs, stage_refs = (scratch[:1], scratch[1:]) if tps > 1 else ((), scratch)
    subs = _sub_chunks(wb_ref.shape[1])
    u_refs, b_refs = stage_refs[:len(subs)], stage_refs[len(subs):]

    @pl.when(j == 0)
    def _():
        x = x_ref[...]
        h_ref[...] = _rms(x, nw_ref[...]).astype(BF16)
        o_ref[...] = x
        _clear_carries(carry_refs)

    h = h_ref[...]
    slot = j % 2
    carry = carry_refs[0] if tps > 1 else None
    g_prev, cs_prev = None, None
    for k, cs in enumerate(subs):
        b_refs[k][slot] = _dot(h, wb_ref[:, cs])
        _stage(u_refs[k], (slot,), _dot(h, wc_ref[:, cs]) * _dot(h, wv_ref[:, cs]), st_ref, carry, nb_ref, cs,
               n_sub=n_sub, seg=seg, tps=tps)
        if g_prev is not None:
            o_ref[...] += _dot(g_prev, w2_ref[cs_prev, :])
        y = _staged_conv(u_refs[k], (slot,), cw_ref[:, cs], n_sub=n_sub, seg=seg)
        g_prev, cs_prev = (b_refs[k][slot] * y).astype(BF16), cs
    o_ref[...] += _dot(g_prev, w2_ref[cs_prev, :])


def _sc_layer(x, st8, norm_w, w_in, conv_w, w_out, layer, *, n_seq, seg):
    rows, d = x.shape
    tm, n_sub, tps = _row_tiling(n_seq, seg)
    x_spec = pl.BlockSpec((tm, d), lambda i, j: (i, 0))
    tn = _col_tile(d)
    nj = d // tn
    kern = functools.partial(_sc_kernel, n_sub=n_sub, seg=min(seg, tm), tps=tps)
    step = min(SUB_TILE, tn)
    scratch = [pltpu.VMEM((tm, d), BF16)]
    if tps > 1:
        scratch.append(pltpu.VMEM((nj, SUBLANES, tn), F32))
    scratch += _u_scratch(1, n_sub, min(seg, tm), tn)
    scratch += [pltpu.VMEM((2, tm, step), F32) for _ in range(tn // step)]
    out, nb = pl.pallas_call(
        kern,
        grid=(rows // tm, nj),
        in_specs=[
            x_spec,
            pl.BlockSpec((1, d), lambda i, j: (0, 0)),
            pl.BlockSpec((None, d, tn), lambda i, j: (layer, 0, j)),
            pl.BlockSpec((None, d, tn), lambda i, j: (layer, 0, nj + j)),
            pl.BlockSpec((None, d, tn), lambda i, j: (layer, 0, 2 * nj + j)),
            pl.BlockSpec((conv_w.shape[0], tn), lambda i, j: (0, j)),
            pl.BlockSpec((n_sub, SUBLANES, tn), lambda i, j: (i // tps, 0, j)),
            pl.BlockSpec((None, tn, d), lambda i, j: (layer, j, 0)),
        ],
        out_specs=[
            pl.BlockSpec((tm, d), lambda i, j: (i, 0)),
            _nb_spec(n_sub, nj, tn, tps),
        ],
        out_shape=[jax.ShapeDtypeStruct((rows, d), F32), _nb_shape(n_seq, nj, tn)],
        scratch_shapes=scratch,
        compiler_params=_params("arbitrary", "arbitrary"),
        name="short_conv_mixer",
    )(x, norm_w, w_in, w_in, w_in, conv_w, st8, w_out)
    return out, _nb_merge(nb)


def _ffn_kernel(x_ref, nw_ref, wa_ref, wv_ref, cwa_ref, cwv_ref, ba_ref, bv_ref, sta_ref, stv_ref, w2_ref,
                fw_ref, o_ref, nba_ref, nbv_ref, h_ref, *scratch, n_sub, seg, tps, final_norm):
    j = pl.program_id(1)
    carry_refs, stage_refs = (scratch[:2], scratch[2:]) if tps > 1 else ((), scratch)
    ca, cv = carry_refs if tps > 1 else (None, None)
    subs = _sub_chunks(wa_ref.shape[1])
    ua_refs, uv_refs = stage_refs[:len(subs)], stage_refs[len(subs):]
    stage = functools.partial(_stage, n_sub=n_sub, seg=seg, tps=tps)
    conv = functools.partial(_staged_conv, n_sub=n_sub, seg=seg)

    @pl.when(j == 0)
    def _():
        x = x_ref[...]
        h_ref[...] = _rms(x, nw_ref[...]).astype(BF16)
        o_ref[...] = x
        _clear_carries(carry_refs)

    h = h_ref[...]
    slot = j % 2
    g_prev, cs_prev = None, None
    for k, cs in enumerate(subs):
        stage(ua_refs[k], (slot,), _dot(h, wa_ref[:, cs]), sta_ref, ca, nba_ref, cs)
        stage(uv_refs[k], (slot,), _dot(h, wv_ref[:, cs]), stv_ref, cv, nbv_ref, cs)
        if g_prev is not None:
            o_ref[...] += _dot(g_prev, w2_ref[cs_prev, :])
        a = conv(ua_refs[k], (slot,), cwa_ref[:, cs]) + ba_ref[:, cs]
        v = conv(uv_refs[k], (slot,), cwv_ref[:, cs]) + bv_ref[:, cs]
        g_prev, cs_prev = _silu(a, v).astype(BF16), cs
    o_ref[...] += _dot(g_prev, w2_ref[cs_prev, :])

    if final_norm:
        @pl.when(j == pl.num_programs(1) - 1)
        def _():
            o_ref[...] = _rms(o_ref[...], fw_ref[...])


def _ffn_layer(x, st8, norm_w, w_up, conv_w, conv_b, w_down, final_w, layer, *, n_seq, seg, final_norm):
    rows, d = x.shape
    dff = w_down.shape[1]
    tm, n_sub, tps, x_spec = _residual_tiling(n_seq, seg, d)
    tn = _col_tile(dff)
    nj = dff // tn
    kern = functools.partial(_ffn_kernel, n_sub=n_sub, seg=min(seg, tm), tps=tps, final_norm=final_norm)
    scratch = [pltpu.VMEM((tm, d), BF16)]
    if tps > 1:
        scratch += [pltpu.VMEM((nj, SUBLANES, tn), F32), pltpu.VMEM((nj, SUBLANES, tn), F32)]
    scratch += _u_scratch(2, n_sub, min(seg, tm), tn)
    width = conv_w.shape[0]
    out, nba, nbv = pl.pallas_call(
        kern,
        grid=(rows // tm, nj),
        in_specs=[
            x_spec,
            pl.BlockSpec((1, d), lambda i, j: (0, 0)),
            pl.BlockSpec((None, d, tn), lambda i, j: (layer, 0, j)),
            pl.BlockSpec((None, d, tn), lambda i, j: (layer, 0, nj + j)),
            pl.BlockSpec((width, tn), lambda i, j: (0, j)),
            pl.BlockSpec((width, tn), lambda i, j: (0, nj + j)),
            pl.BlockSpec((1, tn), lambda i, j: (0, j)),
            pl.BlockSpec((1, tn), lambda i, j: (0, nj + j)),
            pl.BlockSpec((n_sub, SUBLANES, tn), lambda i, j: (i // tps, 0, j)),
            pl.BlockSpec((n_sub, SUBLANES, tn), lambda i, j: (i // tps, 0, nj + j)),
            pl.BlockSpec((None, tn, d), lambda i, j: (layer, j, 0)),
            pl.BlockSpec((1, d), lambda i, j: (0, 0)),
        ],
        out_specs=[
            pl.BlockSpec((tm, d), lambda i, j: (i, 0)),
            _nb_spec(n_sub, nj, tn, tps),
            _nb_spec(n_sub, nj, tn, tps),
        ],
        out_shape=[jax.ShapeDtypeStruct((rows, d), F32), _nb_shape(n_seq, nj, tn), _nb_shape(n_seq, nj, tn)],
        scratch_shapes=scratch,
        compiler_params=_params("arbitrary", "arbitrary"),
        name="conv_ffn",
    )(x, norm_w, w_up, w_up, conv_w, conv_w, conv_b, conv_b, st8, st8, w_down, final_w)
    return out, jnp.concatenate([_nb_merge(nba), _nb_merge(nbv)], axis=-1)


def _ssd_in_kernel(x_ref, nw_ref, w_ref, cw_ref, cb_ref, st_ref, wdt_ref, dtb_ref, z_ref, o_ref, nb_ref, dt_ref,
                   h_ref, *scratch, n_sub, seg, tps, njx):
    s = pl.program_id(1)
    carry_refs, bufs = (scratch[:1], scratch[1:]) if tps > 1 else ((), scratch)
    carry = carry_refs[0] if tps > 1 else None
    tn = w_ref.shape[1]
    wide = [slice(k, k + tn // len(bufs)) for k in range(0, tn, tn // len(bufs))]

    @pl.when(s == 0)
    def _():
        h_ref[...] = _rms(x_ref[...], nw_ref[...]).astype(BF16)
        _clear_carries(carry_refs)

    @pl.when(s < njx)
    def _():
        h = h_ref[...]
        for k, cs in enumerate(wide):
            _stage(bufs[k], (), _dot(h, w_ref[:, cs]), st_ref, carry, nb_ref, cs,
                   n_sub=n_sub, seg=seg, tps=tps, chunk=s)
            y = _staged_conv(bufs[k], (), cw_ref[:, cs], n_sub=n_sub, seg=seg)
            o_ref[:, cs] = _silu(y + cb_ref[:, cs])

    @pl.when(s >= njx)
    def _():
        h = h_ref[...]
        for cs in _sub_chunks(tn):
            z_ref[:, cs] = _silu(_dot(h, w_ref[:, cs]))

    @pl.when(s == pl.num_programs(1) - 1)
    def _():
        dt_ref[...] = _softplus(_dot(h_ref[...], wdt_ref[...]) + dtb_ref[...])


def _ssd_in(x, st8, norm_w, w_in, conv_w, conv_b, dt_bias, layer, *, n_seq, seg, inner, heads):
    rows, d = x.shape
    width, conv_dim = conv_w.shape
    cols = inner + conv_dim
    tm, n_sub, tps = _row_tiling(n_seq, seg, WIDE_ROW_TILE if seg >= WIDE_ROW_TILE else ROW_TILE)
    tn = _col_tile(inner, WIDE_COL_TILE)
    assert conv_dim % tn == 0
    nz = inner // tn
    njx = conv_dim // tn
    assert njx >= 2
    w_blk = lambda s: jnp.where(s < njx, nz + s, s - njx)
    mm_blk = lambda s: jnp.minimum(s, njx - 1)
    cv_blk = mm_blk
    z_blk = lambda s: jnp.maximum(s - njx, 0)
    w_dt = w_in[layer, :, cols:]
    seg_t = min(seg, tm)
    step = min(2 * SUB_TILE, tn)
    kern = functools.partial(_ssd_in_kernel, n_sub=n_sub, seg=seg_t, tps=tps, njx=njx)
    scratch = [pltpu.VMEM((tm, d), BF16)]
    if tps > 1:
        scratch.append(pltpu.VMEM((njx, SUBLANES, tn), F32))
    scratch += [pltpu.VMEM((n_sub * (seg_t + SUBLANES), step), F32) for _ in range(tn // step)]
    zs, xbc, nb, dt = pl.pallas_call(
        kern,
        grid=(rows // tm, nz + njx),
        in_specs=[
            pl.BlockSpec((tm, d), lambda i, s: (i, 0)),
            pl.BlockSpec((1, d), lambda i, s: (0, 0)),
            pl.BlockSpec((None, d, tn), lambda i, s: (layer, 0, w_blk(s))),
            pl.BlockSpec((width, tn), lambda i, s: (0, cv_blk(s))),
            pl.BlockSpec((1, tn), lambda i, s: (0, cv_blk(s))),
            pl.BlockSpec((n_sub, SUBLANES, tn), lambda i, s: (i // tps, 0, mm_blk(s))),
            pl.BlockSpec((d, heads), lambda i, s: (0, 0)),
            pl.BlockSpec((1, heads), lambda i, s: (0, 0)),
        ],
        out_specs=[
            pl.BlockSpec((tm, tn), lambda i, s: (i, z_blk(s))),
            pl.BlockSpec((tm, tn), lambda i, s: (i, cv_blk(s))),
            _nb_spec(n_sub, njx, tn, tps),
            pl.BlockSpec((tm, heads), lambda i, s: (i, 0)),
        ],
        out_shape=[jax.ShapeDtypeStruct((rows, inner), F32), jax.ShapeDtypeStruct((rows, conv_dim), F32),
                   _nb_shape(n_seq, njx, tn), jax.ShapeDtypeStruct((rows, heads), F32)],
        scratch_shapes=scratch,
        compiler_params=_params("arbitrary", "arbitrary"),
        name="ssd_in_proj",
    )(x, norm_w, w_in, conv_w, conv_b, st8, w_dt, dt_bias)
    return zs, xbc, dt, _nb_merge(nb)


def _split3(x):
    hi = x.astype(BF16).astype(F32)
    r = x - hi
    mid = r.astype(BF16).astype(F32)
    lo = (r - mid).astype(BF16).astype(F32)
    return hi, mid, lo


def _dot_f32_lhs(x, sel):
    hi, mid, lo = _split3(x)
    return _dot(hi.astype(BF16), sel) + _dot(mid.astype(BF16), sel) + _dot(lo.astype(BF16), sel)


def _dot_f32_rhs(sel, x):
    hi, mid, lo = _split3(x)
    return _dot(sel, hi.astype(BF16)) + _dot(sel, mid.astype(BF16)) + _dot(sel, lo.astype(BF16))


def _scan_kernel(zs_ref, xs_ref, b_ref, c_ref, dt3_ref, dtt_ref, a3_ref, at_ref, dsk_ref, nw_ref, st0_ref,
                 rep3_ref, tril_ref, tri2_ref, mask2_ref, bd_ref, y_ref, stout_ref, st_ref,
                 *, groups, hpg, q, n):
    c = pl.program_id(1)
    gw = hpg * q
    heads = groups * hpg

    @pl.when(c == 0)
    def _():
        for g in range(groups):
            st_ref[g] = st0_ref[0, g].T

    dt3 = dt3_ref[...]
    cs3 = _dot_f32_rhs(tril_ref[...], dt3 * a3_ref[...])
    tail3 = jnp.exp(cs3[q - 1:q] - cs3) * dt3
    hi, mid, lo = _split3(jnp.concatenate([cs3, tail3], axis=0))
    lane = lax.broadcasted_iota(jnp.int32, hi.shape, 1)
    pieces = jnp.where(lane < heads, hi, jnp.where(lane < 2 * heads, mid, lo)).astype(BF16)
    gsl = [slice(g * gw, (g + 1) * gw) for g in range(groups)]
    bcs = [_dot(pieces, rep3_ref[:, gsl[g]]) for g in range(groups)]
    cs_b = [bc[0:q] for bc in bcs]
    tail_b = [bc[q:2 * q] for bc in bcs]
    ecs_b = [jnp.exp(c) for c in cs_b]
    dtt = dtt_ref[0]
    cst = _dot_f32_lhs(dtt * at_ref[...], tri2_ref[...])
    mask2 = mask2_ref[...] > 0.0
    bd = bd_ref[...]

    cgs = [c_ref[:, g * n:(g + 1) * n].astype(BF16) for g in range(groups)]
    bgs = [b_ref[:, g * n:(g + 1) * n].astype(BF16) for g in range(groups)]
    cb2s = [lax.dot_general(cgs[g], jnp.concatenate([bgs[g], bgs[g]], axis=0), (((1,), (1,)), ((), ())),
                            preferred_element_type=F32) for g in range(groups)]
    y_inter = [_dot(cgs[g], st_ref[g].astype(BF16)) for g in range(groups)]
    ds = [lax.dot_general(bgs[g], (xs_ref[:, gsl[g]] * tail_b[g]).astype(BF16), (((0,), (0,)), ((), ())),
                          preferred_element_type=F32) for g in range(groups)]
    ms, xbds = [], []
    for g in range(groups):
        for k in range(hpg // 2):
            psl = slice(g * gw + 2 * k * q, g * gw + 2 * (k + 1) * q)
            pair = g * (hpg // 2) + k
            seg = cs_b[g][:, 2 * k * q:2 * (k + 1) * q] - jnp.broadcast_to(cst[pair:pair + 1], (q, 2 * q))
            w = jnp.where(mask2, jnp.exp(seg), 0.0) * jnp.broadcast_to(dtt[pair:pair + 1], (q, 2 * q))
            ms.append((cb2s[g] * w).astype(BF16))
            xk = xs_ref[:, psl]
            xbds.append((jnp.concatenate([xk, xk], axis=0) * bd).astype(BF16))
    parts = [_dot(m, xbd) for m, xbd in zip(ms, xbds)]
    for g in range(groups):
        st_ref[g] = st_ref[g] * ecs_b[g][q - 1:q] + ds[g]
        pg = parts[g * (hpg // 2):(g + 1) * (hpg // 2)]
        y = pg[0] if len(pg) == 1 else jnp.concatenate(pg, axis=1)
        y = y + y_inter[g] * ecs_b[g]
        y = (y + dsk_ref[:, gsl[g]] * xs_ref[:, gsl[g]]) * zs_ref[:, gsl[g]]
        y = y * lax.rsqrt(jnp.mean(y * y, axis=-1, keepdims=True) + EPS)
        y_ref[:, gsl[g]] = (y * nw_ref[:, gsl[g]]).astype(BF16)

    @pl.when(c == pl.num_programs(1) - 1)
    def _():
        for g in range(groups):
            stout_ref[0, g] = st_ref[g].T


def _ssd_scan(zs, xbc, dt, st_all, layer, a_log, d_skip, norm_w, *, n_seq, seg, inner, groups, n, heads):
    rows = zs.shape[0]
    p = inner // heads
    q = p
    hpg = heads // groups
    assert seg % q == 0 and hpg % 2 == 0 and inner % (groups * n) == 0
    nc = seg // q
    gn = groups * n
    a = -jnp.exp(a_log.astype(F32))
    a3 = jnp.tile(a.reshape(1, heads), (1, 3))
    a_t = jnp.repeat(a.reshape(heads // 2, 2), q, axis=1)
    dt3 = jnp.tile(dt, (1, 3))
    dtt = dt.reshape(rows // q, q, heads // 2, 2).transpose(0, 2, 3, 1).reshape(rows // q, heads // 2, 2 * q)
    dsk = jnp.repeat(d_skip.astype(F32), p).reshape(1, inner)
    rep3 = jnp.tile(jnp.repeat(jnp.eye(heads, dtype=BF16), p, axis=1), (3, 1))
    tril = jnp.tril(jnp.ones((q, q), BF16))
    triu = jnp.triu(jnp.ones((q, q), F32))
    zero = jnp.zeros((q, q), F32)
    tri2 = jnp.block([[triu, zero], [zero, triu]]).astype(BF16)
    mask2 = jnp.concatenate([jnp.tril(jnp.ones((q, q), F32))] * 2, axis=1)
    one = jnp.ones((q, q), F32)
    bd = jnp.block([[one, zero], [zero, one]])

    kern = functools.partial(_scan_kernel, groups=groups, hpg=hpg, q=q, n=n)
    const = lambda s, c: (0, 0)
    operands = [zs, xbc, xbc, xbc, dt3, dtt, a3, a_t, dsk, norm_w, st_all, rep3, tril, tri2, mask2, bd]
    y, st = pl.pallas_call(
        kern,
        grid=(n_seq, nc),
        in_specs=[
            pl.BlockSpec((q, inner), lambda s, c: (s * nc + c, 0)),
            pl.BlockSpec((q, inner), lambda s, c: (s * nc + c, 0)),
            pl.BlockSpec((q, gn), lambda s, c: (s * nc + c, inner // gn)),
            pl.BlockSpec((q, gn), lambda s, c: (s * nc + c, inner // gn + 1)),
            pl.BlockSpec((q, 3 * heads), lambda s, c: (s * nc + c, 0)),
            pl.BlockSpec((1, heads // 2, 2 * q), lambda s, c: (s * nc + c, 0, 0)),
            pl.BlockSpec((1, 3 * heads), const),
            pl.BlockSpec((heads // 2, 2 * q), const),
            pl.BlockSpec((1, inner), const),
            pl.BlockSpec((1, inner), const),
            pl.BlockSpec((None, 1, groups, hpg * p, n), lambda s, c: (layer, s, 0, 0, 0)),
            pl.BlockSpec(rep3.shape, const),
            pl.BlockSpec(tril.shape, const),
            pl.BlockSpec(tri2.shape, const),
            pl.BlockSpec(mask2.shape, const),
            pl.BlockSpec(bd.shape, const),
        ],
        out_specs=[
            pl.BlockSpec((q, inner), lambda s, c: (s * nc + c, 0)),
            pl.BlockSpec((None, 1, groups, hpg * p, n), lambda s, c: (layer, s, 0, 0, 0)),
        ],
        out_shape=[jax.ShapeDtypeStruct((rows, inner), BF16),
                   jax.ShapeDtypeStruct(st_all.shape, F32)],
        scratch_shapes=[pltpu.VMEM((groups, n, hpg * p), F32)],
        input_output_aliases={10: 1},
        compiler_params=_params("arbitrary", "arbitrary"),
        name="ssd_scan",
    )(*operands)
    return y, st


def _out_proj_kernel(x_ref, g_ref, w_ref, o_ref):
    o_ref[...] = x_ref[...] + _dot(g_ref[...], w_ref[...])


def _out_proj(x, g, w, layer, *, n_seq, seg):
    rows, d = x.shape
    k = g.shape[1]
    tm, _, _ = _row_tiling(n_seq, seg)
    return pl.pallas_call(
        _out_proj_kernel,
        grid=(rows // tm,),
        in_specs=[
            pl.BlockSpec((tm, d), lambda i: (i, 0)),
            pl.BlockSpec((tm, k), lambda i: (i, 0)),
            pl.BlockSpec((None, k, d), lambda i: (layer, 0, 0), pipeline_mode=pl.Buffered(1)),
        ],
        out_specs=pl.BlockSpec((tm, d), lambda i: (i, 0)),
        out_shape=jax.ShapeDtypeStruct((rows, d), F32),
        compiler_params=_params("arbitrary"),
        name="ssd_out_proj",
    )(x, g, w)


def _row_tiling(n_seq, seg, row_tile=ROW_TILE):
    if seg >= row_tile:
        assert seg % row_tile == 0
        return row_tile, 1, seg // row_tile
    n_sub = max(1, min(n_seq, row_tile // seg))
    while n_seq % n_sub:
        n_sub -= 1
    return n_sub * seg, n_sub, 1


def _residual_tiling(n_seq, seg, d):
    wide = seg >= WIDE_ROW_TILE
    tm, n_sub, tps = _row_tiling(n_seq, seg, WIDE_ROW_TILE if wide else ROW_TILE)
    mode = dict(pipeline_mode=pl.Buffered(1)) if wide else {}
    return tm, n_sub, tps, pl.BlockSpec((tm, d), lambda i, j: (i, 0), **mode)


def _col_tile(width, pref=COL_TILE):
    tn = pref
    while width % tn:
        tn //= 2
    assert tn >= 128
    return tn


def _nb_spec(n_sub, nj, tn, tps):
    return pl.BlockSpec((n_sub, nj, SUBLANES, tn), lambda i, j: (i // tps, 0, 0, 0))


def _nb_shape(n_seq, nj, tn):
    return jax.ShapeDtypeStruct((n_seq, nj, SUBLANES, tn), F32)


def _nb_merge(nb):
    n_seq, nj, rows, tn = nb.shape
    return nb.transpose(0, 2, 1, 3).reshape(n_seq, rows, nj * tn)


def _pad8(buf):
    return jnp.pad(buf, ((0, 0), (SUBLANES - buf.shape[1], 0), (0, 0)))


def _state_to_kernel(s, groups):
    b, h, p, n = s.shape
    return s.reshape(b, groups, (h // groups) * p, n)


def _state_from_kernel(s, heads):
    b, g, w, n = s.shape
    return s.reshape(b, heads, (g * w) // heads, n)


def _trunk(x, conv_a8, ssd_conv8, ssd_st, ffn8, wts, *, n_seq, seg):
    depth = wts["norm_mix"].shape[0]
    new_a, new_sc, new_f = [], [], []
    for i in range(depth):
        nm = wts["norm_mix"][i][None]
        j = i // 2
        if i % 2 == 0:
            x, nb = _sc_layer(x, conv_a8[j], nm, wts["sc_w_in"], wts["sc_conv_w"][j], wts["sc_w_out"], j,
                              n_seq=n_seq, seg=seg)
            new_a.append(nb)
        else:
            heads = wts["ssd_a_log"].shape[1]
            inner = wts["ssd_w_out"].shape[1]
            groups, n = ssd_st.shape[2], ssd_st.shape[4]
            zs, xbc, dt, nb = _ssd_in(x, ssd_conv8[j], nm, wts["ssd_w_in"], wts["ssd_conv_w"][j],
                                      wts["ssd_conv_b"][j][None], wts["ssd_dt_bias"][j][None], j,
                                      n_seq=n_seq, seg=seg, inner=inner, heads=heads)
            q = inner // heads
            pad = (-seg) % q
            if pad:
                padr = lambda t: jnp.pad(t.reshape(n_seq, seg, -1), ((0, 0), (0, pad), (0, 0))).reshape(
                    n_seq * (seg + pad), -1)
                zs, xbc, dt = padr(zs), padr(xbc), padr(dt)
            g, ssd_st = _ssd_scan(zs, xbc, dt, ssd_st, j, wts["ssd_a_log"][j], wts["ssd_d"][j],
                                  wts["ssd_norm_w"][j][None], n_seq=n_seq, seg=seg + pad, inner=inner,
                                  groups=groups, n=n, heads=heads)
            if pad:
                g = g.reshape(n_seq, seg + pad, -1)[:, :seg].reshape(n_seq * seg, -1)
            x = _out_proj(x, g, wts["ssd_w_out"], j, n_seq=n_seq, seg=seg)
            new_sc.append(nb)
        x, nf = _ffn_layer(x, ffn8[i], wts["norm_ffn"][i][None], wts["ffn_w_up"], wts["ffn_conv_w"][i],
                           wts["ffn_conv_b"][i][None], wts["ffn_w_down"], wts["norm_final"][None], i,
                           n_seq=n_seq, seg=seg, final_norm=(i == depth - 1))
        new_f.append(nf)
    return x, jnp.stack(new_a), jnp.stack(new_sc), ssd_st, jnp.stack(new_f)


def kernel(x_prompt, x_sample, state_conv_a, state_ssd_conv, state_ssd, state_ffn_conv, meta_tokens, norm_mix,
           norm_ffn, norm_final, sc_w_in, sc_conv_w, sc_w_out, ssd_w_in, ssd_conv_w, ssd_conv_b, ssd_dt_bias,
           ssd_a_log, ssd_d, ssd_norm_w, ssd_w_out, ffn_w_up, ffn_conv_w, ffn_conv_b, ffn_w_down):
    b, seq, d = x_prompt.shape
    sb, sseq, _ = x_sample.shape
    n_meta = meta_tokens.shape[0]
    heads = ssd_a_log.shape[1]
    inner = ssd_w_out.shape[1]
    n_state = state_ssd.shape[-1]
    groups = (ssd_conv_w.shape[2] - inner) // (2 * n_state)
    wts = dict(
        norm_mix=norm_mix, norm_ffn=norm_ffn, norm_final=norm_final,
        sc_w_in=sc_w_in.astype(BF16), sc_conv_w=sc_conv_w, sc_w_out=sc_w_out.astype(BF16),
        ssd_w_in=ssd_w_in.astype(BF16), ssd_conv_w=ssd_conv_w, ssd_conv_b=ssd_conv_b, ssd_dt_bias=ssd_dt_bias,
        ssd_a_log=ssd_a_log, ssd_d=ssd_d, ssd_norm_w=ssd_norm_w, ssd_w_out=ssd_w_out.astype(BF16),
        ffn_w_up=ffn_w_up.astype(BF16), ffn_conv_w=ffn_conv_w, ffn_conv_b=ffn_conv_b,
        ffn_w_down=ffn_w_down.astype(BF16))

    def zeros8(like, n_seq):
        return jnp.zeros((like.shape[0], n_seq, SUBLANES, like.shape[-1]), F32)

    zero_st = jnp.zeros((state_ssd.shape[0], 1, groups, inner // groups, n_state), F32)
    _, m_a, m_sc, m_st, m_f = _trunk(
        meta_tokens.astype(F32), zeros8(state_conv_a, 1), zeros8(state_ssd_conv, 1), zero_st,
        zeros8(state_ffn_conv, 1), wts, n_seq=1, seg=n_meta)
    rep = lambda t: jnp.broadcast_to(t, (t.shape[0], b) + t.shape[2:])
    yp, p_a, p_sc, p_st, p_f = _trunk(
        x_prompt.reshape(b * seq, d), rep(m_a), rep(m_sc), rep(m_st), rep(m_f), wts, n_seq=b, seg=seq)
    ys, s_a, s_sc, s_st, s_f = _trunk(
        x_sample.reshape(sb * sseq, d), jax.vmap(_pad8)(state_conv_a), jax.vmap(_pad8)(state_ssd_conv),
        jax.vmap(lambda s: _state_to_kernel(s, groups))(state_ssd.astype(F32)), jax.vmap(_pad8)(state_ffn_conv),
        wts, n_seq=sb, seg=sseq)

    tail = lambda t, like: t[:, :, SUBLANES - like.shape[2]:, :]
    unst = jax.vmap(lambda s: _state_from_kernel(s, heads))
    return (yp.reshape(b, seq, d), ys.reshape(sb, sseq, d),
            tail(p_a, state_conv_a), tail(p_sc, state_ssd_conv), unst(p_st), tail(p_f, state_ffn_conv),
            tail(s_a, state_conv_a), tail(s_sc, state_ssd_conv), unst(s_st), tail(s_f, state_ffn_conv))
```

```python
import functools

import jax
import jax.numpy as jnp
from jax import lax
from jax.experimental import pallas as pl
from jax.experimental.pallas import tpu as pltpu

EPS = 1e-6
F32 = jnp.float32
BF16 = jnp.bfloat16

SUBLANES = 8
VMEM_LIMIT_BYTES = 58 * 1024 * 1024
ROW_TILE = 512
COL_TILE = 512
WIDE_ROW_TILE = 1024
WIDE_COL_TILE = 1024
SUB_TILE = 256
GROUP_BATCH = 4


def _dot(a, b):
    return jnp.dot(a, b, preferred_element_type=F32)


def _params(*sem):
    return pltpu.CompilerParams(dimension_semantics=sem, vmem_limit_bytes=VMEM_LIMIT_BYTES)


def _rms(x, w):
    return x * lax.rsqrt(jnp.mean(x * x, axis=-1, keepdims=True) + EPS) * w


def _silu(x, scale=None):
    half = 0.5 * x
    gate = 1.0 + jnp.tanh(half)
    return half * gate if scale is None else (half * scale) * gate


def _softplus(x):
    return jnp.maximum(x, 0.0) + jnp.log1p(jnp.exp(-jnp.abs(x)))


def _stage(u_ref, idx, u, st_ref, carry_ref, nb_ref, cols, *, n_sub, seg, tps, chunk=None):
    i = pl.program_id(0)
    j = pl.program_id(1) if chunk is None else chunk
    for s in range(n_sub):
        us = u[s * seg:(s + 1) * seg]
        halo = st_ref[s, :, cols]
        if tps > 1:
            halo = jnp.where((i % tps) == 0, halo, carry_ref[j, :, cols])
        base = s * (seg + SUBLANES)
        u_ref[idx + (pl.ds(base, SUBLANES), slice(None))] = halo
        u_ref[idx + (pl.ds(base + SUBLANES, seg), slice(None))] = us
        nb_ref[s, j, :, cols] = us[seg - SUBLANES:seg]
    if tps > 1:
        carry_ref[j, :, cols] = u[u.shape[0] - SUBLANES:]


def _staged_conv(u_ref, idx, w, *, n_sub, seg):
    width = w.shape[0]
    ys = []
    for s in range(n_sub):
        base = s * (seg + SUBLANES) + SUBLANES
        y = None
        for d in range(width):
            term = u_ref[idx + (pl.ds(base - d, seg), slice(None))] * w[width - 1 - d:width - d]
            y = term if y is None else y + term
        ys.append(y)
    return ys[0] if n_sub == 1 else jnp.concatenate(ys, axis=0)


def _staged(u_ref, idx, *, n_sub, seg):
    parts = [u_ref[idx + (pl.ds(s * (seg + SUBLANES) + SUBLANES, seg), slice(None))] for s in range(n_sub)]
    return parts[0] if n_sub == 1 else jnp.concatenate(parts, axis=0)


def _clear_carries(carry_refs):
    @pl.when(pl.program_id(0) == 0)
    def _():
        for ref in carry_refs:
            ref[...] = jnp.zeros(ref.shape, F32)


def _sub_chunks(tn):
    step = min(SUB_TILE, tn)
    return [slice(k, k + step) for k in range(0, tn, step)]


def _u_scratch(parts, n_sub, seg, tn):
    step = min(SUB_TILE, tn)
    return [pltpu.VMEM((2, n_sub * (seg + SUBLANES), step), F32) for _ in range(parts * (tn // step))]


def _sc_kernel(x_ref, nw_ref, wb_ref, wc_ref, wv_ref, cw_ref, st_ref, w2_ref, o_ref, nb_ref,
               h_ref, *scratch, n_sub, seg, tps):
    j = pl.program_id(1)
    carry_refs, stage_refs = (scratch[:1], scratch[1:]) if tps > 1 else ((), scratch)
    subs = _sub_chunks(wb_ref.shape[1])
    u_refs, b_refs = stage_refs[:len(subs)], stage_refs[len(subs):]

    @pl.when(j == 0)
    def _():
        x = x_ref[...]
        h_ref[...] = _rms(x, nw_ref[...]).astype(BF16)
        o_ref[...] = x
        _clear_carries(carry_refs)

    h = h_ref[...]
    slot = j % 2
    carry = carry_refs[0] if tps > 1 else None
    g_prev, cs_prev = None, None
    for k, cs in enumerate(subs):
        b_refs[k][slot] = _dot(h, wb_ref[:, cs])
        _stage(u_refs[k], (slot,), _dot(h, wc_ref[:, cs]) * _dot(h, wv_ref[:, cs]), st_ref, carry, nb_ref, cs,
               n_sub=n_sub, seg=seg, tps=tps)
        if g_prev is not None:
            o_ref[...] += _dot(g_prev, w2_ref[cs_prev, :])
        y = _staged_conv(u_refs[k], (slot,), cw_ref[:, cs], n_sub=n_sub, seg=seg)
        g_prev, cs_prev = (b_refs[k][slot] * y).astype(BF16), cs
    o_ref[...] += _dot(g_prev, w2_ref[cs_prev, :])


def _sc_layer(x, st8, norm_w, w_in, conv_w, w_out, layer, *, n_seq, seg):
    rows, d = x.shape
    tm, n_sub, tps = _row_tiling(n_seq, seg)
    x_spec = pl.BlockSpec((tm, d), lambda i, j: (i, 0))
    tn = _col_tile(d)
    nj = d // tn
    kern = functools.partial(_sc_kernel, n_sub=n_sub, seg=min(seg, tm), tps=tps)
    step = min(SUB_TILE, tn)
    scratch = [pltpu.VMEM((tm, d), BF16)]
    if tps > 1:
        scratch.append(pltpu.VMEM((nj, SUBLANES, tn), F32))
    scratch += _u_scratch(1, n_sub, min(seg, tm), tn)
    scratch += [pltpu.VMEM((2, tm, step), F32) for _ in range(tn // step)]
    out, nb = pl.pallas_call(
        kern,
        grid=(rows // tm, nj),
        in_specs=[
            x_spec,
            pl.BlockSpec((1, d), lambda i, j: (0, 0)),
            pl.BlockSpec((None, d, tn), lambda i, j: (layer, 0, j)),
            pl.BlockSpec((None, d, tn), lambda i, j: (layer, 0, nj + j)),
            pl.BlockSpec((None, d, tn), lambda i, j: (layer, 0, 2 * nj + j)),
            pl.BlockSpec((conv_w.shape[0], tn), lambda i, j: (0, j)),
            pl.BlockSpec((n_sub, SUBLANES, tn), lambda i, j: (i // tps, 0, j)),
            pl.BlockSpec((None, tn, d), lambda i, j: (layer, j, 0)),
        ],
        out_specs=[
            pl.BlockSpec((tm, d), lambda i, j: (i, 0)),
            _nb_spec(n_sub, nj, tn, tps),
        ],
        out_shape=[jax.ShapeDtypeStruct((rows, d), F32), _nb_shape(n_seq, nj, tn)],
        scratch_shapes=scratch,
        compiler_params=_params("arbitrary", "arbitrary"),
        name="short_conv_mixer",
    )(x, norm_w, w_in, w_in, w_in, conv_w, st8, w_out)
    return out, _nb_merge(nb)


def _ffn_kernel(x_ref, nw_ref, wa_ref, wv_ref, cwa_ref, cwv_ref, ba_ref, bv_ref, sta_ref, stv_ref, w2_ref,
                fw_ref, o_ref, nba_ref, nbv_ref, h_ref, *scratch, n_sub, seg, tps, final_norm):
    j = pl.program_id(1)
    carry_refs, stage_refs = (scratch[:2], scratch[2:]) if tps > 1 else ((), scratch)
    ca, cv = carry_refs if tps > 1 else (None, None)
    subs = _sub_chunks(wa_ref.shape[1])
    ua_refs, uv_refs = stage_refs[:len(subs)], stage_refs[len(subs):]
    stage = functools.partial(_stage, n_sub=n_sub, seg=seg, tps=tps)
    conv = functools.partial(_staged_conv, n_sub=n_sub, seg=seg)

    @pl.when(j == 0)
    def _():
        x = x_ref[...]
        h_ref[...] = _rms(x, nw_ref[...]).astype(BF16)
        o_ref[...] = x
        _clear_carries(carry_refs)

    h = h_ref[...]
    slot = j % 2
    g_prev, cs_prev = None, None
    for k, cs in enumerate(subs):
        stage(ua_refs[k], (slot,), _dot(h, wa_ref[:, cs]), sta_ref, ca, nba_ref, cs)
        stage(uv_refs[k], (slot,), _dot(h, wv_ref[:, cs]), stv_ref, cv, nbv_ref, cs)
        if g_prev is not None:
            o_ref[...] += _dot(g_prev, w2_ref[cs_prev, :])
        a = conv(ua_refs[k], (slot,), cwa_ref[:, cs]) + ba_ref[:, cs]
        v = conv(uv_refs[k], (slot,), cwv_ref[:, cs]) + bv_ref[:, cs]
        g_prev, cs_prev = _silu(a, v).astype(BF16), cs
    o_ref[...] += _dot(g_prev, w2_ref[cs_prev, :])

    if final_norm:
        @pl.when(j == pl.num_programs(1) - 1)
        def _():
            o_ref[...] = _rms(o_ref[...], fw_ref[...])


def _ffn_layer(x, st8, norm_w, w_up, conv_w, conv_b, w_down, final_w, layer, *, n_seq, seg, final_norm):
    rows, d = x.shape
    dff = w_down.shape[1]
    tm, n_sub, tps, x_spec = _residual_tiling(n_seq, seg, d)
    tn = _col_tile(dff)
    nj = dff // tn
    kern = functools.partial(_ffn_kernel, n_sub=n_sub, seg=min(seg, tm), tps=tps, final_norm=final_norm)
    scratch = [pltpu.VMEM((tm, d), BF16)]
    if tps > 1:
        scratch += [pltpu.VMEM((nj, SUBLANES, tn), F32), pltpu.VMEM((nj, SUBLANES, tn), F32)]
    scratch += _u_scratch(2, n_sub, min(seg, tm), tn)
    width = conv_w.shape[0]
    out, nba, nbv = pl.pallas_call(
        kern,
        grid=(rows // tm, nj),
        in_specs=[
            x_spec,
            pl.BlockSpec((1, d), lambda i, j: (0, 0)),
            pl.BlockSpec((None, d, tn), lambda i, j: (layer, 0, j)),
            pl.BlockSpec((None, d, tn), lambda i, j: (layer, 0, nj + j)),
            pl.BlockSpec((width, tn), lambda i, j: (0, j)),
            pl.BlockSpec((width, tn), lambda i, j: (0, nj + j)),
            pl.BlockSpec((1, tn), lambda i, j: (0, j)),
            pl.BlockSpec((1, tn), lambda i, j: (0, nj + j)),
            pl.BlockSpec((n_sub, SUBLANES, tn), lambda i, j: (i // tps, 0, j)),
            pl.BlockSpec((n_sub, SUBLANES, tn), lambda i, j: (i // tps, 0, nj + j)),
            pl.BlockSpec((None, tn, d), lambda i, j: (layer, j, 0)),
            pl.BlockSpec((1, d), lambda i, j: (0, 0)),
        ],
        out_specs=[
            pl.BlockSpec((tm, d), lambda i, j: (i, 0)),
            _nb_spec(n_sub, nj, tn, tps),
            _nb_spec(n_sub, nj, tn, tps),
        ],
        out_shape=[jax.ShapeDtypeStruct((rows, d), F32), _nb_shape(n_seq, nj, tn), _nb_shape(n_seq, nj, tn)],
        scratch_shapes=scratch,
        compiler_params=_params("arbitrary", "arbitrary"),
        name="conv_ffn",
    )(x, norm_w, w_up, w_up, conv_w, conv_w, conv_b, conv_b, st8, st8, w_down, final_w)
    return out, jnp.concatenate([_nb_merge(nba), _nb_merge(nbv)], axis=-1)


def _ssd_in_kernel(x_ref, nw_ref, w_ref, cw_ref, cb_ref, st_ref, wdt_ref, dtb_ref, z_ref, o_ref, nb_ref, dt_ref,
                   h_ref, *scratch, n_sub, seg, tps, njx):
    s = pl.program_id(1)
    carry_refs, bufs = (scratch[:1], scratch[1:]) if tps > 1 else ((), scratch)
    carry = carry_refs[0] if tps > 1 else None
    tn = w_ref.shape[1]
    wide = [slice(k, k + tn // len(bufs)) for k in range(0, tn, tn // len(bufs))]

    @pl.when(s == 0)
    def _():
        h_ref[...] = _rms(x_ref[...], nw_ref[...]).astype(BF16)
        _clear_carries(carry_refs)

    @pl.when(s < njx)
    def _():
        h = h_ref[...]
        for k, cs in enumerate(wide):
            _stage(bufs[k], (), _dot(h, w_ref[:, cs]), st_ref, carry, nb_ref, cs,
                   n_sub=n_sub, seg=seg, tps=tps, chunk=s)
            y = _staged_conv(bufs[k], (), cw_ref[:, cs], n_sub=n_sub, seg=seg)
            o_ref[:, cs] = _silu(y + cb_ref[:, cs])

    @pl.when(s >= njx)
    def _():
        h = h_ref[...]
        for cs in _sub_chunks(tn):
            z_ref[:, cs] = _silu(_dot(h, w_ref[:, cs]))

    @pl.when(s == pl.num_programs(1) - 1)
    def _():
        dt_ref[...] = _softplus(_dot(h_ref[...], wdt_ref[...]) + dtb_ref[...])


def _ssd_in(x, st8, norm_w, w_in, conv_w, conv_b, dt_bias, layer, *, n_seq, seg, inner, heads):
    rows, d = x.shape
    width, conv_dim = conv_w.shape
    cols = inner + conv_dim
    tm, n_sub, tps = _row_tiling(n_seq, seg, WIDE_ROW_TILE if seg >= WIDE_ROW_TILE else ROW_TILE)
    tn = _col_tile(inner, WIDE_COL_TILE)
    assert conv_dim % tn == 0
    nz = inner // tn
    njx = conv_dim // tn
    assert njx >= 2
    w_blk = lambda s: jnp.where(s < njx, nz + s, s - njx)
    mm_blk = lambda s: jnp.minimum(s, njx - 1)
    cv_blk = mm_blk
    z_blk = lambda s: jnp.maximum(s - njx, 0)
    w_dt = w_in[layer, :, cols:]
    seg_t = min(seg, tm)
    step = min(2 * SUB_TILE, tn)
    kern = functools.partial(_ssd_in_kernel, n_sub=n_sub, seg=seg_t, tps=tps, njx=njx)
    scratch = [pltpu.VMEM((tm, d), BF16)]
    if tps > 1:
        scratch.append(pltpu.VMEM((njx, SUBLANES, tn), F32))
    scratch += [pltpu.VMEM((n_sub * (seg_t + SUBLANES), step), F32) for _ in range(tn // step)]
    zs, xbc, nb, dt = pl.pallas_call(
        kern,
        grid=(rows // tm, nz + njx),
        in_specs=[
            pl.BlockSpec((tm, d), lambda i, s: (i, 0)),
            pl.BlockSpec((1, d), lambda i, s: (0, 0)),
            pl.BlockSpec((None, d, tn), lambda i, s: (layer, 0, w_blk(s))),
            pl.BlockSpec((width, tn), lambda i, s: (0, cv_blk(s))),
            pl.BlockSpec((1, tn), lambda i, s: (0, cv_blk(s))),
            pl.BlockSpec((n_sub, SUBLANES, tn), lambda i, s: (i // tps, 0, mm_blk(s))),
            pl.BlockSpec((d, heads), lambda i, s: (0, 0)),
            pl.BlockSpec((1, heads), lambda i, s: (0, 0)),
        ],
        out_specs=[
            pl.BlockSpec((tm, tn), lambda i, s: (i, z_blk(s))),
            pl.BlockSpec((tm, tn), lambda i, s: (i, cv_blk(s))),
            _nb_spec(n_sub, njx, tn, tps),
            pl.BlockSpec((tm, heads), lambda i, s: (i, 0)),
        ],
        out_shape=[jax.ShapeDtypeStruct((rows, inner), F32), jax.ShapeDtypeStruct((rows, conv_dim), F32),
                   _nb_shape(n_seq, njx, tn), jax.ShapeDtypeStruct((rows, heads), F32)],
        scratch_shapes=scratch,
        compiler_params=_params("arbitrary", "arbitrary"),
        name="ssd_in_proj",
    )(x, norm_w, w_in, conv_w, conv_b, st8, w_dt, dt_bias)
    return zs, xbc, dt, _nb_merge(nb)


def _split3(x):
    hi = x.astype(BF16).astype(F32)
    r = x - hi
    mid = r.astype(BF16).astype(F32)
    lo = (r - mid).astype(BF16).astype(F32)
    return hi, mid, lo


def _dot_f32_lhs(x, sel):
    hi, mid, lo = _split3(x)
    return _dot(hi.astype(BF16), sel) + _dot(mid.astype(BF16), sel) + _dot(lo.astype(BF16), sel)


def _dot_f32_rhs(sel, x):
    hi, mid, lo = _split3(x)
    return _dot(sel, hi.astype(BF16)) + _dot(sel, mid.astype(BF16)) + _dot(sel, lo.astype(BF16))


def _scan_kernel(zs_ref, xs_ref, b_ref, c_ref, dt3_ref, dtt_ref, a3_ref, at_ref, dsk_ref, nw_ref, st0_ref,
                 rep3_ref, tril_ref, tri2_ref, mask2_ref, bd_ref, *rest, groups, hpg, q, n):
    y_ref, stout_ref, st_ref = rest[-3:]
    c = pl.program_id(1)
    gw = hpg * q
    heads = groups * hpg

    @pl.when(c == 0)
    def _():
        for g in range(groups):
            st_ref[g] = st0_ref[0, g].T

    dt3 = dt3_ref[...]
    cs3 = _dot_f32_rhs(tril_ref[...], dt3 * a3_ref[...])
    tail3 = jnp.exp(cs3[q - 1:q] - cs3) * dt3
    hi, mid, lo = _split3(jnp.concatenate([cs3, tail3], axis=0))
    lane = lax.broadcasted_iota(jnp.int32, hi.shape, 1)
    pieces = jnp.where(lane < heads, hi, jnp.where(lane < 2 * heads, mid, lo)).astype(BF16)
    gsl = [slice(g * gw, (g + 1) * gw) for g in range(groups)]
    bcs = [_dot(pieces, rep3_ref[:, gsl[g]]) for g in range(groups)]
    cs_b = [bc[0:q] for bc in bcs]
    tail_b = [bc[q:2 * q] for bc in bcs]
    ecs_b = [jnp.exp(c) for c in cs_b]
    dtt = dtt_ref[0]
    cst = _dot_f32_lhs(dtt * at_ref[...], tri2_ref[...])
    mask2 = mask2_ref[...] > 0.0
    bd = bd_ref[...]

    for g0 in range(0, groups, GROUP_BATCH):
        gs = range(g0, min(g0 + GROUP_BATCH, groups))
        cgs = {g: c_ref[:, g * n:(g + 1) * n].astype(BF16) for g in gs}
        bgs = {g: b_ref[:, g * n:(g + 1) * n].astype(BF16) for g in gs}
        cb2s = {g: lax.dot_general(cgs[g], jnp.concatenate([bgs[g], bgs[g]], axis=0), (((1,), (1,)), ((), ())),
                                   preferred_element_type=F32) for g in gs}
        y_inter = {g: _dot(cgs[g], st_ref[g].astype(BF16)) for g in gs}
        ds = {g: lax.dot_general(bgs[g], (xs_ref[:, gsl[g]] * tail_b[g]).astype(BF16), (((0,), (0,)), ((), ())),
                                 preferred_element_type=F32) for g in gs}
        ms, xbds = [], []
        for g in gs:
            for k in range(hpg // 2):
                psl = slice(g * gw + 2 * k * q, g * gw + 2 * (k + 1) * q)
                pair = g * (hpg // 2) + k
                seg = cs_b[g][:, 2 * k * q:2 * (k + 1) * q] - jnp.broadcast_to(cst[pair:pair + 1], (q, 2 * q))
                w = jnp.where(mask2, jnp.exp(seg), 0.0) * jnp.broadcast_to(dtt[pair:pair + 1], (q, 2 * q))
                ms.append((cb2s[g] * w).astype(BF16))
                xk = xs_ref[:, psl].astype(BF16)
                xbds.append(jnp.concatenate([xk, xk], axis=0) * bd)
        parts = [_dot(m, xbd) for m, xbd in zip(ms, xbds)]
        for i, g in enumerate(gs):
            st_ref[g] = st_ref[g] * ecs_b[g][q - 1:q] + ds[g]
            pg = parts[i * (hpg // 2):(i + 1) * (hpg // 2)]
            y = pg[0] if len(pg) == 1 else jnp.concatenate(pg, axis=1)
            y = y + y_inter[g] * ecs_b[g]
            y = (y + dsk_ref[:, gsl[g]] * xs_ref[:, gsl[g]]) * zs_ref[:, gsl[g]]
            y = y * lax.rsqrt(jnp.mean(y * y, axis=-1, keepdims=True) + EPS)
            y_ref[:, gsl[g]] = (y * nw_ref[:, gsl[g]]).astype(BF16)

    @pl.when(c == pl.num_programs(1) - 1)
    def _():
        for g in range(groups):
            stout_ref[0, g] = st_ref[g].T


def _ssd_scan(zs, xbc, dt, st_all, st_buf, layer, a_log, d_skip, norm_w, *, n_seq, seg, inner, groups, n, heads):
    rows = zs.shape[0]
    p = inner // heads
    q = p
    hpg = heads // groups
    assert seg % q == 0 and hpg % 2 == 0 and inner % (groups * n) == 0
    nc = seg // q
    gn = groups * n
    a = -jnp.exp(a_log.astype(F32))
    a3 = jnp.tile(a.reshape(1, heads), (1, 3))
    a_t = jnp.repeat(a.reshape(heads // 2, 2), q, axis=1)
    dt3 = jnp.tile(dt, (1, 3))
    dtt = dt.reshape(rows // q, q, heads // 2, 2).transpose(0, 2, 3, 1).reshape(rows // q, heads // 2, 2 * q)
    dsk = jnp.repeat(d_skip.astype(F32), p).reshape(1, inner)
    rep3 = jnp.tile(jnp.repeat(jnp.eye(heads, dtype=BF16), p, axis=1), (3, 1))
    tril = jnp.tril(jnp.ones((q, q), BF16))
    triu = jnp.triu(jnp.ones((q, q), F32))
    zero = jnp.zeros((q, q), F32)
    tri2 = jnp.block([[triu, zero], [zero, triu]]).astype(BF16)
    mask2 = jnp.concatenate([jnp.tril(jnp.ones((q, q), F32))] * 2, axis=1)
    one = jnp.ones((q, q), F32)
    bd = jnp.block([[one, zero], [zero, one]]).astype(BF16)

    kern = functools.partial(_scan_kernel, groups=groups, hpg=hpg, q=q, n=n)
    const = lambda s, c: (0, 0)
    operands = [zs, xbc, xbc, xbc, dt3, dtt, a3, a_t, dsk, norm_w, st_all, rep3, tril, tri2, mask2, bd]
    extra_specs, aliases = [], {}
    if st_buf is not None:
        extra_specs, aliases = [pl.BlockSpec(memory_space=pl.ANY)], {len(operands): 1}
        operands.append(st_buf)
    y, st = pl.pallas_call(
        kern,
        grid=(n_seq, nc),
        in_specs=[
            pl.BlockSpec((q, inner), lambda s, c: (s * nc + c, 0)),
            pl.BlockSpec((q, inner), lambda s, c: (s * nc + c, 0)),
            pl.BlockSpec((q, gn), lambda s, c: (s * nc + c, inner // gn)),
            pl.BlockSpec((q, gn), lambda s, c: (s * nc + c, inner // gn + 1)),
            pl.BlockSpec((q, 3 * heads), lambda s, c: (s * nc + c, 0)),
            pl.BlockSpec((1, heads // 2, 2 * q), lambda s, c: (s * nc + c, 0, 0)),
            pl.BlockSpec((1, 3 * heads), const),
            pl.BlockSpec((heads // 2, 2 * q), const),
            pl.BlockSpec((1, inner), const),
            pl.BlockSpec((1, inner), const),
            pl.BlockSpec((None, 1, groups, hpg * p, n), lambda s, c: (layer, s, 0, 0, 0)),
            pl.BlockSpec(rep3.shape, const),
            pl.BlockSpec(tril.shape, const),
            pl.BlockSpec(tri2.shape, const),
            pl.BlockSpec(mask2.shape, const),
            pl.BlockSpec(bd.shape, const),
        ] + extra_specs,
        out_specs=[
            pl.BlockSpec((q, inner), lambda s, c: (s * nc + c, 0)),
            pl.BlockSpec((None, 1, groups, hpg * p, n), lambda s, c: (layer, s, 0, 0, 0)),
        ],
        out_shape=[jax.ShapeDtypeStruct((rows, inner), BF16),
                   jax.ShapeDtypeStruct(st_all.shape, F32)],
        scratch_shapes=[pltpu.VMEM((groups, n, hpg * p), F32)],
        input_output_aliases=aliases,
        compiler_params=_params("arbitrary", "arbitrary"),
        name="ssd_scan",
    )(*operands)
    return y, st


def _out_proj_kernel(x_ref, g_ref, w_ref, o_ref):
    o_ref[...] = x_ref[...] + _dot(g_ref[...], w_ref[...])


def _out_proj(x, g, w, layer, *, n_seq, seg):
    rows, d = x.shape
    k = g.shape[1]
    tm, _, _ = _row_tiling(n_seq, seg)
    return pl.pallas_call(
        _out_proj_kernel,
        grid=(rows // tm,),
        in_specs=[
            pl.BlockSpec((tm, d), lambda i: (i, 0)),
            pl.BlockSpec((tm, k), lambda i: (i, 0)),
            pl.BlockSpec((None, k, d), lambda i: (layer, 0, 0), pipeline_mode=pl.Buffered(1)),
        ],
        out_specs=pl.BlockSpec((tm, d), lambda i: (i, 0)),
        out_shape=jax.ShapeDtypeStruct((rows, d), F32),
        compiler_params=_params("arbitrary"),
        name="ssd_out_proj",
    )(x, g, w)


def _row_tiling(n_seq, seg, row_tile=ROW_TILE):
    if seg >= row_tile:
        assert seg % row_tile == 0
        return row_tile, 1, seg // row_tile
    n_sub = max(1, min(n_seq, row_tile // seg))
    while n_seq % n_sub:
        n_sub -= 1
    return n_sub * seg, n_sub, 1


def _residual_tiling(n_seq, seg, d):
    wide = seg >= WIDE_ROW_TILE
    tm, n_sub, tps = _row_tiling(n_seq, seg, WIDE_ROW_TILE if wide else ROW_TILE)
    mode = dict(pipeline_mode=pl.Buffered(1)) if wide else {}
    return tm, n_sub, tps, pl.BlockSpec((tm, d), lambda i, j: (i, 0), **mode)


def _col_tile(width, pref=COL_TILE):
    tn = pref
    while width % tn:
        tn //= 2
    assert tn >= 128
    return tn


def _nb_spec(n_sub, nj, tn, tps):
    return pl.BlockSpec((n_sub, nj, SUBLANES, tn), lambda i, j: (i // tps, 0, 0, 0))


def _nb_shape(n_seq, nj, tn):
    return jax.ShapeDtypeStruct((n_seq, nj, SUBLANES, tn), F32)


def _nb_merge(nb):
    n_seq, nj, rows, tn = nb.shape
    return nb.transpose(0, 2, 1, 3).reshape(n_seq, rows, nj * tn)


def _pad8(buf):
    return jnp.pad(buf, ((0, 0), (SUBLANES - buf.shape[1], 0), (0, 0)))


def _state_to_kernel(s, groups):
    b, h, p, n = s.shape
    return s.reshape(b, groups, (h // groups) * p, n)


def _state_from_kernel(s, heads):
    b, g, w, n = s.shape
    return s.reshape(b, heads, (g * w) // heads, n)


def _trunk(x, conv_a8, ssd_conv8, ssd_st, ffn8, wts, *, n_seq, seg):
    depth = wts["norm_mix"].shape[0]
    new_a, new_sc, new_f, new_st = [], [], [], None
    for i in range(depth):
        nm = wts["norm_mix"][i][None]
        j = i // 2
        if i % 2 == 0:
            x, nb = _sc_layer(x, conv_a8[j], nm, wts["sc_w_in"], wts["sc_conv_w"][j], wts["sc_w_out"], j,
                              n_seq=n_seq, seg=seg)
            new_a.append(nb)
        else:
            heads = wts["ssd_a_log"].shape[1]
            inner = wts["ssd_w_out"].shape[1]
            groups, n = ssd_st.shape[2], ssd_st.shape[4]
            zs, xbc, dt, nb = _ssd_in(x, ssd_conv8[j], nm, wts["ssd_w_in"], wts["ssd_conv_w"][j],
                                      wts["ssd_conv_b"][j][None], wts["ssd_dt_bias"][j][None], j,
                                      n_seq=n_seq, seg=seg, inner=inner, heads=heads)
            q = inner // heads
            pad = (-seg) % q
            if pad:
                padr = lambda t: jnp.pad(t.reshape(n_seq, seg, -1), ((0, 0), (0, pad), (0, 0))).reshape(
                    n_seq * (seg + pad), -1)
                zs, xbc, dt = padr(zs), padr(xbc), padr(dt)
            g, new_st = _ssd_scan(zs, xbc, dt, ssd_st, new_st, j, wts["ssd_a_log"][j], wts["ssd_d"][j],
                                  wts["ssd_norm_w"][j][None], n_seq=n_seq, seg=seg + pad, inner=inner,
                                  groups=groups, n=n, heads=heads)
            if pad:
                g = g.reshape(n_seq, seg + pad, -1)[:, :seg].reshape(n_seq * seg, -1)
            x = _out_proj(x, g, wts["ssd_w_out"], j, n_seq=n_seq, seg=seg)
            new_sc.append(nb)
        x, nf = _ffn_layer(x, ffn8[i], wts["norm_ffn"][i][None], wts["ffn_w_up"], wts["ffn_conv_w"][i],
                           wts["ffn_conv_b"][i][None], wts["ffn_w_down"], wts["norm_final"][None], i,
                           n_seq=n_seq, seg=seg, final_norm=(i == depth - 1))
        new_f.append(nf)
    return x, jnp.stack(new_a), jnp.stack(new_sc), new_st, jnp.stack(new_f)


def kernel(x_prompt, x_sample, state_conv_a, state_ssd_conv, state_ssd, state_ffn_conv, meta_tokens, norm_mix,
           norm_ffn, norm_final, sc_w_in, sc_conv_w, sc_w_out, ssd_w_in, ssd_conv_w, ssd_conv_b, ssd_dt_bias,
           ssd_a_log, ssd_d, ssd_norm_w, ssd_w_out, ffn_w_up, ffn_conv_w, ffn_conv_b, ffn_w_down):
    b, seq, d = x_prompt.shape
    sb, sseq, _ = x_sample.shape
    n_meta = meta_tokens.shape[0]
    heads = ssd_a_log.shape[1]
    inner = ssd_w_out.shape[1]
    n_state = state_ssd.shape[-1]
    groups = (ssd_conv_w.shape[2] - inner) // (2 * n_state)
    wts = dict(
        norm_mix=norm_mix, norm_ffn=norm_ffn, norm_final=norm_final,
        sc_w_in=sc_w_in.astype(BF16), sc_conv_w=sc_conv_w, sc_w_out=sc_w_out.astype(BF16),
        ssd_w_in=ssd_w_in.astype(BF16), ssd_conv_w=ssd_conv_w, ssd_conv_b=ssd_conv_b, ssd_dt_bias=ssd_dt_bias,
        ssd_a_log=ssd_a_log, ssd_d=ssd_d, ssd_norm_w=ssd_norm_w, ssd_w_out=ssd_w_out.astype(BF16),
        ffn_w_up=ffn_w_up.astype(BF16), ffn_conv_w=ffn_conv_w, ffn_conv_b=ffn_conv_b,
        ffn_w_down=ffn_w_down.astype(BF16))

    def zeros8(like, n_seq):
        return jnp.zeros((like.shape[0], n_seq, SUBLANES, like.shape[-1]), F32)

    zero_st = jnp.zeros((state_ssd.shape[0], 1, groups, inner // groups, n_state), F32)
    _, m_a, m_sc, m_st, m_f = _trunk(
        meta_tokens.astype(F32), zeros8(state_conv_a, 1), zeros8(state_ssd_conv, 1), zero_st,
        zeros8(state_ffn_conv, 1), wts, n_seq=1, seg=n_meta)
    rep = lambda t: jnp.broadcast_to(t, (t.shape[0], b) + t.shape[2:])
    yp, p_a, p_sc, p_st, p_f = _trunk(
        x_prompt.reshape(b * seq, d), rep(m_a), rep(m_sc), rep(m_st), rep(m_f), wts, n_seq=b, seg=seq)
    ys, s_a, s_sc, s_st, s_f = _trunk(
        x_sample.reshape(sb * sseq, d), jax.vmap(_pad8)(state_conv_a), jax.vmap(_pad8)(state_ssd_conv),
        jax.vmap(lambda s: _state_to_kernel(s, groups))(state_ssd.astype(F32)), jax.vmap(_pad8)(state_ffn_conv),
        wts, n_seq=sb, seg=sseq)

    tail = lambda t, like: t[:, :, SUBLANES - like.shape[2]:, :]
    unst = jax.vmap(lambda s: _state_from_kernel(s, heads))
    return (yp.reshape(b, seq, d), ys.reshape(sb, sseq, d),
            tail(p_a, state_conv_a), tail(p_sc, state_ssd_conv), unst(p_st), tail(p_f, state_ffn_conv),
            tail(s_a, state_conv_a), tail(s_sc, state_ssd_conv), unst(s_st), tail(s_f, state_ffn_conv))
```

```python
import functools

import jax
import jax.numpy as jnp
from jax import lax
from jax.experimental import pallas as pl
from jax.experimental.pallas import tpu as pltpu

EPS = 1e-6
F32 = jnp.float32
BF16 = jnp.bfloat16

SUBLANES = 8
VMEM_LIMIT_BYTES = 58 * 1024 * 1024
ROW_TILE = 512
COL_TILE = 512
WIDE_ROW_TILE = 1024
WIDE_COL_TILE = 1024
SUB_TILE = 256
GROUP_BATCH = 4


def _dot(a, b):
    return jnp.dot(a, b, preferred_element_type=F32)


def _params(*sem):
    return pltpu.CompilerParams(dimension_semantics=sem, vmem_limit_bytes=VMEM_LIMIT_BYTES)


def _rms(x, w):
    return x * lax.rsqrt(jnp.mean(x * x, axis=-1, keepdims=True) + EPS) * w


def _silu(x, scale=None):
    half = 0.5 * x
    gate = 1.0 + jnp.tanh(half)
    return half * gate if scale is None else (half * scale) * gate


def _softplus(x):
    return jnp.maximum(x, 0.0) + jnp.log1p(jnp.exp(-jnp.abs(x)))


def _stage(u_ref, idx, u, st_ref, carry_ref, nb_ref, cols, *, n_sub, seg, tps, chunk=None):
    i = pl.program_id(0)
    j = pl.program_id(1) if chunk is None else chunk
    for s in range(n_sub):
        us = u[s * seg:(s + 1) * seg]
        halo = st_ref[s, :, cols]
        if tps > 1:
            halo = jnp.where((i % tps) == 0, halo, carry_ref[j, :, cols])
        base = s * (seg + SUBLANES)
        u_ref[idx + (pl.ds(base, SUBLANES), slice(None))] = halo
        u_ref[idx + (pl.ds(base + SUBLANES, seg), slice(None))] = us
        nb_ref[s, j, :, cols] = us[seg - SUBLANES:seg]
    if tps > 1:
        carry_ref[j, :, cols] = u[u.shape[0] - SUBLANES:]


def _staged_conv(u_ref, idx, w, *, n_sub, seg):
    width = w.shape[0]
    ys = []
    for s in range(n_sub):
        base = s * (seg + SUBLANES) + SUBLANES
        y = None
        for d in range(width):
            term = u_ref[idx + (pl.ds(base - d, seg), slice(None))] * w[width - 1 - d:width - d]
            y = term if y is None else y + term
        ys.append(y)
    return ys[0] if n_sub == 1 else jnp.concatenate(ys, axis=0)


def _staged(u_ref, idx, *, n_sub, seg):
    parts = [u_ref[idx + (pl.ds(s * (seg + SUBLANES) + SUBLANES, seg), slice(None))] for s in range(n_sub)]
    return parts[0] if n_sub == 1 else jnp.concatenate(parts, axis=0)


def _clear_carries(carry_refs):
    @pl.when(pl.program_id(0) == 0)
    def _():
        for ref in carry_refs:
            ref[...] = jnp.zeros(ref.shape, F32)


def _sub_chunks(tn):
    step = min(SUB_TILE, tn)
    return [slice(k, k + step) for k in range(0, tn, step)]


def _u_scratch(parts, n_sub, seg, tn):
    step = min(SUB_TILE, tn)
    return [pltpu.VMEM((2, n_sub * (seg + SUBLANES), step), F32) for _ in range(parts * (tn // step))]


def _sc_kernel(x_ref, nw_ref, wb_ref, wc_ref, wv_ref, cw_ref, st_ref, w2_ref, o_ref, nb_ref,
               h_ref, *scratch, n_sub, seg, tps):
    j = pl.program_id(1)
    carry_refs, stage_refs = (scratch[:1], scratch[1:]) if tps > 1 else ((), scratch)
    subs = _sub_chunks(wb_ref.shape[1])
    u_refs, b_refs = stage_refs[:len(subs)], stage_refs[len(subs):]

    @pl.when(j == 0)
    def _():
        x = x_ref[...]
        h_ref[...] = _rms(x, nw_ref[...]).astype(BF16)
        o_ref[...] = x
        _clear_carries(carry_refs)

    h = h_ref[...]
    slot = j % 2
    carry = carry_refs[0] if tps > 1 else None
    g_prev, cs_prev = None, None
    for k, cs in enumerate(subs):
        b_refs[k][slot] = _dot(h, wb_ref[:, cs])
        _stage(u_refs[k], (slot,), _dot(h, wc_ref[:, cs]) * _dot(h, wv_ref[:, cs]), st_ref, carry, nb_ref, cs,
               n_sub=n_sub, seg=seg, tps=tps)
        if g_prev is not None:
            o_ref[...] += _dot(g_prev, w2_ref[cs_prev, :])
        y = _staged_conv(u_refs[k], (slot,), cw_ref[:, cs], n_sub=n_sub, seg=seg)
        g_prev, cs_prev = (b_refs[k][slot] * y).astype(BF16), cs
    o_ref[...] += _dot(g_prev, w2_ref[cs_prev, :])


def _sc_layer(x, st8, norm_w, w_in, conv_w, w_out, layer, *, n_seq, seg):
    rows, d = x.shape
    tm, n_sub, tps = _row_tiling(n_seq, seg)
    x_spec = pl.BlockSpec((tm, d), lambda i, j: (i, 0))
    tn = _col_tile(d)
    nj = d // tn
    kern = functools.partial(_sc_kernel, n_sub=n_sub, seg=min(seg, tm), tps=tps)
    step = min(SUB_TILE, tn)
    scratch = [pltpu.VMEM((tm, d), BF16)]
    if tps > 1:
        scratch.append(pltpu.VMEM((nj, SUBLANES, tn), F32))
    scratch += _u_scratch(1, n_sub, min(seg, tm), tn)
    scratch += [pltpu.VMEM((2, tm, step), F32) for _ in range(tn // step)]
    out, nb = pl.pallas_call(
        kern,
        grid=(rows // tm, nj),
        in_specs=[
            x_spec,
            pl.BlockSpec((1, d), lambda i, j: (0, 0)),
            pl.BlockSpec((None, d, tn), lambda i, j: (layer, 0, j)),
            pl.BlockSpec((None, d, tn), lambda i, j: (layer, 0, nj + j)),
            pl.BlockSpec((None, d, tn), lambda i, j: (layer, 0, 2 * nj + j)),
            pl.BlockSpec((conv_w.shape[0], tn), lambda i, j: (0, j)),
            pl.BlockSpec((n_sub, SUBLANES, tn), lambda i, j: (i // tps, 0, j)),
            pl.BlockSpec((None, tn, d), lambda i, j: (layer, j, 0)),
        ],
        out_specs=[
            pl.BlockSpec((tm, d), lambda i, j: (i, 0)),
            _nb_spec(n_sub, nj, tn, tps),
        ],
        out_shape=[jax.ShapeDtypeStruct((rows, d), F32), _nb_shape(n_seq, nj, tn)],
        scratch_shapes=scratch,
        compiler_params=_params("arbitrary", "arbitrary"),
        name="short_conv_mixer",
    )(x, norm_w, w_in, w_in, w_in, conv_w, st8, w_out)
    return out, _nb_merge(nb)


def _ffn_kernel(x_ref, nw_ref, wa_ref, wv_ref, cwa_ref, cwv_ref, ba_ref, bv_ref, sta_ref, stv_ref, w2_ref,
                fw_ref, o_ref, nba_ref, nbv_ref, h_ref, *scratch, n_sub, seg, tps, final_norm):
    j = pl.program_id(1)
    carry_refs, stage_refs = (scratch[:2], scratch[2:]) if tps > 1 else ((), scratch)
    ca, cv = carry_refs if tps > 1 else (None, None)
    subs = _sub_chunks(wa_ref.shape[1])
    ua_refs, uv_refs = stage_refs[:len(subs)], stage_refs[len(subs):]
    stage = functools.partial(_stage, n_sub=n_sub, seg=seg, tps=tps)
    conv = functools.partial(_staged_conv, n_sub=n_sub, seg=seg)

    @pl.when(j == 0)
    def _():
        x = x_ref[...]
        h_ref[...] = _rms(x, nw_ref[...]).astype(BF16)
        o_ref[...] = x
        _clear_carries(carry_refs)

    h = h_ref[...]
    slot = j % 2
    g_prev, cs_prev = None, None
    for k, cs in enumerate(subs):
        stage(ua_refs[k], (slot,), _dot(h, wa_ref[:, cs]), sta_ref, ca, nba_ref, cs)
        stage(uv_refs[k], (slot,), _dot(h, wv_ref[:, cs]), stv_ref, cv, nbv_ref, cs)
        if g_prev is not None:
            o_ref[...] += _dot(g_prev, w2_ref[cs_prev, :])
        a = conv(ua_refs[k], (slot,), cwa_ref[:, cs]) + ba_ref[:, cs]
        v = conv(uv_refs[k], (slot,), cwv_ref[:, cs]) + bv_ref[:, cs]
        g_prev, cs_prev = _silu(a, v).astype(BF16), cs
    o_ref[...] += _dot(g_prev, w2_ref[cs_prev, :])

    if final_norm:
        @pl.when(j == pl.num_programs(1) - 1)
        def _():
            o_ref[...] = _rms(o_ref[...], fw_ref[...])


def _ffn_layer(x, st8, norm_w, w_up, conv_w, conv_b, w_down, final_w, layer, *, n_seq, seg, final_norm):
    rows, d = x.shape
    dff = w_down.shape[1]
    tm, n_sub, tps, x_spec = _residual_tiling(n_seq, seg, d)
    tn = _col_tile(dff)
    nj = dff // tn
    kern = functools.partial(_ffn_kernel, n_sub=n_sub, seg=min(seg, tm), tps=tps, final_norm=final_norm)
    scratch = [pltpu.VMEM((tm, d), BF16)]
    if tps > 1:
        scratch += [pltpu.VMEM((nj, SUBLANES, tn), F32), pltpu.VMEM((nj, SUBLANES, tn), F32)]
    scratch += _u_scratch(2, n_sub, min(seg, tm), tn)
    width = conv_w.shape[0]
    out, nba, nbv = pl.pallas_call(
        kern,
        grid=(rows // tm, nj),
        in_specs=[
            x_spec,
            pl.BlockSpec((1, d), lambda i, j: (0, 0)),
            pl.BlockSpec((None, d, tn), lambda i, j: (layer, 0, j)),
            pl.BlockSpec((None, d, tn), lambda i, j: (layer, 0, nj + j)),
            pl.BlockSpec((width, tn), lambda i, j: (0, j)),
            pl.BlockSpec((width, tn), lambda i, j: (0, nj + j)),
            pl.BlockSpec((1, tn), lambda i, j: (0, j)),
            pl.BlockSpec((1, tn), lambda i, j: (0, nj + j)),
            pl.BlockSpec((n_sub, SUBLANES, tn), lambda i, j: (i // tps, 0, j)),
            pl.BlockSpec((n_sub, SUBLANES, tn), lambda i, j: (i // tps, 0, nj + j)),
            pl.BlockSpec((None, tn, d), lambda i, j: (layer, j, 0)),
            pl.BlockSpec((1, d), lambda i, j: (0, 0)),
        ],
        out_specs=[
            pl.BlockSpec((tm, d), lambda i, j: (i, 0)),
            _nb_spec(n_sub, nj, tn, tps),
            _nb_spec(n_sub, nj, tn, tps),
        ],
        out_shape=[jax.ShapeDtypeStruct((rows, d), F32), _nb_shape(n_seq, nj, tn), _nb_shape(n_seq, nj, tn)],
        scratch_shapes=scratch,
        compiler_params=_params("arbitrary", "arbitrary"),
        name="conv_ffn",
    )(x, norm_w, w_up, w_up, conv_w, conv_w, conv_b, conv_b, st8, st8, w_down, final_w)
    return out, jnp.concatenate([_nb_merge(nba), _nb_merge(nbv)], axis=-1)


def _ssd_in_kernel(x_ref, nw_ref, wx_ref, wz_ref, cw_ref, cb_ref, st_ref, wdt_ref, dtb_ref, z_ref, o_ref, nb_ref,
                   dt_ref, h_ref, *scratch, n_sub, seg, tps, njx, nzc):
    s = pl.program_id(1)
    carry_refs, bufs = (scratch[:1], scratch[1:]) if tps > 1 else ((), scratch)
    carry = carry_refs[0] if tps > 1 else None
    tn = wx_ref.shape[1]
    wide = [slice(k, k + tn // len(bufs)) for k in range(0, tn, tn // len(bufs))]
    z_subs = _sub_chunks(wz_ref.shape[1])

    @pl.when(s == 0)
    def _():
        h_ref[...] = _rms(x_ref[...], nw_ref[...]).astype(BF16)
        _clear_carries(carry_refs)

    def step(with_x, with_z):
        h = h_ref[...]
        for k, cs in enumerate(wide if with_x else []):
            _stage(bufs[k], (), _dot(h, wx_ref[:, cs]), st_ref, carry, nb_ref, cs,
                   n_sub=n_sub, seg=seg, tps=tps, chunk=s)
            if k < len(wide) - 1:
                o_ref[:, cs] = _silu(_staged_conv(bufs[k], (), cw_ref[:, cs], n_sub=n_sub, seg=seg) + cb_ref[:, cs])
        zs = [_dot(h, wz_ref[:, cs]) for cs in z_subs] if with_z else []
        if with_x:
            k, cs = len(wide) - 1, wide[-1]
            o_ref[:, cs] = _silu(_staged_conv(bufs[k], (), cw_ref[:, cs], n_sub=n_sub, seg=seg) + cb_ref[:, cs])
        for cs, z in zip(z_subs, zs):
            z_ref[:, cs] = _silu(z)

    both = min(njx, nzc)

    @pl.when(s < both)
    def _():
        step(True, True)

    if njx != nzc:
        @pl.when(s >= both)
        def _():
            step(njx > nzc, nzc > njx)

    @pl.when(s == pl.num_programs(1) - 1)
    def _():
        dt_ref[...] = _softplus(_dot(h_ref[...], wdt_ref[...]) + dtb_ref[...])


def _ssd_in(x, st8, norm_w, w_in, conv_w, conv_b, dt_bias, layer, *, n_seq, seg, inner, heads):
    rows, d = x.shape
    width, conv_dim = conv_w.shape
    cols = inner + conv_dim
    tm, n_sub, tps = _row_tiling(n_seq, seg, WIDE_ROW_TILE if seg >= WIDE_ROW_TILE else ROW_TILE)
    tn = _col_tile(inner, WIDE_COL_TILE)
    assert conv_dim % tn == 0
    njx = conv_dim // tn
    tz = tn // 2
    nzc = inner // tz
    assert inner % tn == 0
    x_blk = lambda s: jnp.minimum(s, njx - 1)
    z_blk = lambda s: jnp.minimum(s, nzc - 1)
    w_dt = w_in[layer, :, cols:]
    seg_t = min(seg, tm)
    step = min(2 * SUB_TILE, tn)
    kern = functools.partial(_ssd_in_kernel, n_sub=n_sub, seg=seg_t, tps=tps, njx=njx, nzc=nzc)
    scratch = [pltpu.VMEM((tm, d), BF16)]
    if tps > 1:
        scratch.append(pltpu.VMEM((njx, SUBLANES, tn), F32))
    scratch += [pltpu.VMEM((n_sub * (seg_t + SUBLANES), step), F32) for _ in range(tn // step)]
    zs, xbc, nb, dt = pl.pallas_call(
        kern,
        grid=(rows // tm, max(njx, nzc)),
        in_specs=[
            pl.BlockSpec((tm, d), lambda i, s: (i, 0)),
            pl.BlockSpec((1, d), lambda i, s: (0, 0)),
            pl.BlockSpec((None, d, tn), lambda i, s: (layer, 0, inner // tn + x_blk(s))),
            pl.BlockSpec((None, d, tz), lambda i, s: (layer, 0, z_blk(s))),
            pl.BlockSpec((width, tn), lambda i, s: (0, x_blk(s))),
            pl.BlockSpec((1, tn), lambda i, s: (0, x_blk(s))),
            pl.BlockSpec((n_sub, SUBLANES, tn), lambda i, s: (i // tps, 0, x_blk(s))),
            pl.BlockSpec((d, heads), lambda i, s: (0, 0)),
            pl.BlockSpec((1, heads), lambda i, s: (0, 0)),
        ],
        out_specs=[
            pl.BlockSpec((tm, tz), lambda i, s: (i, z_blk(s))),
            pl.BlockSpec((tm, tn), lambda i, s: (i, x_blk(s))),
            _nb_spec(n_sub, njx, tn, tps),
            pl.BlockSpec((tm, heads), lambda i, s: (i, 0)),
        ],
        out_shape=[jax.ShapeDtypeStruct((rows, inner), F32), jax.ShapeDtypeStruct((rows, conv_dim), F32),
                   _nb_shape(n_seq, njx, tn), jax.ShapeDtypeStruct((rows, heads), F32)],
        scratch_shapes=scratch,
        compiler_params=_params("arbitrary", "arbitrary"),
        name="ssd_in_proj",
    )(x, norm_w, w_in, w_in, conv_w, conv_b, st8, w_dt, dt_bias)
    return zs, xbc, dt, _nb_merge(nb)


def _split3(x):
    hi = x.astype(BF16).astype(F32)
    r = x - hi
    mid = r.astype(BF16).astype(F32)
    lo = (r - mid).astype(BF16).astype(F32)
    return hi, mid, lo


def _dot_f32_lhs(x, sel):
    hi, mid, lo = _split3(x)
    return _dot(hi.astype(BF16), sel) + _dot(mid.astype(BF16), sel) + _dot(lo.astype(BF16), sel)


def _dot_f32_rhs(sel, x):
    hi, mid, lo = _split3(x)
    return _dot(sel, hi.astype(BF16)) + _dot(sel, mid.astype(BF16)) + _dot(sel, lo.astype(BF16))


def _scan_kernel(zs_ref, xs_ref, b_ref, c_ref, dt3_ref, dtt_ref, a3_ref, at_ref, dsk_ref, nw_ref, st0_ref,
                 rep3_ref, tril_ref, tri2_ref, mask2_ref, bd_ref, *rest, groups, hpg, q, n):
    y_ref, stout_ref, st_ref = rest[-3:]
    c = pl.program_id(1)
    gw = hpg * q
    heads = groups * hpg

    @pl.when(c == 0)
    def _():
        for g in range(groups):
            st_ref[g] = st0_ref[0, g * hpg:(g + 1) * hpg].reshape(gw, n).T

    dt3 = dt3_ref[...]
    cs3 = _dot_f32_rhs(tril_ref[...], dt3 * a3_ref[...])
    tail3 = jnp.exp(cs3[q - 1:q] - cs3) * dt3
    hi, mid, lo = _split3(jnp.concatenate([cs3, tail3], axis=0))
    lane = lax.broadcasted_iota(jnp.int32, hi.shape, 1)
    pieces = jnp.where(lane < heads, hi, jnp.where(lane < 2 * heads, mid, lo)).astype(BF16)
    gsl = [slice(g * gw, (g + 1) * gw) for g in range(groups)]
    bcs = [_dot(pieces, rep3_ref[:, gsl[g]]) for g in range(groups)]
    cs_b = [bc[0:q] for bc in bcs]
    tail_b = [bc[q:2 * q] for bc in bcs]
    ecs_b = [jnp.exp(c) for c in cs_b]
    dtt = dtt_ref[0]
    cst = _dot_f32_lhs(dtt * at_ref[...], tri2_ref[...])
    mask2 = mask2_ref[...] > 0.0
    bd = bd_ref[...]

    for g0 in range(0, groups, GROUP_BATCH):
        gs = range(g0, min(g0 + GROUP_BATCH, groups))
        cgs = {g: c_ref[:, g * n:(g + 1) * n].astype(BF16) for g in gs}
        bgs = {g: b_ref[:, g * n:(g + 1) * n].astype(BF16) for g in gs}
        cb2s = {g: lax.dot_general(cgs[g], jnp.concatenate([bgs[g], bgs[g]], axis=0), (((1,), (1,)), ((), ())),
                                   preferred_element_type=F32) for g in gs}
        y_inter = {g: _dot(cgs[g], st_ref[g].astype(BF16)) for g in gs}
        ds = {g: lax.dot_general(bgs[g], (xs_ref[:, gsl[g]] * tail_b[g]).astype(BF16), (((0,), (0,)), ((), ())),
                                 preferred_element_type=F32) for g in gs}
        ms, xbds = [], []
        for g in gs:
            for k in range(hpg // 2):
                psl = slice(g * gw + 2 * k * q, g * gw + 2 * (k + 1) * q)
                pair = g * (hpg // 2) + k
                seg = cs_b[g][:, 2 * k * q:2 * (k + 1) * q] - jnp.broadcast_to(cst[pair:pair + 1], (q, 2 * q))
                w = jnp.where(mask2, jnp.exp(seg), 0.0) * jnp.broadcast_to(dtt[pair:pair + 1], (q, 2 * q))
                ms.append((cb2s[g] * w).astype(BF16))
                xk = xs_ref[:, psl].astype(BF16)
                xbds.append(jnp.concatenate([xk, xk], axis=0) * bd)
        parts = [_dot(m, xbd) for m, xbd in zip(ms, xbds)]
        for i, g in enumerate(gs):
            st_ref[g] = st_ref[g] * ecs_b[g][q - 1:q] + ds[g]
            pg = parts[i * (hpg // 2):(i + 1) * (hpg // 2)]
            y = pg[0] if len(pg) == 1 else jnp.concatenate(pg, axis=1)
            y = y + y_inter[g] * ecs_b[g]
            y = (y + dsk_ref[:, gsl[g]] * xs_ref[:, gsl[g]]) * zs_ref[:, gsl[g]]
            y = y * lax.rsqrt(jnp.mean(y * y, axis=-1, keepdims=True) + EPS)
            y_ref[:, gsl[g]] = (y * nw_ref[:, gsl[g]]).astype(BF16)

    @pl.when(c == pl.num_programs(1) - 1)
    def _():
        for g in range(groups):
            stout_ref[0, g * hpg:(g + 1) * hpg] = st_ref[g].T.reshape(hpg, q, n)


def _ssd_scan(zs, xbc, dt, st_all, st_buf, layer, a_log, d_skip, norm_w, *, n_seq, seg, inner, groups, n, heads):
    rows = zs.shape[0]
    p = inner // heads
    q = p
    hpg = heads // groups
    assert seg % q == 0 and hpg % 2 == 0 and inner % (groups * n) == 0
    nc = seg // q
    gn = groups * n
    a = -jnp.exp(a_log.astype(F32))
    a3 = jnp.tile(a.reshape(1, heads), (1, 3))
    a_t = jnp.repeat(a.reshape(heads // 2, 2), q, axis=1)
    dt3 = jnp.tile(dt, (1, 3))
    dtt = dt.reshape(rows // q, q, heads // 2, 2).transpose(0, 2, 3, 1).reshape(rows // q, heads // 2, 2 * q)
    dsk = jnp.repeat(d_skip.astype(F32), p).reshape(1, inner)
    rep3 = jnp.tile(jnp.repeat(jnp.eye(heads, dtype=BF16), p, axis=1), (3, 1))
    tril = jnp.tril(jnp.ones((q, q), BF16))
    triu = jnp.triu(jnp.ones((q, q), F32))
    zero = jnp.zeros((q, q), F32)
    tri2 = jnp.block([[triu, zero], [zero, triu]]).astype(BF16)
    mask2 = jnp.concatenate([jnp.tril(jnp.ones((q, q), F32))] * 2, axis=1)
    one = jnp.ones((q, q), F32)
    bd = jnp.block([[one, zero], [zero, one]]).astype(BF16)

    kern = functools.partial(_scan_kernel, groups=groups, hpg=hpg, q=q, n=n)
    const = lambda s, c: (0, 0)
    operands = [zs, xbc, xbc, xbc, dt3, dtt, a3, a_t, dsk, norm_w, st_all, rep3, tril, tri2, mask2, bd]
    extra_specs, aliases = [], {}
    if st_buf is not None:
        extra_specs, aliases = [pl.BlockSpec(memory_space=pl.ANY)], {len(operands): 1}
        operands.append(st_buf)
    y, st = pl.pallas_call(
        kern,
        grid=(n_seq, nc),
        in_specs=[
            pl.BlockSpec((q, inner), lambda s, c: (s * nc + c, 0)),
            pl.BlockSpec((q, inner), lambda s, c: (s * nc + c, 0)),
            pl.BlockSpec((q, gn), lambda s, c: (s * nc + c, inner // gn)),
            pl.BlockSpec((q, gn), lambda s, c: (s * nc + c, inner // gn + 1)),
            pl.BlockSpec((q, 3 * heads), lambda s, c: (s * nc + c, 0)),
            pl.BlockSpec((1, heads // 2, 2 * q), lambda s, c: (s * nc + c, 0, 0)),
            pl.BlockSpec((1, 3 * heads), const),
            pl.BlockSpec((heads // 2, 2 * q), const),
            pl.BlockSpec((1, inner), const),
            pl.BlockSpec((1, inner), const),
            pl.BlockSpec((None, 1, heads, p, n), lambda s, c: (layer, s, 0, 0, 0)),
            pl.BlockSpec(rep3.shape, const),
            pl.BlockSpec(tril.shape, const),
            pl.BlockSpec(tri2.shape, const),
            pl.BlockSpec(mask2.shape, const),
            pl.BlockSpec(bd.shape, const),
        ] + extra_specs,
        out_specs=[
            pl.BlockSpec((q, inner), lambda s, c: (s * nc + c, 0)),
            pl.BlockSpec((None, 1, heads, p, n), lambda s, c: (layer, s, 0, 0, 0)),
        ],
        out_shape=[jax.ShapeDtypeStruct((rows, inner), BF16),
                   jax.ShapeDtypeStruct(st_all.shape, F32)],
        scratch_shapes=[pltpu.VMEM((groups, n, hpg * p), F32)],
        input_output_aliases=aliases,
        compiler_params=_params("arbitrary", "arbitrary"),
        name="ssd_scan",
    )(*operands)
    return y, st


def _out_proj_kernel(x_ref, g_ref, w_ref, o_ref):
    o_ref[...] = x_ref[...] + _dot(g_ref[...], w_ref[...])


def _out_proj(x, g, w, layer, *, n_seq, seg):
    rows, d = x.shape
    k = g.shape[1]
    tm, _, _ = _row_tiling(n_seq, seg)
    return pl.pallas_call(
        _out_proj_kernel,
        grid=(rows // tm,),
        in_specs=[
            pl.BlockSpec((tm, d), lambda i: (i, 0)),
            pl.BlockSpec((tm, k), lambda i: (i, 0)),
            pl.BlockSpec((None, k, d), lambda i: (layer, 0, 0), pipeline_mode=pl.Buffered(1)),
        ],
        out_specs=pl.BlockSpec((tm, d), lambda i: (i, 0)),
        out_shape=jax.ShapeDtypeStruct((rows, d), F32),
        compiler_params=_params("arbitrary"),
        name="ssd_out_proj",
    )(x, g, w)


def _row_tiling(n_seq, seg, row_tile=ROW_TILE):
    if seg >= row_tile:
        assert seg % row_tile == 0
        return row_tile, 1, seg // row_tile
    n_sub = max(1, min(n_seq, row_tile // seg))
    while n_seq % n_sub:
        n_sub -= 1
    return n_sub * seg, n_sub, 1


def _residual_tiling(n_seq, seg, d):
    wide = seg >= WIDE_ROW_TILE
    tm, n_sub, tps = _row_tiling(n_seq, seg, WIDE_ROW_TILE if wide else ROW_TILE)
    mode = dict(pipeline_mode=pl.Buffered(1)) if wide else {}
    return tm, n_sub, tps, pl.BlockSpec((tm, d), lambda i, j: (i, 0), **mode)


def _col_tile(width, pref=COL_TILE):
    tn = pref
    while width % tn:
        tn //= 2
    assert tn >= 128
    return tn


def _nb_spec(n_sub, nj, tn, tps):
    return pl.BlockSpec((n_sub, nj, SUBLANES, tn), lambda i, j: (i // tps, 0, 0, 0))


def _nb_shape(n_seq, nj, tn):
    return jax.ShapeDtypeStruct((n_seq, nj, SUBLANES, tn), F32)


def _nb_merge(nb):
    n_seq, nj, rows, tn = nb.shape
    return nb.transpose(0, 2, 1, 3).reshape(n_seq, rows, nj * tn)


def _pad8(buf):
    return jnp.pad(buf, ((0, 0), (SUBLANES - buf.shape[1], 0), (0, 0)))


def _trunk(x, conv_a8, ssd_conv8, ssd_st, ffn8, wts, *, n_seq, seg, groups):
    depth = wts["norm_mix"].shape[0]
    new_a, new_sc, new_f, new_st = [], [], [], None
    for i in range(depth):
        nm = wts["norm_mix"][i][None]
        j = i // 2
        if i % 2 == 0:
            x, nb = _sc_layer(x, conv_a8[j], nm, wts["sc_w_in"], wts["sc_conv_w"][j], wts["sc_w_out"], j,
                              n_seq=n_seq, seg=seg)
            new_a.append(nb)
        else:
            heads = wts["ssd_a_log"].shape[1]
            inner = wts["ssd_w_out"].shape[1]
            n = ssd_st.shape[4]
            zs, xbc, dt, nb = _ssd_in(x, ssd_conv8[j], nm, wts["ssd_w_in"], wts["ssd_conv_w"][j],
                                      wts["ssd_conv_b"][j][None], wts["ssd_dt_bias"][j][None], j,
                                      n_seq=n_seq, seg=seg, inner=inner, heads=heads)
            q = inner // heads
            pad = (-seg) % q
            if pad:
                padr = lambda t: jnp.pad(t.reshape(n_seq, seg, -1), ((0, 0), (0, pad), (0, 0))).reshape(
                    n_seq * (seg + pad), -1)
                zs, xbc, dt = padr(zs), padr(xbc), padr(dt)
            g, new_st = _ssd_scan(zs, xbc, dt, ssd_st, new_st, j, wts["ssd_a_log"][j], wts["ssd_d"][j],
                                  wts["ssd_norm_w"][j][None], n_seq=n_seq, seg=seg + pad, inner=inner,
                                  groups=groups, n=n, heads=heads)
            if pad:
                g = g.reshape(n_seq, seg + pad, -1)[:, :seg].reshape(n_seq * seg, -1)
            x = _out_proj(x, g, wts["ssd_w_out"], j, n_seq=n_seq, seg=seg)
            new_sc.append(nb)
        x, nf = _ffn_layer(x, ffn8[i], wts["norm_ffn"][i][None], wts["ffn_w_up"], wts["ffn_conv_w"][i],
                           wts["ffn_conv_b"][i][None], wts["ffn_w_down"], wts["norm_final"][None], i,
                           n_seq=n_seq, seg=seg, final_norm=(i == depth - 1))
        new_f.append(nf)
    return x, jnp.stack(new_a), jnp.stack(new_sc), new_st, jnp.stack(new_f)


def kernel(x_prompt, x_sample, state_conv_a, state_ssd_conv, state_ssd, state_ffn_conv, meta_tokens, norm_mix,
           norm_ffn, norm_final, sc_w_in, sc_conv_w, sc_w_out, ssd_w_in, ssd_conv_w, ssd_conv_b, ssd_dt_bias,
           ssd_a_log, ssd_d, ssd_norm_w, ssd_w_out, ffn_w_up, ffn_conv_w, ffn_conv_b, ffn_w_down):
    b, seq, d = x_prompt.shape
    sb, sseq, _ = x_sample.shape
    n_meta = meta_tokens.shape[0]
    inner = ssd_w_out.shape[1]
    n_state = state_ssd.shape[-1]
    groups = (ssd_conv_w.shape[2] - inner) // (2 * n_state)
    wts = dict(
        norm_mix=norm_mix, norm_ffn=norm_ffn, norm_final=norm_final,
        sc_w_in=sc_w_in.astype(BF16), sc_conv_w=sc_conv_w, sc_w_out=sc_w_out.astype(BF16),
        ssd_w_in=ssd_w_in.astype(BF16), ssd_conv_w=ssd_conv_w, ssd_conv_b=ssd_conv_b, ssd_dt_bias=ssd_dt_bias,
        ssd_a_log=ssd_a_log, ssd_d=ssd_d, ssd_norm_w=ssd_norm_w, ssd_w_out=ssd_w_out.astype(BF16),
        ffn_w_up=ffn_w_up.astype(BF16), ffn_conv_w=ffn_conv_w, ffn_conv_b=ffn_conv_b,
        ffn_w_down=ffn_w_down.astype(BF16))

    def zeros8(like, n_seq):
        return jnp.zeros((like.shape[0], n_seq, SUBLANES, like.shape[-1]), F32)

    zero_st = jnp.zeros((state_ssd.shape[0], 1) + state_ssd.shape[2:], F32)
    _, m_a, m_sc, m_st, m_f = _trunk(
        meta_tokens.astype(F32), zeros8(state_conv_a, 1), zeros8(state_ssd_conv, 1), zero_st,
        zeros8(state_ffn_conv, 1), wts, n_seq=1, seg=n_meta, groups=groups)
    rep = lambda t: jnp.broadcast_to(t, (t.shape[0], b) + t.shape[2:])
    yp, p_a, p_sc, p_st, p_f = _trunk(
        x_prompt.reshape(b * seq, d), rep(m_a), rep(m_sc), rep(m_st), rep(m_f), wts, n_seq=b, seg=seq,
        groups=groups)
    ys, s_a, s_sc, s_st, s_f = _trunk(
        x_sample.reshape(sb * sseq, d), jax.vmap(_pad8)(state_conv_a), jax.vmap(_pad8)(state_ssd_conv),
        state_ssd.astype(F32), jax.vmap(_pad8)(state_ffn_conv), wts, n_seq=sb, seg=sseq, groups=groups)

    tail = lambda t, like: t[:, :, SUBLANES - like.shape[2]:, :]
    return (yp.reshape(b, seq, d), ys.reshape(sb, sseq, d),
            tail(p_a, state_conv_a), tail(p_sc, state_ssd_conv), p_st, tail(p_f, state_ffn_conv),
            tail(s_a, state_conv_a), tail(s_sc, state_ssd_conv), s_st, tail(s_f, state_ffn_conv))
```

```python
import functools

import jax
import jax.numpy as jnp
from jax import lax
from jax.experimental import pallas as pl
from jax.experimental.pallas import tpu as pltpu

EPS = 1e-6
F32 = jnp.float32
BF16 = jnp.bfloat16

SUBLANES = 8
VMEM_LIMIT_BYTES = 58 * 1024 * 1024
ROW_TILE = 512
WIDE_ROW_TILE = 1024
COL_TILE = 512
WIDE_COL_TILE = 1024
SUB_TILE = 256
GROUP_BATCH = 4


def _dot(a, b):
    return jnp.dot(a, b, preferred_element_type=F32)


def _params(*sem):
    return pltpu.CompilerParams(dimension_semantics=sem, vmem_limit_bytes=VMEM_LIMIT_BYTES)


def _rms(x, w):
    return x * lax.rsqrt(jnp.mean(x * x, axis=-1, keepdims=True) + EPS) * w


def _silu(x, scale=None):
    half = 0.5 * x
    gate = 1.0 + jnp.tanh(half)
    return half * gate if scale is None else (half * scale) * gate


def _softplus(x):
    return jnp.maximum(x, 0.0) + jnp.log1p(jnp.exp(-jnp.abs(x)))


def _stage(u_ref, idx, u, st_ref, carry_ref, nb_ref, cols, *, n_sub, seg, tps, chunk=None):
    i = pl.program_id(0)
    j = pl.program_id(1) if chunk is None else chunk
    for s in range(n_sub):
        us = u[s * seg:(s + 1) * seg]
        halo = st_ref[s, :, cols]
        if tps > 1:
            halo = jnp.where((i % tps) == 0, halo, carry_ref[j, :, cols])
        base = s * (seg + SUBLANES)
        u_ref[idx + (pl.ds(base, SUBLANES), slice(None))] = halo
        u_ref[idx + (pl.ds(base + SUBLANES, seg), slice(None))] = us
        nb_ref[s, j, :, cols] = us[seg - SUBLANES:seg]
    if tps > 1:
        carry_ref[j, :, cols] = u[u.shape[0] - SUBLANES:]


def _staged_conv(u_ref, idx, w, *, n_sub, seg):
    width = w.shape[0]
    ys = []
    for s in range(n_sub):
        base = s * (seg + SUBLANES) + SUBLANES
        y = None
        for d in range(width):
            term = u_ref[idx + (pl.ds(base - d, seg), slice(None))] * w[width - 1 - d:width - d]
            y = term if y is None else y + term
        ys.append(y)
    return ys[0] if n_sub == 1 else jnp.concatenate(ys, axis=0)


def _clear_carries(carry_refs):
    @pl.when(pl.program_id(0) == 0)
    def _():
        for ref in carry_refs:
            ref[...] = jnp.zeros(ref.shape, F32)


def _sub_chunks(tn):
    step = min(SUB_TILE, tn)
    return [slice(k, k + step) for k in range(0, tn, step)]


def _u_scratch(parts, n_sub, seg, tn):
    step = min(SUB_TILE, tn)
    return [pltpu.VMEM((2, n_sub * (seg + SUBLANES), step), F32) for _ in range(parts * (tn // step))]


def _sc_kernel(x_ref, nw_ref, wb_ref, wc_ref, wv_ref, cw_ref, st_ref, w2_ref, o_ref, nb_ref,
               h_ref, *scratch, n_sub, seg, tps):
    j = pl.program_id(1)
    carry_refs, stage_refs = (scratch[:1], scratch[1:]) if tps > 1 else ((), scratch)
    subs = _sub_chunks(wb_ref.shape[1])
    u_refs, b_refs = stage_refs[:len(subs)], stage_refs[len(subs):]

    @pl.when(j == 0)
    def _():
        x = x_ref[...]
        h_ref[...] = _rms(x, nw_ref[...]).astype(BF16)
        o_ref[...] = x
        _clear_carries(carry_refs)

    h = h_ref[...]
    slot = j % 2
    carry = carry_refs[0] if tps > 1 else None
    g_prev, cs_prev = None, None
    for k, cs in enumerate(subs):
        b_refs[k][slot] = _dot(h, wb_ref[:, cs])
        _stage(u_refs[k], (slot,), _dot(h, wc_ref[:, cs]) * _dot(h, wv_ref[:, cs]), st_ref, carry, nb_ref, cs,
               n_sub=n_sub, seg=seg, tps=tps)
        if g_prev is not None:
            o_ref[...] += _dot(g_prev, w2_ref[cs_prev, :])
        y = _staged_conv(u_refs[k], (slot,), cw_ref[:, cs], n_sub=n_sub, seg=seg)
        g_prev, cs_prev = (b_refs[k][slot] * y).astype(BF16), cs
    o_ref[...] += _dot(g_prev, w2_ref[cs_prev, :])


def _sc_layer(x, st8, norm_w, w_in, conv_w, w_out, layer, *, n_seq, seg):
    rows, d = x.shape
    tm, n_sub, tps = _row_tiling(n_seq, seg)
    tn = _col_tile(d)
    nj = d // tn
    kern = functools.partial(_sc_kernel, n_sub=n_sub, seg=min(seg, tm), tps=tps)
    step = min(SUB_TILE, tn)
    scratch = [pltpu.VMEM((tm, d), BF16)]
    if tps > 1:
        scratch.append(pltpu.VMEM((nj, SUBLANES, tn), F32))
    scratch += _u_scratch(1, n_sub, min(seg, tm), tn)
    scratch += [pltpu.VMEM((2, tm, step), F32) for _ in range(tn // step)]
    out, nb = pl.pallas_call(
        kern,
        grid=(rows // tm, nj),
        in_specs=[
            pl.BlockSpec((tm, d), lambda i, j: (i, 0)),
            pl.BlockSpec((1, d), lambda i, j: (0, 0)),
            pl.BlockSpec((None, d, tn), lambda i, j: (layer, 0, j)),
            pl.BlockSpec((None, d, tn), lambda i, j: (layer, 0, nj + j)),
            pl.BlockSpec((None, d, tn), lambda i, j: (layer, 0, 2 * nj + j)),
            pl.BlockSpec((conv_w.shape[0], tn), lambda i, j: (0, j)),
            pl.BlockSpec((n_sub, SUBLANES, tn), lambda i, j: (i // tps, 0, j)),
            pl.BlockSpec((None, tn, d), lambda i, j: (layer, j, 0)),
        ],
        out_specs=[
            pl.BlockSpec((tm, d), lambda i, j: (i, 0)),
            _nb_spec(n_sub, nj, tn, tps),
        ],
        out_shape=[jax.ShapeDtypeStruct((rows, d), F32), _nb_shape(n_seq, nj, tn)],
        scratch_shapes=scratch,
        compiler_params=_params("arbitrary", "arbitrary"),
        name="short_conv_mixer",
    )(x, norm_w, w_in, w_in, w_in, conv_w, st8, w_out)
    return out, _nb_merge(nb)


def _ffn_kernel(x_ref, nw_ref, wa_ref, wv_ref, cwa_ref, cwv_ref, ba_ref, bv_ref, sta_ref, stv_ref, w2_ref,
                fw_ref, o_ref, nba_ref, nbv_ref, h_ref, *scratch, n_sub, seg, tps, final_norm):
    j = pl.program_id(1)
    carry_refs, stage_refs = (scratch[:2], scratch[2:]) if tps > 1 else ((), scratch)
    ca, cv = carry_refs if tps > 1 else (None, None)
    subs = _sub_chunks(wa_ref.shape[1])
    ua_refs, uv_refs = stage_refs[:len(subs)], stage_refs[len(subs):]
    stage = functools.partial(_stage, n_sub=n_sub, seg=seg, tps=tps)
    conv = functools.partial(_staged_conv, n_sub=n_sub, seg=seg)

    @pl.when(j == 0)
    def _():
        x = x_ref[...]
        h_ref[...] = _rms(x, nw_ref[...]).astype(BF16)
        o_ref[...] = x
        _clear_carries(carry_refs)

    h = h_ref[...]
    slot = j % 2
    g_prev, cs_prev = None, None
    for k, cs in enumerate(subs):
        stage(ua_refs[k], (slot,), _dot(h, wa_ref[:, cs]), sta_ref, ca, nba_ref, cs)
        stage(uv_refs[k], (slot,), _dot(h, wv_ref[:, cs]), stv_ref, cv, nbv_ref, cs)
        if g_prev is not None:
            o_ref[...] += _dot(g_prev, w2_ref[cs_prev, :])
        a = conv(ua_refs[k], (slot,), cwa_ref[:, cs]) + ba_ref[:, cs]
        v = conv(uv_refs[k], (slot,), cwv_ref[:, cs]) + bv_ref[:, cs]
        g_prev, cs_prev = _silu(a, v).astype(BF16), cs
    o_ref[...] += _dot(g_prev, w2_ref[cs_prev, :])

    if final_norm:
        @pl.when(j == pl.num_programs(1) - 1)
        def _():
            o_ref[...] = _rms(o_ref[...], fw_ref[...])


def _ffn_layer(x, st8, norm_w, w_up, conv_w, conv_b, w_down, final_w, layer, *, n_seq, seg, final_norm):
    rows, d = x.shape
    dff = w_down.shape[1]
    tm, n_sub, tps, x_spec = _residual_tiling(n_seq, seg, d)
    tn = _col_tile(dff)
    nj = dff // tn
    kern = functools.partial(_ffn_kernel, n_sub=n_sub, seg=min(seg, tm), tps=tps, final_norm=final_norm)
    scratch = [pltpu.VMEM((tm, d), BF16)]
    if tps > 1:
        scratch += [pltpu.VMEM((nj, SUBLANES, tn), F32), pltpu.VMEM((nj, SUBLANES, tn), F32)]
    scratch += _u_scratch(2, n_sub, min(seg, tm), tn)
    width = conv_w.shape[0]
    out, nba, nbv = pl.pallas_call(
        kern,
        grid=(rows // tm, nj),
        in_specs=[
            x_spec,
            pl.BlockSpec((1, d), lambda i, j: (0, 0)),
            pl.BlockSpec((None, d, tn), lambda i, j: (layer, 0, j)),
            pl.BlockSpec((None, d, tn), lambda i, j: (layer, 0, nj + j)),
            pl.BlockSpec((width, tn), lambda i, j: (0, j)),
            pl.BlockSpec((width, tn), lambda i, j: (0, nj + j)),
            pl.BlockSpec((1, tn), lambda i, j: (0, j)),
            pl.BlockSpec((1, tn), lambda i, j: (0, nj + j)),
            pl.BlockSpec((n_sub, SUBLANES, tn), lambda i, j: (i // tps, 0, j)),
            pl.BlockSpec((n_sub, SUBLANES, tn), lambda i, j: (i // tps, 0, nj + j)),
            pl.BlockSpec((None, tn, d), lambda i, j: (layer, j, 0)),
            pl.BlockSpec((1, d), lambda i, j: (0, 0)),
        ],
        out_specs=[
            pl.BlockSpec((tm, d), lambda i, j: (i, 0)),
            _nb_spec(n_sub, nj, tn, tps),
            _nb_spec(n_sub, nj, tn, tps),
        ],
        out_shape=[jax.ShapeDtypeStruct((rows, d), F32), _nb_shape(n_seq, nj, tn), _nb_shape(n_seq, nj, tn)],
        scratch_shapes=scratch,
        compiler_params=_params("arbitrary", "arbitrary"),
        name="conv_ffn",
    )(x, norm_w, w_up, w_up, conv_w, conv_w, conv_b, conv_b, st8, st8, w_down, final_w)
    return out, jnp.concatenate([_nb_merge(nba), _nb_merge(nbv)], axis=-1)


def _ssd_in_kernel(x_ref, nw_ref, w_ref, cw_ref, cb_ref, st_ref, wdt_ref, dtb_ref, z_ref, o_ref, nb_ref, dt_ref,
                   h_ref, *scratch, n_sub, seg, tps, njx):
    s = pl.program_id(1)
    carry_refs, bufs = (scratch[:1], scratch[1:]) if tps > 1 else ((), scratch)
    carry = carry_refs[0] if tps > 1 else None
    tn = w_ref.shape[1]
    wide = [slice(k, k + tn // len(bufs)) for k in range(0, tn, tn // len(bufs))]

    @pl.when(s == 0)
    def _():
        h_ref[...] = _rms(x_ref[...], nw_ref[...]).astype(BF16)
        _clear_carries(carry_refs)

    @pl.when(s < njx)
    def _():
        h = h_ref[...]
        for k, cs in enumerate(wide):
            _stage(bufs[k], (), _dot(h, w_ref[:, cs]), st_ref, carry, nb_ref, cs,
                   n_sub=n_sub, seg=seg, tps=tps, chunk=s)
            y = _staged_conv(bufs[k], (), cw_ref[:, cs], n_sub=n_sub, seg=seg)
            o_ref[:, cs] = _silu(y + cb_ref[:, cs])

    @pl.when(s >= njx)
    def _():
        h = h_ref[...]
        for cs in _sub_chunks(tn):
            z_ref[:, cs] = _silu(_dot(h, w_ref[:, cs]))

    @pl.when(s == pl.num_programs(1) - 1)
    def _():
        dt_ref[...] = _softplus(_dot(h_ref[...], wdt_ref[...]) + dtb_ref[...])


def _ssd_in(x, st8, norm_w, w_in, conv_w, conv_b, dt_bias, layer, *, n_seq, seg, inner, heads):
    rows, d = x.shape
    width, conv_dim = conv_w.shape
    cols = inner + conv_dim
    tm, n_sub, tps = _row_tiling(n_seq, seg, WIDE_ROW_TILE if seg >= WIDE_ROW_TILE else ROW_TILE)
    tn = _col_tile(inner, WIDE_COL_TILE)
    assert conv_dim % tn == 0
    nz = inner // tn
    njx = conv_dim // tn
    w_blk = lambda s: jnp.where(s < njx, nz + s, s - njx)
    x_blk = lambda s: jnp.minimum(s, njx - 1)
    z_blk = lambda s: jnp.maximum(s - njx, 0)
    w_dt = w_in[layer, :, cols:]
    seg_t = min(seg, tm)
    step = min(2 * SUB_TILE, tn)
    kern = functools.partial(_ssd_in_kernel, n_sub=n_sub, seg=seg_t, tps=tps, njx=njx)
    scratch = [pltpu.VMEM((tm, d), BF16)]
    if tps > 1:
        scratch.append(pltpu.VMEM((njx, SUBLANES, tn), F32))
    scratch += [pltpu.VMEM((n_sub * (seg_t + SUBLANES), step), F32) for _ in range(tn // step)]
    zs, xbc, nb, dt = pl.pallas_call(
        kern,
        grid=(rows // tm, nz + njx),
        in_specs=[
            pl.BlockSpec((tm, d), lambda i, s: (i, 0)),
            pl.BlockSpec((1, d), lambda i, s: (0, 0)),
            pl.BlockSpec((None, d, tn), lambda i, s: (layer, 0, w_blk(s))),
            pl.BlockSpec((width, tn), lambda i, s: (0, x_blk(s))),
            pl.BlockSpec((1, tn), lambda i, s: (0, x_blk(s))),
            pl.BlockSpec((n_sub, SUBLANES, tn), lambda i, s: (i // tps, 0, x_blk(s))),
            pl.BlockSpec((d, heads), lambda i, s: (0, 0)),
            pl.BlockSpec((1, heads), lambda i, s: (0, 0)),
        ],
        out_specs=[
            pl.BlockSpec((tm, tn), lambda i, s: (i, z_blk(s))),
            pl.BlockSpec((tm, tn), lambda i, s: (i, x_blk(s))),
            _nb_spec(n_sub, njx, tn, tps),
            pl.BlockSpec((tm, heads), lambda i, s: (i, 0)),
        ],
        out_shape=[jax.ShapeDtypeStruct((rows, inner), F32), jax.ShapeDtypeStruct((rows, conv_dim), F32),
                   _nb_shape(n_seq, njx, tn), jax.ShapeDtypeStruct((rows, heads), F32)],
        scratch_shapes=scratch,
        compiler_params=_params("arbitrary", "arbitrary"),
        name="ssd_in_proj",
    )(x, norm_w, w_in, conv_w, conv_b, st8, w_dt, dt_bias)
    return zs, xbc, dt, _nb_merge(nb)


def _split3(x):
    hi = x.astype(BF16).astype(F32)
    r = x - hi
    mid = r.astype(BF16).astype(F32)
    lo = (r - mid).astype(BF16).astype(F32)
    return hi, mid, lo


def _dot_f32_lhs(x, sel):
    hi, mid, lo = _split3(x)
    return _dot(hi.astype(BF16), sel) + _dot(mid.astype(BF16), sel) + _dot(lo.astype(BF16), sel)


def _dot_f32_rhs(sel, x):
    hi, mid, lo = _split3(x)
    return _dot(sel, hi.astype(BF16)) + _dot(sel, mid.astype(BF16)) + _dot(sel, lo.astype(BF16))


def _scan_kernel(zs_ref, xs_ref, b_ref, c_ref, dt3_ref, dtt_ref, a3_ref, at_ref, dsk_ref, nw_ref, st0_ref,
                 rep3_ref, tril_ref, tri2_ref, mask2_ref, bd_ref, y_ref, stout_ref, st_ref,
                 *, groups, hpg, q, n):
    c = pl.program_id(1)
    gw = hpg * q
    heads = groups * hpg

    @pl.when(c == 0)
    def _():
        for g in range(groups):
            st_ref[g] = st0_ref[0, g].T

    dt3 = dt3_ref[...]
    cs3 = _dot_f32_rhs(tril_ref[...], dt3 * a3_ref[...])
    tail3 = jnp.exp(cs3[q - 1:q] - cs3) * dt3
    hi, mid, lo = _split3(jnp.concatenate([cs3, tail3], axis=0))
    lane = lax.broadcasted_iota(jnp.int32, hi.shape, 1)
    pieces = jnp.where(lane < heads, hi, jnp.where(lane < 2 * heads, mid, lo)).astype(BF16)
    gsl = [slice(g * gw, (g + 1) * gw) for g in range(groups)]
    bcs = [_dot(pieces, rep3_ref[:, gsl[g]]) for g in range(groups)]
    cs_b = [bc[0:q] for bc in bcs]
    tail_b = [bc[q:2 * q] for bc in bcs]
    ecs_b = [jnp.exp(cs) for cs in cs_b]
    dtt = dtt_ref[0]
    cst = _dot_f32_lhs(dtt * at_ref[...], tri2_ref[...])
    mask2 = mask2_ref[...] > 0.0
    bd = bd_ref[...]

    for g0 in range(0, groups, GROUP_BATCH):
        gs = range(g0, min(g0 + GROUP_BATCH, groups))
        cgs = {g: c_ref[:, g * n:(g + 1) * n].astype(BF16) for g in gs}
        bgs = {g: b_ref[:, g * n:(g + 1) * n].astype(BF16) for g in gs}
        cb2s = {g: lax.dot_general(cgs[g], jnp.concatenate([bgs[g], bgs[g]], axis=0), (((1,), (1,)), ((), ())),
                                   preferred_element_type=F32) for g in gs}
        y_inter = {g: _dot(cgs[g], st_ref[g].astype(BF16)) for g in gs}
        ds = {g: lax.dot_general(bgs[g], (xs_ref[:, gsl[g]] * tail_b[g]).astype(BF16), (((0,), (0,)), ((), ())),
                                 preferred_element_type=F32) for g in gs}
        ms, xbds = [], []
        for g in gs:
            for k in range(hpg // 2):
                psl = slice(g * gw + 2 * k * q, g * gw + 2 * (k + 1) * q)
                pair = g * (hpg // 2) + k
                seg = cs_b[g][:, 2 * k * q:2 * (k + 1) * q] - jnp.broadcast_to(cst[pair:pair + 1], (q, 2 * q))
                w = jnp.where(mask2, jnp.exp(seg), 0.0) * jnp.broadcast_to(dtt[pair:pair + 1], (q, 2 * q))
                ms.append((cb2s[g] * w).astype(BF16))
                xk = xs_ref[:, psl].astype(BF16)
                xbds.append(jnp.concatenate([xk, xk], axis=0) * bd)
        parts = [_dot(m, xbd) for m, xbd in zip(ms, xbds)]
        for i, g in enumerate(gs):
            st_ref[g] = st_ref[g] * ecs_b[g][q - 1:q] + ds[g]
            pg = parts[i * (hpg // 2):(i + 1) * (hpg // 2)]
            y = pg[0] if len(pg) == 1 else jnp.concatenate(pg, axis=1)
            y = y + y_inter[g] * ecs_b[g]
            y = (y + dsk_ref[:, gsl[g]] * xs_ref[:, gsl[g]]) * zs_ref[:, gsl[g]]
            y = y * lax.rsqrt(jnp.mean(y * y, axis=-1, keepdims=True) + EPS)
            y_ref[:, gsl[g]] = (y * nw_ref[:, gsl[g]]).astype(BF16)

    @pl.when(c == pl.num_programs(1) - 1)
    def _():
        for g in range(groups):
            stout_ref[0, g] = st_ref[g].T


def _ssd_scan(zs, xbc, dt, st_all, layer, a_log, d_skip, norm_w, *, n_seq, seg, inner, groups, n, heads):
    rows = zs.shape[0]
    p = inner // heads
    q = p
    hpg = heads // groups
    assert seg % q == 0 and hpg % 2 == 0 and inner % (groups * n) == 0
    nc = seg // q
    gn = groups * n
    a = -jnp.exp(a_log.astype(F32))
    a3 = jnp.tile(a.reshape(1, heads), (1, 3))
    a_t = jnp.repeat(a.reshape(heads // 2, 2), q, axis=1)
    dt3 = jnp.tile(dt, (1, 3))
    dtt = dt.reshape(rows // q, q, heads // 2, 2).transpose(0, 2, 3, 1).reshape(rows // q, heads // 2, 2 * q)
    dsk = jnp.repeat(d_skip.astype(F32), p).reshape(1, inner)
    rep3 = jnp.tile(jnp.repeat(jnp.eye(heads, dtype=BF16), p, axis=1), (3, 1))
    tril = jnp.tril(jnp.ones((q, q), BF16))
    triu = jnp.triu(jnp.ones((q, q), F32))
    zero = jnp.zeros((q, q), F32)
    tri2 = jnp.block([[triu, zero], [zero, triu]]).astype(BF16)
    mask2 = jnp.concatenate([jnp.tril(jnp.ones((q, q), F32))] * 2, axis=1)
    one = jnp.ones((q, q), F32)
    bd = jnp.block([[one, zero], [zero, one]]).astype(BF16)

    kern = functools.partial(_scan_kernel, groups=groups, hpg=hpg, q=q, n=n)
    const = lambda s, c: (0, 0)
    operands = [zs, xbc, xbc, xbc, dt3, dtt, a3, a_t, dsk, norm_w, st_all, rep3, tril, tri2, mask2, bd]
    y, st = pl.pallas_call(
        kern,
        grid=(n_seq, nc),
        in_specs=[
            pl.BlockSpec((q, inner), lambda s, c: (s * nc + c, 0)),
            pl.BlockSpec((q, inner), lambda s, c: (s * nc + c, 0)),
            pl.BlockSpec((q, gn), lambda s, c: (s * nc + c, inner // gn)),
            pl.BlockSpec((q, gn), lambda s, c: (s * nc + c, inner // gn + 1)),
            pl.BlockSpec((q, 3 * heads), lambda s, c: (s * nc + c, 0)),
            pl.BlockSpec((1, heads // 2, 2 * q), lambda s, c: (s * nc + c, 0, 0)),
            pl.BlockSpec((1, 3 * heads), const),
            pl.BlockSpec((heads // 2, 2 * q), const),
            pl.BlockSpec((1, inner), const),
            pl.BlockSpec((1, inner), const),
            pl.BlockSpec((None, 1, groups, hpg * p, n), lambda s, c: (layer, s, 0, 0, 0)),
            pl.BlockSpec(rep3.shape, const),
            pl.BlockSpec(tril.shape, const),
            pl.BlockSpec(tri2.shape, const),
            pl.BlockSpec(mask2.shape, const),
            pl.BlockSpec(bd.shape, const),
        ],
        out_specs=[
            pl.BlockSpec((q, inner), lambda s, c: (s * nc + c, 0)),
            pl.BlockSpec((None, 1, groups, hpg * p, n), lambda s, c: (layer, s, 0, 0, 0)),
        ],
        out_shape=[jax.ShapeDtypeStruct((rows, inner), BF16),
                   jax.ShapeDtypeStruct(st_all.shape, F32)],
        scratch_shapes=[pltpu.VMEM((groups, n, hpg * p), F32)],
        input_output_aliases={next(i for i, o in enumerate(operands) if o is st_all): 1},
        compiler_params=_params("arbitrary", "arbitrary"),
        name="ssd_scan",
    )(*operands)
    return y, st


def _out_proj_kernel(x_ref, g_ref, w_ref, o_ref):
    o_ref[...] = x_ref[...] + _dot(g_ref[...], w_ref[...])


def _out_proj(x, g, w, layer, *, n_seq, seg):
    rows, d = x.shape
    k = g.shape[1]
    tm, _, _ = _row_tiling(n_seq, seg)
    return pl.pallas_call(
        _out_proj_kernel,
        grid=(rows // tm,),
        in_specs=[
            pl.BlockSpec((tm, d), lambda i: (i, 0)),
            pl.BlockSpec((tm, k), lambda i: (i, 0)),
            pl.BlockSpec((None, k, d), lambda i: (layer, 0, 0), pipeline_mode=pl.Buffered(1)),
        ],
        out_specs=pl.BlockSpec((tm, d), lambda i: (i, 0)),
        out_shape=jax.ShapeDtypeStruct((rows, d), F32),
        compiler_params=_params("arbitrary"),
        name="ssd_out_proj",
    )(x, g, w)


def _row_tiling(n_seq, seg, row_tile=ROW_TILE):
    if seg >= row_tile:
        assert seg % row_tile == 0
        return row_tile, 1, seg // row_tile
    n_sub = max(1, min(n_seq, row_tile // seg))
    while n_seq % n_sub:
        n_sub -= 1
    return n_sub * seg, n_sub, 1


def _residual_tiling(n_seq, seg, d):
    wide = seg >= WIDE_ROW_TILE
    tm, n_sub, tps = _row_tiling(n_seq, seg, WIDE_ROW_TILE if wide else ROW_TILE)
    mode = dict(pipeline_mode=pl.Buffered(1)) if wide else {}
    return tm, n_sub, tps, pl.BlockSpec((tm, d), lambda i, j: (i, 0), **mode)


def _col_tile(width, pref=COL_TILE):
    tn = pref
    while width % tn:
        tn //= 2
    assert tn >= 128
    return tn


def _nb_spec(n_sub, nj, tn, tps):
    return pl.BlockSpec((n_sub, nj, SUBLANES, tn), lambda i, j: (i // tps, 0, 0, 0))


def _nb_shape(n_seq, nj, tn):
    return jax.ShapeDtypeStruct((n_seq, nj, SUBLANES, tn), F32)


def _nb_merge(nb):
    n_seq, nj, rows, tn = nb.shape
    return nb.transpose(0, 2, 1, 3).reshape(n_seq, rows, nj * tn)


def _pad8(buf):
    return jnp.pad(buf, ((0, 0), (SUBLANES - buf.shape[1], 0), (0, 0)))


def _state_to_kernel(s, groups):
    b, h, p, n = s.shape
    return s.reshape(b, groups, (h // groups) * p, n)


def _state_from_kernel(s, heads):
    b, g, w, n = s.shape
    return s.reshape(b, heads, (g * w) // heads, n)


def _trunk(x, conv_a8, ssd_conv8, ssd_st, ffn8, wts, *, n_seq, seg):
    depth = wts["norm_mix"].shape[0]
    new_a, new_sc, new_f = [], [], []
    for i in range(depth):
        nm = wts["norm_mix"][i][None]
        j = i // 2
        if i % 2 == 0:
            x, nb = _sc_layer(x, conv_a8[j], nm, wts["sc_w_in"], wts["sc_conv_w"][j], wts["sc_w_out"], j,
                              n_seq=n_seq, seg=seg)
            new_a.append(nb)
        else:
            heads = wts["ssd_a_log"].shape[1]
            inner = wts["ssd_w_out"].shape[1]
            groups, n = ssd_st.shape[2], ssd_st.shape[4]
            zs, xbc, dt, nb = _ssd_in(x, ssd_conv8[j], nm, wts["ssd_w_in"], wts["ssd_conv_w"][j],
                                      wts["ssd_conv_b"][j][None], wts["ssd_dt_bias"][j][None], j,
                                      n_seq=n_seq, seg=seg, inner=inner, heads=heads)
            q = inner // heads
            pad = (-seg) % q
            if pad:
                padr = lambda t: jnp.pad(t.reshape(n_seq, seg, -1), ((0, 0), (0, pad), (0, 0))).reshape(
                    n_seq * (seg + pad), -1)
                zs, xbc, dt = padr(zs), padr(xbc), padr(dt)
            g, ssd_st = _ssd_scan(zs, xbc, dt, ssd_st, j, wts["ssd_a_log"][j], wts["ssd_d"][j],
                                  wts["ssd_norm_w"][j][None], n_seq=n_seq, seg=seg + pad, inner=inner,
                                  groups=groups, n=n, heads=heads)
            if pad:
                g = g.reshape(n_seq, seg + pad, -1)[:, :seg].reshape(n_seq * seg, -1)
            x = _out_proj(x, g, wts["ssd_w_out"], j, n_seq=n_seq, seg=seg)
            new_sc.append(nb)
        x, nf = _ffn_layer(x, ffn8[i], wts["norm_ffn"][i][None], wts["ffn_w_up"], wts["ffn_conv_w"][i],
                           wts["ffn_conv_b"][i][None], wts["ffn_w_down"], wts["norm_final"][None], i,
                           n_seq=n_seq, seg=seg, final_norm=(i == depth - 1))
        new_f.append(nf)
    return x, jnp.stack(new_a), jnp.stack(new_sc), ssd_st, jnp.stack(new_f)


def kernel(x_prompt, x_sample, state_conv_a, state_ssd_conv, state_ssd, state_ffn_conv, meta_tokens, norm_mix,
           norm_ffn, norm_final, sc_w_in, sc_conv_w, sc_w_out, ssd_w_in, ssd_conv_w, ssd_conv_b, ssd_dt_bias,
           ssd_a_log, ssd_d, ssd_norm_w, ssd_w_out, ffn_w_up, ffn_conv_w, ffn_conv_b, ffn_w_down):
    b, seq, d = x_prompt.shape
    sb, sseq, _ = x_sample.shape
    n_meta = meta_tokens.shape[0]
    heads = ssd_a_log.shape[1]
    inner = ssd_w_out.shape[1]
    n_state = state_ssd.shape[-1]
    groups = (ssd_conv_w.shape[2] - inner) // (2 * n_state)
    wts = dict(
        norm_mix=norm_mix, norm_ffn=norm_ffn, norm_final=norm_final,
        sc_w_in=sc_w_in.astype(BF16), sc_conv_w=sc_conv_w, sc_w_out=sc_w_out.astype(BF16),
        ssd_w_in=ssd_w_in.astype(BF16), ssd_conv_w=ssd_conv_w, ssd_conv_b=ssd_conv_b, ssd_dt_bias=ssd_dt_bias,
        ssd_a_log=ssd_a_log, ssd_d=ssd_d, ssd_norm_w=ssd_norm_w, ssd_w_out=ssd_w_out.astype(BF16),
        ffn_w_up=ffn_w_up.astype(BF16), ffn_conv_w=ffn_conv_w, ffn_conv_b=ffn_conv_b,
        ffn_w_down=ffn_w_down.astype(BF16))

    def zeros8(like, n_seq):
        return jnp.zeros((like.shape[0], n_seq, SUBLANES, like.shape[-1]), F32)

    zero_st = jnp.zeros((state_ssd.shape[0], 1, groups, inner // groups, n_state), F32)
    _, m_a, m_sc, m_st, m_f = _trunk(
        meta_tokens.astype(F32), zeros8(state_conv_a, 1), zeros8(state_ssd_conv, 1), zero_st,
        zeros8(state_ffn_conv, 1), wts, n_seq=1, seg=n_meta)
    rep = lambda t: jnp.broadcast_to(t, (t.shape[0], b) + t.shape[2:])
    yp, p_a, p_sc, p_st, p_f = _trunk(
        x_prompt.reshape(b * seq, d), rep(m_a), rep(m_sc), rep(m_st), rep(m_f), wts, n_seq=b, seg=seq)
    ys, s_a, s_sc, s_st, s_f = _trunk(
        x_sample.reshape(sb * sseq, d), jax.vmap(_pad8)(state_conv_a), jax.vmap(_pad8)(state_ssd_conv),
        jax.vmap(lambda s: _state_to_kernel(s, groups))(state_ssd.astype(F32)), jax.vmap(_pad8)(state_ffn_conv),
        wts, n_seq=sb, seg=sseq)

    tail = lambda t, like: t[:, :, SUBLANES - like.shape[2]:, :]
    unst = jax.vmap(lambda s: _state_from_kernel(s, heads))
    return (yp.reshape(b, seq, d), ys.reshape(sb, sseq, d),
            tail(p_a, state_conv_a), tail(p_sc, state_ssd_conv), unst(p_st), tail(p_f, state_ffn_conv),
            tail(s_a, state_conv_a), tail(s_sc, state_ssd_conv), unst(s_st), tail(s_f, state_ffn_conv))
```

```python
import functools

import jax
import jax.numpy as jnp
from jax import lax
from jax.experimental import pallas as pl
from jax.experimental.pallas import tpu as pltpu

EPS = 1e-6
F32 = jnp.float32
BF16 = jnp.bfloat16

SUBLANES = 8
VMEM_LIMIT_BYTES = 58 * 1024 * 1024
ROW_TILE = 512
WIDE_ROW_TILE = 1024
COL_TILE = 512
WIDE_COL_TILE = 1024
SUB_TILE = 256
GROUP_BATCH = 4


def _dot(a, b):
    return jnp.dot(a, b, preferred_element_type=F32)


def _params(*sem):
    return pltpu.CompilerParams(dimension_semantics=sem, vmem_limit_bytes=VMEM_LIMIT_BYTES)


def _rms(x, w):
    return x * lax.rsqrt(jnp.mean(x * x, axis=-1, keepdims=True) + EPS) * w


def _silu(x, scale=None):
    half = 0.5 * x
    gate = 1.0 + jnp.tanh(half)
    return half * gate if scale is None else (half * scale) * gate


def _softplus(x):
    return jnp.maximum(x, 0.0) + jnp.log1p(jnp.exp(-jnp.abs(x)))


def _stage(u_ref, idx, u, st_ref, carry_ref, nb_ref, cols, *, n_sub, seg, tps, chunk=None, st_chunk=None):
    i = pl.program_id(0)
    j = pl.program_id(1) if chunk is None else chunk
    for s in range(n_sub):
        us = u[s * seg:(s + 1) * seg]
        halo = st_ref[s, :, cols] if st_chunk is None else st_ref[s, st_chunk, :, cols]
        if tps > 1:
            halo = jnp.where((i % tps) == 0, halo, carry_ref[j, :, cols])
        base = s * (seg + SUBLANES)
        u_ref[idx + (pl.ds(base, SUBLANES), slice(None))] = halo
        u_ref[idx + (pl.ds(base + SUBLANES, seg), slice(None))] = us
        nb_ref[s, j, :, cols] = us[seg - SUBLANES:seg]
    if tps > 1:
        carry_ref[j, :, cols] = u[u.shape[0] - SUBLANES:]


def _staged_conv(u_ref, idx, w, *, n_sub, seg):
    width = w.shape[0]
    ys = []
    for s in range(n_sub):
        base = s * (seg + SUBLANES) + SUBLANES
        y = None
        for d in range(width):
            term = u_ref[idx + (pl.ds(base - d, seg), slice(None))] * w[width - 1 - d:width - d]
            y = term if y is None else y + term
        ys.append(y)
    return ys[0] if n_sub == 1 else jnp.concatenate(ys, axis=0)


def _clear_carries(carry_refs):
    @pl.when(pl.program_id(0) == 0)
    def _():
        for ref in carry_refs:
            ref[...] = jnp.zeros(ref.shape, F32)


def _sub_chunks(tn):
    step = min(SUB_TILE, tn)
    return [slice(k, k + step) for k in range(0, tn, step)]


def _u_scratch(parts, n_sub, seg, tn):
    step = min(SUB_TILE, tn)
    return [pltpu.VMEM((2, n_sub * (seg + SUBLANES), step), F32) for _ in range(parts * (tn // step))]


def _sc_kernel(x_ref, nw_ref, wb_ref, wc_ref, wv_ref, cw_ref, st_ref, w2_ref, o_ref, nb_ref,
               h_ref, *scratch, n_sub, seg, tps):
    j = pl.program_id(1)
    carry_refs, stage_refs = (scratch[:1], scratch[1:]) if tps > 1 else ((), scratch)
    subs = _sub_chunks(wb_ref.shape[1])
    u_refs, b_refs = stage_refs[:len(subs)], stage_refs[len(subs):]

    @pl.when(j == 0)
    def _():
        x = x_ref[...]
        h_ref[...] = _rms(x, nw_ref[...]).astype(BF16)
        o_ref[...] = x
        _clear_carries(carry_refs)

    h = h_ref[...]
    slot = j % 2
    carry = carry_refs[0] if tps > 1 else None
    g_prev, cs_prev = None, None
    for k, cs in enumerate(subs):
        b_refs[k][slot] = _dot(h, wb_ref[:, cs])
        _stage(u_refs[k], (slot,), _dot(h, wc_ref[:, cs]) * _dot(h, wv_ref[:, cs]), st_ref, carry, nb_ref, cs,
               n_sub=n_sub, seg=seg, tps=tps)
        if g_prev is not None:
            o_ref[...] += _dot(g_prev, w2_ref[cs_prev, :])
        y = _staged_conv(u_refs[k], (slot,), cw_ref[:, cs], n_sub=n_sub, seg=seg)
        g_prev, cs_prev = (b_refs[k][slot] * y).astype(BF16), cs
    o_ref[...] += _dot(g_prev, w2_ref[cs_prev, :])


def _sc_layer(x, st8, norm_w, w_in, conv_w, w_out, layer, *, n_seq, seg):
    rows, d = x.shape
    tm, n_sub, tps = _row_tiling(n_seq, seg)
    tn = _col_tile(d)
    nj = d // tn
    kern = functools.partial(_sc_kernel, n_sub=n_sub, seg=min(seg, tm), tps=tps)
    step = min(SUB_TILE, tn)
    scratch = [pltpu.VMEM((tm, d), BF16)]
    if tps > 1:
        scratch.append(pltpu.VMEM((nj, SUBLANES, tn), F32))
    scratch += _u_scratch(1, n_sub, min(seg, tm), tn)
    scratch += [pltpu.VMEM((2, tm, step), F32) for _ in range(tn // step)]
    out, nb = pl.pallas_call(
        kern,
        grid=(rows // tm, nj),
        in_specs=[
            pl.BlockSpec((tm, d), lambda i, j: (i, 0)),
            pl.BlockSpec((1, d), lambda i, j: (0, 0)),
            pl.BlockSpec((None, d, tn), lambda i, j: (layer, 0, j)),
            pl.BlockSpec((None, d, tn), lambda i, j: (layer, 0, nj + j)),
            pl.BlockSpec((None, d, tn), lambda i, j: (layer, 0, 2 * nj + j)),
            pl.BlockSpec((conv_w.shape[0], tn), lambda i, j: (0, j)),
            pl.BlockSpec((n_sub, SUBLANES, tn), lambda i, j: (i // tps, 0, j)),
            pl.BlockSpec((None, tn, d), lambda i, j: (layer, j, 0)),
        ],
        out_specs=[
            pl.BlockSpec((tm, d), lambda i, j: (i, 0)),
            _nb_spec(n_sub, nj, tn, tps),
        ],
        out_shape=[jax.ShapeDtypeStruct((rows, d), F32), _nb_shape(n_seq, nj, tn)],
        scratch_shapes=scratch,
        compiler_params=_params("arbitrary", "arbitrary"),
        name="short_conv_mixer",
    )(x, norm_w, w_in, w_in, w_in, conv_w, st8, w_out)
    return out, _nb_merge(nb)


def _ffn_kernel(x_ref, nw_ref, wa_ref, wv_ref, cw_ref, cb_ref, st_ref, w2_ref,
                fw_ref, o_ref, nba_ref, nbv_ref, h_ref, *scratch, n_sub, seg, tps, final_norm):
    j = pl.program_id(1)
    nj = pl.num_programs(1)
    carry_refs, stage_refs = (scratch[:2], scratch[2:]) if tps > 1 else ((), scratch)
    ca, cv = carry_refs if tps > 1 else (None, None)
    subs = _sub_chunks(wa_ref.shape[1])
    ua_refs, uv_refs = stage_refs[:len(subs)], stage_refs[len(subs):]
    stage = functools.partial(_stage, n_sub=n_sub, seg=seg, tps=tps)
    conv = functools.partial(_staged_conv, n_sub=n_sub, seg=seg)

    @pl.when(j == 0)
    def _():
        x = x_ref[...]
        h_ref[...] = _rms(x, nw_ref[...]).astype(BF16)
        o_ref[...] = x
        _clear_carries(carry_refs)

    h = h_ref[...]
    slot = j % 2
    g_prev, cs_prev = None, None
    for k, cs in enumerate(subs):
        stage(ua_refs[k], (slot,), _dot(h, wa_ref[:, cs]), st_ref, ca, nba_ref, cs, st_chunk=j)
        stage(uv_refs[k], (slot,), _dot(h, wv_ref[:, cs]), st_ref, cv, nbv_ref, cs, st_chunk=nj + j)
        if g_prev is not None:
            o_ref[...] += _dot(g_prev, w2_ref[cs_prev, :])
        a = conv(ua_refs[k], (slot,), cw_ref[j, :, cs]) + cb_ref[j, :, cs]
        v = conv(uv_refs[k], (slot,), cw_ref[nj + j, :, cs]) + cb_ref[nj + j, :, cs]
        g_prev, cs_prev = _silu(a, v).astype(BF16), cs
    o_ref[...] += _dot(g_prev, w2_ref[cs_prev, :])

    if final_norm:
        @pl.when(j == pl.num_programs(1) - 1)
        def _():
            o_ref[...] = _rms(o_ref[...], fw_ref[...])


def _ffn_layer(x, st8, norm_w, w_up, conv_w, conv_b, w_down, final_w, layer, *, n_seq, seg, final_norm):
    rows, d = x.shape
    dff = w_down.shape[1]
    tm, n_sub, tps, x_spec = _residual_tiling(n_seq, seg, d)
    tn = _col_tile(dff)
    nj = dff // tn
    kern = functools.partial(_ffn_kernel, n_sub=n_sub, seg=min(seg, tm), tps=tps, final_norm=final_norm)
    scratch = [pltpu.VMEM((tm, d), BF16)]
    if tps > 1:
        scratch += [pltpu.VMEM((nj, SUBLANES, tn), F32), pltpu.VMEM((nj, SUBLANES, tn), F32)]
    scratch += _u_scratch(2, n_sub, min(seg, tm), tn)
    width = conv_w.shape[0]
    cw = conv_w.reshape(width, 2 * nj, tn).transpose(1, 0, 2)
    cb = conv_b.reshape(2 * nj, 1, tn)
    st = st8.reshape(n_seq, SUBLANES, 2 * nj, tn).transpose(0, 2, 1, 3)
    out, nba, nbv = pl.pallas_call(
        kern,
        grid=(rows // tm, nj),
        in_specs=[
            x_spec,
            pl.BlockSpec((1, d), lambda i, j: (0, 0)),
            pl.BlockSpec((None, d, tn), lambda i, j: (layer, 0, j)),
            pl.BlockSpec((None, d, tn), lambda i, j: (layer, 0, nj + j)),
            pl.BlockSpec((2 * nj, width, tn), lambda i, j: (0, 0, 0)),
            pl.BlockSpec((2 * nj, 1, tn), lambda i, j: (0, 0, 0)),
            pl.BlockSpec((n_sub, 2 * nj, SUBLANES, tn), lambda i, j: (i // tps, 0, 0, 0)),
            pl.BlockSpec((None, tn, d), lambda i, j: (layer, j, 0)),
            pl.BlockSpec((1, d), lambda i, j: (0, 0)),
        ],
        out_specs=[
            pl.BlockSpec((tm, d), lambda i, j: (i, 0)),
            _nb_spec(n_sub, nj, tn, tps),
            _nb_spec(n_sub, nj, tn, tps),
        ],
        out_shape=[jax.ShapeDtypeStruct((rows, d), F32), _nb_shape(n_seq, nj, tn), _nb_shape(n_seq, nj, tn)],
        scratch_shapes=scratch,
        compiler_params=_params("arbitrary", "arbitrary"),
        name="conv_ffn",
    )(x, norm_w, w_up, w_up, cw, cb, st, w_down, final_w)
    return out, jnp.concatenate([_nb_merge(nba), _nb_merge(nbv)], axis=-1)


def _ssd_in_kernel(x_ref, nw_ref, w_ref, cw_ref, cb_ref, st_ref, wdt_ref, dtb_ref, z_ref, o_ref, nb_ref, dt_ref,
                   h_ref, *scratch, n_sub, seg, tps, njx):
    s = pl.program_id(1)
    carry_refs, bufs = (scratch[:1], scratch[1:]) if tps > 1 else ((), scratch)
    carry = carry_refs[0] if tps > 1 else None
    tn = w_ref.shape[1]
    wide = [slice(k, k + tn // len(bufs)) for k in range(0, tn, tn // len(bufs))]

    @pl.when(s == 0)
    def _():
        h_ref[...] = _rms(x_ref[...], nw_ref[...]).astype(BF16)
        _clear_carries(carry_refs)

    @pl.when(s < njx)
    def _():
        h = h_ref[...]
        for k, cs in enumerate(wide):
            _stage(bufs[k], (), _dot(h, w_ref[:, cs]), st_ref, carry, nb_ref, cs,
                   n_sub=n_sub, seg=seg, tps=tps, chunk=s)
            y = _staged_conv(bufs[k], (), cw_ref[:, cs], n_sub=n_sub, seg=seg)
            o_ref[:, cs] = _silu(y + cb_ref[:, cs])

    @pl.when(s >= njx)
    def _():
        h = h_ref[...]
        for cs in _sub_chunks(tn):
            z_ref[:, cs] = _silu(_dot(h, w_ref[:, cs]))

    @pl.when(s == pl.num_programs(1) - 1)
    def _():
        dt_ref[...] = _softplus(_dot(h_ref[...], wdt_ref[...]) + dtb_ref[...])


def _ssd_in(x, st8, norm_w, w_in, conv_w, conv_b, dt_bias, layer, *, n_seq, seg, inner, heads):
    rows, d = x.shape
    width, conv_dim = conv_w.shape
    cols = inner + conv_dim
    tm, n_sub, tps = _row_tiling(n_seq, seg, WIDE_ROW_TILE if seg >= WIDE_ROW_TILE else ROW_TILE)
    tn = _col_tile(inner, WIDE_COL_TILE)
    assert conv_dim % tn == 0
    nz = inner // tn
    njx = conv_dim // tn
    w_blk = lambda s: jnp.where(s < njx, nz + s, s - njx)
    x_blk = lambda s: jnp.minimum(s, njx - 1)
    z_blk = lambda s: jnp.maximum(s - njx, 0)
    w_dt = w_in[layer, :, cols:]
    seg_t = min(seg, tm)
    step = min(2 * SUB_TILE, tn)
    kern = functools.partial(_ssd_in_kernel, n_sub=n_sub, seg=seg_t, tps=tps, njx=njx)
    scratch = [pltpu.VMEM((tm, d), BF16)]
    if tps > 1:
        scratch.append(pltpu.VMEM((njx, SUBLANES, tn), F32))
    scratch += [pltpu.VMEM((n_sub * (seg_t + SUBLANES), step), F32) for _ in range(tn // step)]
    zs, xbc, nb, dt = pl.pallas_call(
        kern,
        grid=(rows // tm, nz + njx),
        in_specs=[
            pl.BlockSpec((tm, d), lambda i, s: (i, 0)),
            pl.BlockSpec((1, d), lambda i, s: (0, 0)),
            pl.BlockSpec((None, d, tn), lambda i, s: (layer, 0, w_blk(s))),
            pl.BlockSpec((width, tn), lambda i, s: (0, x_blk(s))),
            pl.BlockSpec((1, tn), lambda i, s: (0, x_blk(s))),
            pl.BlockSpec((n_sub, SUBLANES, tn), lambda i, s: (i // tps, 0, x_blk(s))),
            pl.BlockSpec((d, heads), lambda i, s: (0, 0)),
            pl.BlockSpec((1, heads), lambda i, s: (0, 0)),
        ],
        out_specs=[
            pl.BlockSpec((tm, tn), lambda i, s: (i, z_blk(s))),
            pl.BlockSpec((tm, tn), lambda i, s: (i, x_blk(s))),
            _nb_spec(n_sub, njx, tn, tps),
            pl.BlockSpec((tm, heads), lambda i, s: (i, 0)),
        ],
        out_shape=[jax.ShapeDtypeStruct((rows, inner), F32), jax.ShapeDtypeStruct((rows, conv_dim), F32),
                   _nb_shape(n_seq, njx, tn), jax.ShapeDtypeStruct((rows, heads), F32)],
        scratch_shapes=scratch,
        compiler_params=_params("arbitrary", "arbitrary"),
        name="ssd_in_proj",
    )(x, norm_w, w_in, conv_w, conv_b, st8, w_dt, dt_bias)
    return zs, xbc, dt, _nb_merge(nb)


def _split3(x):
    hi = x.astype(BF16).astype(F32)
    r = x - hi
    mid = r.astype(BF16).astype(F32)
    lo = (r - mid).astype(BF16).astype(F32)
    return hi, mid, lo


def _dot_f32_lhs(x, sel):
    hi, mid, lo = _split3(x)
    return _dot(hi.astype(BF16), sel) + _dot(mid.astype(BF16), sel) + _dot(lo.astype(BF16), sel)


def _dot_f32_rhs(sel, x):
    hi, mid, lo = _split3(x)
    return _dot(sel, hi.astype(BF16)) + _dot(sel, mid.astype(BF16)) + _dot(sel, lo.astype(BF16))


def _scan_kernel(zs_ref, xs_ref, b_ref, c_ref, dt3_ref, dtt_ref, a3_ref, at_ref, dsk_ref, nw_ref, st0_ref,
                 rep3_ref, tril_ref, tri2_ref, mask2_ref, bd_ref, y_ref, stout_ref, st_ref,
                 *, groups, hpg, q, n):
    c = pl.program_id(1)
    gw = hpg * q
    heads = groups * hpg

    @pl.when(c == 0)
    def _():
        for g in range(groups):
            st_ref[g] = st0_ref[0, g].T

    dt3 = dt3_ref[...]
    cs3 = _dot_f32_rhs(tril_ref[...], dt3 * a3_ref[...])
    tail3 = jnp.exp(cs3[q - 1:q] - cs3) * dt3
    hi, mid, lo = _split3(jnp.concatenate([cs3, tail3], axis=0))
    lane = lax.broadcasted_iota(jnp.int32, hi.shape, 1)
    pieces = jnp.where(lane < heads, hi, jnp.where(lane < 2 * heads, mid, lo)).astype(BF16)
    gsl = [slice(g * gw, (g + 1) * gw) for g in range(groups)]
    bcs = [_dot(pieces, rep3_ref[:, gsl[g]]) for g in range(groups)]
    cs_b = [bc[0:q] for bc in bcs]
    tail_b = [bc[q:2 * q] for bc in bcs]
    ecs_b = [jnp.exp(cs) for cs in cs_b]
    dtt = dtt_ref[0]
    cst = _dot_f32_lhs(dtt * at_ref[...], tri2_ref[...])
    mask2 = mask2_ref[...] > 0.0
    bd = bd_ref[...]

    for g0 in range(0, groups, GROUP_BATCH):
        gs = range(g0, min(g0 + GROUP_BATCH, groups))
        cgs = {g: c_ref[:, g * n:(g + 1) * n].astype(BF16) for g in gs}
        bgs = {g: b_ref[:, g * n:(g + 1) * n].astype(BF16) for g in gs}
        cb2s = {g: lax.dot_general(cgs[g], jnp.concatenate([bgs[g], bgs[g]], axis=0), (((1,), (1,)), ((), ())),
                                   preferred_element_type=F32) for g in gs}
        y_inter = {g: _dot(cgs[g], st_ref[g].astype(BF16)) for g in gs}
        ds = {g: lax.dot_general(bgs[g], (xs_ref[:, gsl[g]] * tail_b[g]).astype(BF16), (((0,), (0,)), ((), ())),
                                 preferred_element_type=F32) for g in gs}
        ms, xbds = [], []
        for g in gs:
            for k in range(hpg // 2):
                psl = slice(g * gw + 2 * k * q, g * gw + 2 * (k + 1) * q)
                pair = g * (hpg // 2) + k
                seg = cs_b[g][:, 2 * k * q:2 * (k + 1) * q] - jnp.broadcast_to(cst[pair:pair + 1], (q, 2 * q))
                w = jnp.where(mask2, jnp.exp(seg), 0.0) * jnp.broadcast_to(dtt[pair:pair + 1], (q, 2 * q))
                ms.append((cb2s[g] * w).astype(BF16))
                xk = xs_ref[:, psl].astype(BF16)
                xbds.append(jnp.concatenate([xk, xk], axis=0) * bd)
        parts = [_dot(m, xbd) for m, xbd in zip(ms, xbds)]
        for i, g in enumerate(gs):
            st_ref[g] = st_ref[g] * ecs_b[g][q - 1:q] + ds[g]
            pg = parts[i * (hpg // 2):(i + 1) * (hpg // 2)]
            y = pg[0] if len(pg) == 1 else jnp.concatenate(pg, axis=1)
            y = y + y_inter[g] * ecs_b[g]
            y = (y + dsk_ref[:, gsl[g]] * xs_ref[:, gsl[g]]) * zs_ref[:, gsl[g]]
            y = y * lax.rsqrt(jnp.mean(y * y, axis=-1, keepdims=True) + EPS)
            y_ref[:, gsl[g]] = (y * nw_ref[:, gsl[g]]).astype(BF16)

    @pl.when(c == pl.num_programs(1) - 1)
    def _():
        for g in range(groups):
            stout_ref[0, g] = st_ref[g].T


def _ssd_scan(zs, xbc, dt, st_all, layer, a_log, d_skip, norm_w, *, n_seq, seg, inner, groups, n, heads):
    rows = zs.shape[0]
    p = inner // heads
    q = p
    hpg = heads // groups
    assert seg % q == 0 and hpg % 2 == 0 and inner % (groups * n) == 0
    nc = seg // q
    gn = groups * n
    a = -jnp.exp(a_log.astype(F32))
    a3 = jnp.tile(a.reshape(1, heads), (1, 3))
    a_t = jnp.repeat(a.reshape(heads // 2, 2), q, axis=1)
    dt3 = jnp.tile(dt, (1, 3))
    dtt = dt.reshape(rows // q, q, heads // 2, 2).transpose(0, 2, 3, 1).reshape(rows // q, heads // 2, 2 * q)
    dsk = jnp.repeat(d_skip.astype(F32), p).reshape(1, inner)
    rep3 = jnp.tile(jnp.repeat(jnp.eye(heads, dtype=BF16), p, axis=1), (3, 1))
    tril = jnp.tril(jnp.ones((q, q), BF16))
    triu = jnp.triu(jnp.ones((q, q), F32))
    zero = jnp.zeros((q, q), F32)
    tri2 = jnp.block([[triu, zero], [zero, triu]]).astype(BF16)
    mask2 = jnp.concatenate([jnp.tril(jnp.ones((q, q), F32))] * 2, axis=1)
    one = jnp.ones((q, q), F32)
    bd = jnp.block([[one, zero], [zero, one]]).astype(BF16)

    kern = functools.partial(_scan_kernel, groups=groups, hpg=hpg, q=q, n=n)
    const = lambda s, c: (0, 0)
    operands = [zs, xbc, xbc, xbc, dt3, dtt, a3, a_t, dsk, norm_w, st_all, rep3, tril, tri2, mask2, bd]
    y, st = pl.pallas_call(
        kern,
        grid=(n_seq, nc),
        in_specs=[
            pl.BlockSpec((q, inner), lambda s, c: (s * nc + c, 0)),
            pl.BlockSpec((q, inner), lambda s, c: (s * nc + c, 0)),
            pl.BlockSpec((q, gn), lambda s, c: (s * nc + c, inner // gn)),
            pl.BlockSpec((q, gn), lambda s, c: (s * nc + c, inner // gn + 1)),
            pl.BlockSpec((q, 3 * heads), lambda s, c: (s * nc + c, 0)),
            pl.BlockSpec((1, heads // 2, 2 * q), lambda s, c: (s * nc + c, 0, 0)),
            pl.BlockSpec((1, 3 * heads), const),
            pl.BlockSpec((heads // 2, 2 * q), const),
            pl.BlockSpec((1, inner), const),
            pl.BlockSpec((1, inner), const),
            pl.BlockSpec((None, 1, groups, hpg * p, n), lambda s, c: (layer, s, 0, 0, 0)),
            pl.BlockSpec(rep3.shape, const),
            pl.BlockSpec(tril.shape, const),
            pl.BlockSpec(tri2.shape, const),
            pl.BlockSpec(mask2.shape, const),
            pl.BlockSpec(bd.shape, const),
        ],
        out_specs=[
            pl.BlockSpec((q, inner), lambda s, c: (s * nc + c, 0)),
            pl.BlockSpec((None, 1, groups, hpg * p, n), lambda s, c: (layer, s, 0, 0, 0)),
        ],
        out_shape=[jax.ShapeDtypeStruct((rows, inner), BF16),
                   jax.ShapeDtypeStruct(st_all.shape, F32)],
        scratch_shapes=[pltpu.VMEM((groups, n, hpg * p), F32)],
        input_output_aliases={next(i for i, o in enumerate(operands) if o is st_all): 1},
        compiler_params=_params("arbitrary", "arbitrary"),
        name="ssd_scan",
    )(*operands)
    return y, st


def _out_proj_kernel(x_ref, g_ref, w_ref, o_ref):
    o_ref[...] = x_ref[...] + _dot(g_ref[...], w_ref[...])


def _out_proj(x, g, w, layer, *, n_seq, seg):
    rows, d = x.shape
    k = g.shape[1]
    tm, _, _ = _row_tiling(n_seq, seg)
    return pl.pallas_call(
        _out_proj_kernel,
        grid=(rows // tm,),
        in_specs=[
            pl.BlockSpec((tm, d), lambda i: (i, 0)),
            pl.BlockSpec((tm, k), lambda i: (i, 0)),
            pl.BlockSpec((None, k, d), lambda i: (layer, 0, 0), pipeline_mode=pl.Buffered(1)),
        ],
        out_specs=pl.BlockSpec((tm, d), lambda i: (i, 0)),
        out_shape=jax.ShapeDtypeStruct((rows, d), F32),
        compiler_params=_params("arbitrary"),
        name="ssd_out_proj",
    )(x, g, w)


def _row_tiling(n_seq, seg, row_tile=ROW_TILE):
    if seg >= row_tile:
        assert seg % row_tile == 0
        return row_tile, 1, seg // row_tile
    n_sub = max(1, min(n_seq, row_tile // seg))
    while n_seq % n_sub:
        n_sub -= 1
    return n_sub * seg, n_sub, 1


def _residual_tiling(n_seq, seg, d):
    wide = seg >= WIDE_ROW_TILE
    tm, n_sub, tps = _row_tiling(n_seq, seg, WIDE_ROW_TILE if wide else ROW_TILE)
    mode = dict(pipeline_mode=pl.Buffered(1)) if wide else {}
    return tm, n_sub, tps, pl.BlockSpec((tm, d), lambda i, j: (i, 0), **mode)


def _col_tile(width, pref=COL_TILE):
    tn = pref
    while width % tn:
        tn //= 2
    assert tn >= 128
    return tn


def _nb_spec(n_sub, nj, tn, tps):
    return pl.BlockSpec((n_sub, nj, SUBLANES, tn), lambda i, j: (i // tps, 0, 0, 0))


def _nb_shape(n_seq, nj, tn):
    return jax.ShapeDtypeStruct((n_seq, nj, SUBLANES, tn), F32)


def _nb_merge(nb):
    n_seq, nj, rows, tn = nb.shape
    return nb.transpose(0, 2, 1, 3).reshape(n_seq, rows, nj * tn)


def _pad8(buf):
    return jnp.pad(buf, ((0, 0), (SUBLANES - buf.shape[1], 0), (0, 0)))


def _state_to_kernel(s, groups):
    b, h, p, n = s.shape
    return s.reshape(b, groups, (h // groups) * p, n)


def _state_from_kernel(s, heads):
    b, g, w, n = s.shape
    return s.reshape(b, heads, (g * w) // heads, n)


def _trunk(x, conv_a8, ssd_conv8, ssd_st, ffn8, wts, *, n_seq, seg):
    depth = wts["norm_mix"].shape[0]
    new_a, new_sc, new_f = [], [], []
    for i in range(depth):
        nm = wts["norm_mix"][i][None]
        j = i // 2
        if i % 2 == 0:
            x, nb = _sc_layer(x, conv_a8[j], nm, wts["sc_w_in"], wts["sc_conv_w"][j], wts["sc_w_out"], j,
                              n_seq=n_seq, seg=seg)
            new_a.append(nb)
        else:
            heads = wts["ssd_a_log"].shape[1]
            inner = wts["ssd_w_out"].shape[1]
            groups, n = ssd_st.shape[2], ssd_st.shape[4]
            zs, xbc, dt, nb = _ssd_in(x, ssd_conv8[j], nm, wts["ssd_w_in"], wts["ssd_conv_w"][j],
                                      wts["ssd_conv_b"][j][None], wts["ssd_dt_bias"][j][None], j,
                                      n_seq=n_seq, seg=seg, inner=inner, heads=heads)
            q = inner // heads
            pad = (-seg) % q
            if pad:
                padr = lambda t: jnp.pad(t.reshape(n_seq, seg, -1), ((0, 0), (0, pad), (0, 0))).reshape(
                    n_seq * (seg + pad), -1)
                zs, xbc, dt = padr(zs), padr(xbc), padr(dt)
            g, ssd_st = _ssd_scan(zs, xbc, dt, ssd_st, j, wts["ssd_a_log"][j], wts["ssd_d"][j],
                                  wts["ssd_norm_w"][j][None], n_seq=n_seq, seg=seg + pad, inner=inner,
                                  groups=groups, n=n, heads=heads)
            if pad:
                g = g.reshape(n_seq, seg + pad, -1)[:, :seg].reshape(n_seq * seg, -1)
            x = _out_proj(x, g, wts["ssd_w_out"], j, n_seq=n_seq, seg=seg)
            new_sc.append(nb)
        x, nf = _ffn_layer(x, ffn8[i], wts["norm_ffn"][i][None], wts["ffn_w_up"], wts["ffn_conv_w"][i],
                           wts["ffn_conv_b"][i][None], wts["ffn_w_down"], wts["norm_final"][None], i,
                           n_seq=n_seq, seg=seg, final_norm=(i == depth - 1))
        new_f.append(nf)
    return x, jnp.stack(new_a), jnp.stack(new_sc), ssd_st, jnp.stack(new_f)


def kernel(x_prompt, x_sample, state_conv_a, state_ssd_conv, state_ssd, state_ffn_conv, meta_tokens, norm_mix,
           norm_ffn, norm_final, sc_w_in, sc_conv_w, sc_w_out, ssd_w_in, ssd_conv_w, ssd_conv_b, ssd_dt_bias,
           ssd_a_log, ssd_d, ssd_norm_w, ssd_w_out, ffn_w_up, ffn_conv_w, ffn_conv_b, ffn_w_down):
    b, seq, d = x_prompt.shape
    sb, sseq, _ = x_sample.shape
    n_meta = meta_tokens.shape[0]
    heads = ssd_a_log.shape[1]
    inner = ssd_w_out.shape[1]
    n_state = state_ssd.shape[-1]
    groups = (ssd_conv_w.shape[2] - inner) // (2 * n_state)
    wts = dict(
        norm_mix=norm_mix, norm_ffn=norm_ffn, norm_final=norm_final,
        sc_w_in=sc_w_in.astype(BF16), sc_conv_w=sc_conv_w, sc_w_out=sc_w_out.astype(BF16),
        ssd_w_in=ssd_w_in.astype(BF16), ssd_conv_w=ssd_conv_w, ssd_conv_b=ssd_conv_b, ssd_dt_bias=ssd_dt_bias,
        ssd_a_log=ssd_a_log, ssd_d=ssd_d, ssd_norm_w=ssd_norm_w, ssd_w_out=ssd_w_out.astype(BF16),
        ffn_w_up=ffn_w_up.astype(BF16), ffn_conv_w=ffn_conv_w, ffn_conv_b=ffn_conv_b,
        ffn_w_down=ffn_w_down.astype(BF16))

    def zeros8(like, n_seq):
        return jnp.zeros((like.shape[0], n_seq, SUBLANES, like.shape[-1]), F32)

    zero_st = jnp.zeros((state_ssd.shape[0], 1, groups, inner // groups, n_state), F32)
    _, m_a, m_sc, m_st, m_f = _trunk(
        meta_tokens.astype(F32), zeros8(state_conv_a, 1), zeros8(state_ssd_conv, 1), zero_st,
        zeros8(state_ffn_conv, 1), wts, n_seq=1, seg=n_meta)
    rep = lambda t: jnp.broadcast_to(t, (t.shape[0], b) + t.shape[2:])
    yp, p_a, p_sc, p_st, p_f = _trunk(
        x_prompt.reshape(b * seq, d), rep(m_a), rep(m_sc), rep(m_st), rep(m_f), wts, n_seq=b, seg=seq)
    ys, s_a, s_sc, s_st, s_f = _trunk(
        x_sample.reshape(sb * sseq, d), jax.vmap(_pad8)(state_conv_a), jax.vmap(_pad8)(state_ssd_conv),
        jax.vmap(lambda s: _state_to_kernel(s, groups))(state_ssd.astype(F32)), jax.vmap(_pad8)(state_ffn_conv),
        wts, n_seq=sb, seg=sseq)

    tail = lambda t, like: t[:, :, SUBLANES - like.shape[2]:, :]
    unst = jax.vmap(lambda s: _state_from_kernel(s, heads))
    return (yp.reshape(b, seq, d), ys.reshape(sb, sseq, d),
            tail(p_a, state_conv_a), tail(p_sc, state_ssd_conv), unst(p_st), tail(p_f, state_ffn_conv),
            tail(s_a, state_conv_a), tail(s_sc, state_ssd_conv), unst(s_st), tail(s_f, state_ffn_conv))
```

```python
import functools

import jax
import jax.numpy as jnp
from jax import lax
from jax.experimental import pallas as pl
from jax.experimental.pallas import tpu as pltpu

EPS = 1e-6
F32 = jnp.float32
BF16 = jnp.bfloat16

SUBLANES = 8
VMEM_LIMIT_BYTES = 58 * 1024 * 1024
ROW_TILE = 512
WIDE_ROW_TILE = 1024
COL_TILE = 512
WIDE_COL_TILE = 1024
SUB_TILE = 256
GROUP_BATCH = 4


def _dot(a, b):
    return jnp.dot(a, b, preferred_element_type=F32)


def _params(*sem):
    return pltpu.CompilerParams(dimension_semantics=sem, vmem_limit_bytes=VMEM_LIMIT_BYTES)


def _rms(x, w):
    return x * lax.rsqrt(jnp.mean(x * x, axis=-1, keepdims=True) + EPS) * w


def _silu(x, scale=None):
    half = 0.5 * x
    gate = 1.0 + jnp.tanh(half)
    return half * gate if scale is None else (half * scale) * gate


def _softplus(x):
    return jnp.maximum(x, 0.0) + jnp.log1p(jnp.exp(-jnp.abs(x)))


def _stage(u_ref, idx, u, st_ref, carry_ref, nb_ref, cols, *, n_sub, seg, tps, chunk=None):
    i = pl.program_id(0)
    j = pl.program_id(1) if chunk is None else chunk
    for s in range(n_sub):
        us = u[s * seg:(s + 1) * seg]
        halo = st_ref[s, :, cols]
        if tps > 1:
            halo = jnp.where((i % tps) == 0, halo, carry_ref[j, :, cols])
        base = s * (seg + SUBLANES)
        u_ref[idx + (pl.ds(base, SUBLANES), slice(None))] = halo
        u_ref[idx + (pl.ds(base + SUBLANES, seg), slice(None))] = us
        nb_ref[s, j, :, cols] = us[seg - SUBLANES:seg]
    if tps > 1:
        carry_ref[j, :, cols] = u[u.shape[0] - SUBLANES:]


def _staged_conv(u_ref, idx, w, *, n_sub, seg):
    width = w.shape[0]
    ys = []
    for s in range(n_sub):
        base = s * (seg + SUBLANES) + SUBLANES
        y = None
        for d in range(width):
            term = u_ref[idx + (pl.ds(base - d, seg), slice(None))] * w[width - 1 - d:width - d]
            y = term if y is None else y + term
        ys.append(y)
    return ys[0] if n_sub == 1 else jnp.concatenate(ys, axis=0)


def _clear_carries(carry_refs):
    @pl.when(pl.program_id(0) == 0)
    def _():
        for ref in carry_refs:
            ref[...] = jnp.zeros(ref.shape, F32)


def _sub_chunks(tn):
    step = min(SUB_TILE, tn)
    return [slice(k, k + step) for k in range(0, tn, step)]


def _u_scratch(parts, n_sub, seg, tn):
    step = min(SUB_TILE, tn)
    return [pltpu.VMEM((2, n_sub * (seg + SUBLANES), step), F32) for _ in range(parts * (tn // step))]


def _sc_kernel(x_ref, nw_ref, wb_ref, wc_ref, wv_ref, cw_ref, st_ref, w2_ref, o_ref, nb_ref,
               h_ref, *scratch, n_sub, seg, tps):
    j = pl.program_id(1)
    carry_refs, stage_refs = (scratch[:1], scratch[1:]) if tps > 1 else ((), scratch)
    subs = _sub_chunks(wb_ref.shape[1])
    u_refs, b_refs = stage_refs[:len(subs)], stage_refs[len(subs):]

    @pl.when(j == 0)
    def _():
        x = x_ref[...]
        h_ref[...] = _rms(x, nw_ref[...]).astype(BF16)
        o_ref[...] = x
        _clear_carries(carry_refs)

    h = h_ref[...]
    slot = j % 2
    carry = carry_refs[0] if tps > 1 else None
    g_prev, cs_prev = None, None
    for k, cs in enumerate(subs):
        b_refs[k][slot] = _dot(h, wb_ref[:, cs])
        _stage(u_refs[k], (slot,), _dot(h, wc_ref[:, cs]) * _dot(h, wv_ref[:, cs]), st_ref, carry, nb_ref, cs,
               n_sub=n_sub, seg=seg, tps=tps)
        if g_prev is not None:
            o_ref[...] += _dot(g_prev, w2_ref[cs_prev, :])
        y = _staged_conv(u_refs[k], (slot,), cw_ref[:, cs], n_sub=n_sub, seg=seg)
        g_prev, cs_prev = (b_refs[k][slot] * y).astype(BF16), cs
    o_ref[...] += _dot(g_prev, w2_ref[cs_prev, :])


def _sc_layer(x, st8, norm_w, w_in, conv_w, w_out, layer, *, n_seq, seg):
    rows, d = x.shape
    tm, n_sub, tps = _row_tiling(n_seq, seg)
    tn = _col_tile(d)
    nj = d // tn
    kern = functools.partial(_sc_kernel, n_sub=n_sub, seg=min(seg, tm), tps=tps)
    step = min(SUB_TILE, tn)
    scratch = [pltpu.VMEM((tm, d), BF16)]
    if tps > 1:
        scratch.append(pltpu.VMEM((nj, SUBLANES, tn), F32))
    scratch += _u_scratch(1, n_sub, min(seg, tm), tn)
    scratch += [pltpu.VMEM((2, tm, step), F32) for _ in range(tn // step)]
    out, nb = pl.pallas_call(
        kern,
        grid=(rows // tm, nj),
        in_specs=[
            pl.BlockSpec((tm, d), lambda i, j: (i, 0)),
            pl.BlockSpec((1, d), lambda i, j: (0, 0)),
            pl.BlockSpec((None, d, tn), lambda i, j: (layer, 0, j)),
            pl.BlockSpec((None, d, tn), lambda i, j: (layer, 0, nj + j)),
            pl.BlockSpec((None, d, tn), lambda i, j: (layer, 0, 2 * nj + j)),
            pl.BlockSpec((conv_w.shape[0], tn), lambda i, j: (0, j)),
            pl.BlockSpec((n_sub, SUBLANES, tn), lambda i, j: (i // tps, 0, j)),
            pl.BlockSpec((None, tn, d), lambda i, j: (layer, j, 0)),
        ],
        out_specs=[
            pl.BlockSpec((tm, d), lambda i, j: (i, 0)),
            _nb_spec(n_sub, nj, tn, tps),
        ],
        out_shape=[jax.ShapeDtypeStruct((rows, d), F32), _nb_shape(n_seq, nj, tn)],
        scratch_shapes=scratch,
        compiler_params=_params("arbitrary", "arbitrary"),
        name="short_conv_mixer",
    )(x, norm_w, w_in, w_in, w_in, conv_w, st8, w_out)
    return out, _nb_merge(nb)


def _ffn_kernel(x_ref, nw_ref, wa_ref, wv_ref, cwa_ref, cwv_ref, ba_ref, bv_ref, sta_ref, stv_ref, w2_ref,
                fw_ref, o_ref, nba_ref, nbv_ref, h_ref, *scratch, n_sub, seg, tps, final_norm):
    j = pl.program_id(1)
    carry_refs, stage_refs = (scratch[:2], scratch[2:]) if tps > 1 else ((), scratch)
    ca, cv = carry_refs if tps > 1 else (None, None)
    subs = _sub_chunks(wa_ref.shape[1])
    ua_refs, uv_refs = stage_refs[:len(subs)], stage_refs[len(subs):]
    stage = functools.partial(_stage, n_sub=n_sub, seg=seg, tps=tps)
    conv = functools.partial(_staged_conv, n_sub=n_sub, seg=seg)

    @pl.when(j == 0)
    def _():
        x = x_ref[...]
        h_ref[...] = _rms(x, nw_ref[...]).astype(BF16)
        o_ref[...] = x
        _clear_carries(carry_refs)

    h = h_ref[...]
    slot = j % 2
    g_prev, cs_prev = None, None
    for k, cs in enumerate(subs):
        stage(ua_refs[k], (slot,), _dot(h, wa_ref[:, cs]), sta_ref, ca, nba_ref, cs)
        stage(uv_refs[k], (slot,), _dot(h, wv_ref[:, cs]), stv_ref, cv, nbv_ref, cs)
        if g_prev is not None:
            o_ref[...] += _dot(g_prev, w2_ref[cs_prev, :])
        a = conv(ua_refs[k], (slot,), cwa_ref[:, cs]) + ba_ref[:, cs]
        v = conv(uv_refs[k], (slot,), cwv_ref[:, cs]) + bv_ref[:, cs]
        g_prev, cs_prev = _silu(a, v).astype(BF16), cs
    o_ref[...] += _dot(g_prev, w2_ref[cs_prev, :])

    if final_norm:
        @pl.when(j == pl.num_programs(1) - 1)
        def _():
            o_ref[...] = _rms(o_ref[...], fw_ref[...])


def _ffn_layer(x, st8, norm_w, w_up, conv_w, conv_b, w_down, final_w, layer, *, n_seq, seg, final_norm):
    rows, d = x.shape
    dff = w_down.shape[1]
    tm, n_sub, tps, x_spec = _residual_tiling(n_seq, seg, d)
    tn = _col_tile(dff)
    nj = dff // tn
    kern = functools.partial(_ffn_kernel, n_sub=n_sub, seg=min(seg, tm), tps=tps, final_norm=final_norm)
    scratch = [pltpu.VMEM((tm, d), BF16)]
    if tps > 1:
        scratch += [pltpu.VMEM((nj, SUBLANES, tn), F32), pltpu.VMEM((nj, SUBLANES, tn), F32)]
    scratch += _u_scratch(2, n_sub, min(seg, tm), tn)
    width = conv_w.shape[0]
    out, nba, nbv = pl.pallas_call(
        kern,
        grid=(rows // tm, nj),
        in_specs=[
            x_spec,
            pl.BlockSpec((1, d), lambda i, j: (0, 0)),
            pl.BlockSpec((None, d, tn), lambda i, j: (layer, 0, j)),
            pl.BlockSpec((None, d, tn), lambda i, j: (layer, 0, nj + j)),
            pl.BlockSpec((width, tn), lambda i, j: (0, j)),
            pl.BlockSpec((width, tn), lambda i, j: (0, nj + j)),
            pl.BlockSpec((1, tn), lambda i, j: (0, j)),
            pl.BlockSpec((1, tn), lambda i, j: (0, nj + j)),
            pl.BlockSpec((n_sub, SUBLANES, tn), lambda i, j: (i // tps, 0, j)),
            pl.BlockSpec((n_sub, SUBLANES, tn), lambda i, j: (i // tps, 0, nj + j)),
            pl.BlockSpec((None, tn, d), lambda i, j: (layer, j, 0)),
            pl.BlockSpec((1, d), lambda i, j: (0, 0)),
        ],
        out_specs=[
            pl.BlockSpec((tm, d), lambda i, j: (i, 0)),
            _nb_spec(n_sub, nj, tn, tps),
            _nb_spec(n_sub, nj, tn, tps),
        ],
        out_shape=[jax.ShapeDtypeStruct((rows, d), F32), _nb_shape(n_seq, nj, tn), _nb_shape(n_seq, nj, tn)],
        scratch_shapes=scratch,
        compiler_params=_params("arbitrary", "arbitrary"),
        name="conv_ffn",
    )(x, norm_w, w_up, w_up, conv_w, conv_w, conv_b, conv_b, st8, st8, w_down, final_w)
    return out, jnp.concatenate([_nb_merge(nba), _nb_merge(nbv)], axis=-1)


def _ssd_in_kernel(x_ref, nw_ref, w_ref, cw_ref, cb_ref, st_ref, wdt_ref, dtb_ref, z_ref, o_ref, nb_ref, dt_ref,
                   h_ref, *scratch, n_sub, seg, tps, njx):
    s = pl.program_id(1)
    carry_refs, bufs = (scratch[:1], scratch[1:]) if tps > 1 else ((), scratch)
    carry = carry_refs[0] if tps > 1 else None
    tn = w_ref.shape[1]
    wide = [slice(k, k + tn // len(bufs)) for k in range(0, tn, tn // len(bufs))]

    @pl.when(s == 0)
    def _():
        h_ref[...] = _rms(x_ref[...], nw_ref[...]).astype(BF16)
        _clear_carries(carry_refs)

    @pl.when(s < njx)
    def _():
        h = h_ref[...]
        for k, cs in enumerate(wide):
            _stage(bufs[k], (), _dot(h, w_ref[:, cs]), st_ref, carry, nb_ref, cs,
                   n_sub=n_sub, seg=seg, tps=tps, chunk=s)
            y = _staged_conv(bufs[k], (), cw_ref[:, cs], n_sub=n_sub, seg=seg)
            o_ref[:, cs] = _silu(y + cb_ref[:, cs]).astype(o_ref.dtype)

    @pl.when(s >= njx)
    def _():
        h = h_ref[...]
        for cs in _sub_chunks(tn):
            z_ref[:, cs] = _silu(_dot(h, w_ref[:, cs])).astype(z_ref.dtype)

    @pl.when(s == pl.num_programs(1) - 1)
    def _():
        dt_ref[...] = _softplus(_dot(h_ref[...], wdt_ref[...]) + dtb_ref[...])


def _ssd_in(x, st8, norm_w, w_in, conv_w, conv_b, dt_bias, layer, *, n_seq, seg, inner, heads):
    rows, d = x.shape
    width, conv_dim = conv_w.shape
    cols = inner + conv_dim
    tm, n_sub, tps = _row_tiling(n_seq, seg, WIDE_ROW_TILE if seg >= WIDE_ROW_TILE else ROW_TILE)
    tn = _col_tile(inner, WIDE_COL_TILE)
    assert conv_dim % tn == 0
    nz = inner // tn
    njx = conv_dim // tn
    w_blk = lambda s: jnp.where(s < njx, nz + s, s - njx)
    x_blk = lambda s: jnp.minimum(s, njx - 1)
    z_blk = lambda s: jnp.maximum(s - njx, 0)
    w_dt = w_in[layer, :, cols:]
    seg_t = min(seg, tm)
    step = min(2 * SUB_TILE, tn)
    kern = functools.partial(_ssd_in_kernel, n_sub=n_sub, seg=seg_t, tps=tps, njx=njx)
    scratch = [pltpu.VMEM((tm, d), BF16)]
    if tps > 1:
        scratch.append(pltpu.VMEM((njx, SUBLANES, tn), F32))
    scratch += [pltpu.VMEM((n_sub * (seg_t + SUBLANES), step), F32) for _ in range(tn // step)]
    zs, xbc, nb, dt = pl.pallas_call(
        kern,
        grid=(rows // tm, nz + njx),
        in_specs=[
            pl.BlockSpec((tm, d), lambda i, s: (i, 0)),
            pl.BlockSpec((1, d), lambda i, s: (0, 0)),
            pl.BlockSpec((None, d, tn), lambda i, s: (layer, 0, w_blk(s))),
            pl.BlockSpec((width, tn), lambda i, s: (0, x_blk(s))),
            pl.BlockSpec((1, tn), lambda i, s: (0, x_blk(s))),
            pl.BlockSpec((n_sub, SUBLANES, tn), lambda i, s: (i // tps, 0, x_blk(s))),
            pl.BlockSpec((d, heads), lambda i, s: (0, 0)),
            pl.BlockSpec((1, heads), lambda i, s: (0, 0)),
        ],
        out_specs=[
            pl.BlockSpec((tm, tn), lambda i, s: (i, z_blk(s))),
            pl.BlockSpec((tm, tn), lambda i, s: (i, x_blk(s))),
            _nb_spec(n_sub, njx, tn, tps),
            pl.BlockSpec((tm, heads), lambda i, s: (i, 0)),
        ],
        out_shape=[jax.ShapeDtypeStruct((rows, inner), BF16), jax.ShapeDtypeStruct((rows, conv_dim), BF16),
                   _nb_shape(n_seq, njx, tn), jax.ShapeDtypeStruct((rows, heads), F32)],
        scratch_shapes=scratch,
        compiler_params=_params("arbitrary", "arbitrary"),
        name="ssd_in_proj",
    )(x, norm_w, w_in, conv_w, conv_b, st8, w_dt, dt_bias)
    return zs, xbc, dt, _nb_merge(nb)


def _split3(x):
    hi = x.astype(BF16).astype(F32)
    r = x - hi
    mid = r.astype(BF16).astype(F32)
    lo = (r - mid).astype(BF16).astype(F32)
    return hi, mid, lo


def _dot_f32_lhs(x, sel):
    hi, mid, lo = _split3(x)
    return _dot(hi.astype(BF16), sel) + _dot(mid.astype(BF16), sel) + _dot(lo.astype(BF16), sel)


def _dot_f32_rhs(sel, x):
    hi, mid, lo = _split3(x)
    return _dot(sel, hi.astype(BF16)) + _dot(sel, mid.astype(BF16)) + _dot(sel, lo.astype(BF16))


def _scan_kernel(zs_ref, xs_ref, b_ref, c_ref, dt3_ref, dtt_ref, a3_ref, at_ref, dsk_ref, nw_ref, st0_ref,
                 rep3_ref, tril_ref, tri2_ref, mask2_ref, bd_ref, y_ref, stout_ref, st_ref,
                 *, groups, hpg, q, n):
    c = pl.program_id(1)
    gw = hpg * q
    heads = groups * hpg

    @pl.when(c == 0)
    def _():
        for g in range(groups):
            st_ref[g] = st0_ref[0, g].T

    dt3 = dt3_ref[...]
    cs3 = _dot_f32_rhs(tril_ref[...], dt3 * a3_ref[...])
    tail3 = jnp.exp(cs3[q - 1:q] - cs3) * dt3
    hi, mid, lo = _split3(jnp.concatenate([cs3, tail3], axis=0))
    lane = lax.broadcasted_iota(jnp.int32, hi.shape, 1)
    pieces = jnp.where(lane < heads, hi, jnp.where(lane < 2 * heads, mid, lo)).astype(BF16)
    gsl = [slice(g * gw, (g + 1) * gw) for g in range(groups)]
    bcs = [_dot(pieces, rep3_ref[:, gsl[g]]) for g in range(groups)]
    cs_b = [bc[0:q] for bc in bcs]
    tail_b = [bc[q:2 * q] for bc in bcs]
    ecs_b = [jnp.exp(cs) for cs in cs_b]
    dtt = dtt_ref[0]
    cst = _dot_f32_lhs(dtt * at_ref[...], tri2_ref[...])
    mask2 = mask2_ref[...] > 0.0
    bd = bd_ref[...]

    for g0 in range(0, groups, GROUP_BATCH):
        gs = range(g0, min(g0 + GROUP_BATCH, groups))
        cgs = {g: c_ref[:, g * n:(g + 1) * n].astype(BF16) for g in gs}
        bgs = {g: b_ref[:, g * n:(g + 1) * n].astype(BF16) for g in gs}
        cb2s = {g: lax.dot_general(cgs[g], jnp.concatenate([bgs[g], bgs[g]], axis=0), (((1,), (1,)), ((), ())),
                                   preferred_element_type=F32) for g in gs}
        y_inter = {g: _dot(cgs[g], st_ref[g].astype(BF16)) for g in gs}
        ds = {g: lax.dot_general(bgs[g], (xs_ref[:, gsl[g]] * tail_b[g]).astype(BF16), (((0,), (0,)), ((), ())),
                                 preferred_element_type=F32) for g in gs}
        ms, xbds = [], []
        for g in gs:
            for k in range(hpg // 2):
                psl = slice(g * gw + 2 * k * q, g * gw + 2 * (k + 1) * q)
                pair = g * (hpg // 2) + k
                seg = cs_b[g][:, 2 * k * q:2 * (k + 1) * q] - jnp.broadcast_to(cst[pair:pair + 1], (q, 2 * q))
                w = jnp.where(mask2, jnp.exp(seg), 0.0) * jnp.broadcast_to(dtt[pair:pair + 1], (q, 2 * q))
                ms.append((cb2s[g] * w).astype(BF16))
                xk = xs_ref[:, psl].astype(BF16)
                xbds.append(jnp.concatenate([xk, xk], axis=0) * bd)
        parts = [_dot(m, xbd) for m, xbd in zip(ms, xbds)]
        for i, g in enumerate(gs):
            st_ref[g] = st_ref[g] * ecs_b[g][q - 1:q] + ds[g]
            pg = parts[i * (hpg // 2):(i + 1) * (hpg // 2)]
            y = pg[0] if len(pg) == 1 else jnp.concatenate(pg, axis=1)
            y = y + y_inter[g] * ecs_b[g]
            y = (y + dsk_ref[:, gsl[g]] * xs_ref[:, gsl[g]]) * zs_ref[:, gsl[g]]
            y = y * lax.rsqrt(jnp.mean(y * y, axis=-1, keepdims=True) + EPS)
            y_ref[:, gsl[g]] = (y * nw_ref[:, gsl[g]]).astype(BF16)

    @pl.when(c == pl.num_programs(1) - 1)
    def _():
        for g in range(groups):
            stout_ref[0, g] = st_ref[g].T


def _ssd_scan(zs, xbc, dt, st_all, layer, a_log, d_skip, norm_w, *, n_seq, seg, inner, groups, n, heads):
    rows = zs.shape[0]
    p = inner // heads
    q = p
    hpg = heads // groups
    assert seg % q == 0 and hpg % 2 == 0 and inner % (groups * n) == 0
    nc = seg // q
    gn = groups * n
    a = -jnp.exp(a_log.astype(F32))
    a3 = jnp.tile(a.reshape(1, heads), (1, 3))
    a_t = jnp.repeat(a.reshape(heads // 2, 2), q, axis=1)
    dt3 = jnp.tile(dt, (1, 3))
    dtt = dt.reshape(rows // q, q, heads // 2, 2).transpose(0, 2, 3, 1).reshape(rows // q, heads // 2, 2 * q)
    dsk = jnp.repeat(d_skip.astype(F32), p).reshape(1, inner)
    rep3 = jnp.tile(jnp.repeat(jnp.eye(heads, dtype=BF16), p, axis=1), (3, 1))
    tril = jnp.tril(jnp.ones((q, q), BF16))
    triu = jnp.triu(jnp.ones((q, q), F32))
    zero = jnp.zeros((q, q), F32)
    tri2 = jnp.block([[triu, zero], [zero, triu]]).astype(BF16)
    mask2 = jnp.concatenate([jnp.tril(jnp.ones((q, q), F32))] * 2, axis=1)
    one = jnp.ones((q, q), F32)
    bd = jnp.block([[one, zero], [zero, one]]).astype(BF16)

    kern = functools.partial(_scan_kernel, groups=groups, hpg=hpg, q=q, n=n)
    const = lambda s, c: (0, 0)
    operands = [zs, xbc, xbc, xbc, dt3, dtt, a3, a_t, dsk, norm_w, st_all, rep3, tril, tri2, mask2, bd]
    y, st = pl.pallas_call(
        kern,
        grid=(n_seq, nc),
        in_specs=[
            pl.BlockSpec((q, inner), lambda s, c: (s * nc + c, 0)),
            pl.BlockSpec((q, inner), lambda s, c: (s * nc + c, 0)),
            pl.BlockSpec((q, gn), lambda s, c: (s * nc + c, inner // gn)),
            pl.BlockSpec((q, gn), lambda s, c: (s * nc + c, inner // gn + 1)),
            pl.BlockSpec((q, 3 * heads), lambda s, c: (s * nc + c, 0)),
            pl.BlockSpec((1, heads // 2, 2 * q), lambda s, c: (s * nc + c, 0, 0)),
            pl.BlockSpec((1, 3 * heads), const),
            pl.BlockSpec((heads // 2, 2 * q), const),
            pl.BlockSpec((1, inner), const),
            pl.BlockSpec((1, inner), const),
            pl.BlockSpec((None, 1, groups, hpg * p, n), lambda s, c: (layer, s, 0, 0, 0)),
            pl.BlockSpec(rep3.shape, const),
            pl.BlockSpec(tril.shape, const),
            pl.BlockSpec(tri2.shape, const),
            pl.BlockSpec(mask2.shape, const),
            pl.BlockSpec(bd.shape, const),
        ],
        out_specs=[
            pl.BlockSpec((q, inner), lambda s, c: (s * nc + c, 0)),
            pl.BlockSpec((None, 1, groups, hpg * p, n), lambda s, c: (layer, s, 0, 0, 0)),
        ],
        out_shape=[jax.ShapeDtypeStruct((rows, inner), BF16),
                   jax.ShapeDtypeStruct(st_all.shape, F32)],
        scratch_shapes=[pltpu.VMEM((groups, n, hpg * p), F32)],
        input_output_aliases={next(i for i, o in enumerate(operands) if o is st_all): 1},
        compiler_params=_params("arbitrary", "arbitrary"),
        name="ssd_scan",
    )(*operands)
    return y, st


def _out_proj_kernel(x_ref, g_ref, w_ref, o_ref):
    o_ref[...] = x_ref[...] + _dot(g_ref[...], w_ref[...])


def _out_proj(x, g, w, layer, *, n_seq, seg):
    rows, d = x.shape
    k = g.shape[1]
    tm, _, _ = _row_tiling(n_seq, seg)
    return pl.pallas_call(
        _out_proj_kernel,
        grid=(rows // tm,),
        in_specs=[
            pl.BlockSpec((tm, d), lambda i: (i, 0)),
            pl.BlockSpec((tm, k), lambda i: (i, 0)),
            pl.BlockSpec((None, k, d), lambda i: (layer, 0, 0), pipeline_mode=pl.Buffered(1)),
        ],
        out_specs=pl.BlockSpec((tm, d), lambda i: (i, 0)),
        out_shape=jax.ShapeDtypeStruct((rows, d), F32),
        compiler_params=_params("arbitrary"),
        name="ssd_out_proj",
    )(x, g, w)


def _row_tiling(n_seq, seg, row_tile=ROW_TILE):
    if seg >= row_tile:
        assert seg % row_tile == 0
        return row_tile, 1, seg // row_tile
    n_sub = max(1, min(n_seq, row_tile // seg))
    while n_seq % n_sub:
        n_sub -= 1
    return n_sub * seg, n_sub, 1


def _residual_tiling(n_seq, seg, d):
    wide = seg >= WIDE_ROW_TILE
    tm, n_sub, tps = _row_tiling(n_seq, seg, WIDE_ROW_TILE if wide else ROW_TILE)
    mode = dict(pipeline_mode=pl.Buffered(1)) if wide else {}
    return tm, n_sub, tps, pl.BlockSpec((tm, d), lambda i, j: (i, 0), **mode)


def _col_tile(width, pref=COL_TILE):
    tn = pref
    while width % tn:
        tn //= 2
    assert tn >= 128
    return tn


def _nb_spec(n_sub, nj, tn, tps):
    return pl.BlockSpec((n_sub, nj, SUBLANES, tn), lambda i, j: (i // tps, 0, 0, 0))


def _nb_shape(n_seq, nj, tn):
    return jax.ShapeDtypeStruct((n_seq, nj, SUBLANES, tn), F32)


def _nb_merge(nb):
    n_seq, nj, rows, tn = nb.shape
    return nb.transpose(0, 2, 1, 3).reshape(n_seq, rows, nj * tn)


def _pad8(buf):
    return jnp.pad(buf, ((0, 0), (SUBLANES - buf.shape[1], 0), (0, 0)))


def _state_to_kernel(s, groups):
    b, h, p, n = s.shape
    return s.reshape(b, groups, (h // groups) * p, n)


def _state_from_kernel(s, heads):
    b, g, w, n = s.shape
    return s.reshape(b, heads, (g * w) // heads, n)


def _trunk(x, conv_a8, ssd_conv8, ssd_st, ffn8, wts, *, n_seq, seg):
    depth = wts["norm_mix"].shape[0]
    new_a, new_sc, new_f = [], [], []
    for i in range(depth):
        nm = wts["norm_mix"][i][None]
        j = i // 2
        if i % 2 == 0:
            x, nb = _sc_layer(x, conv_a8[j], nm, wts["sc_w_in"], wts["sc_conv_w"][j], wts["sc_w_out"], j,
                              n_seq=n_seq, seg=seg)
            new_a.append(nb)
        else:
            heads = wts["ssd_a_log"].shape[1]
            inner = wts["ssd_w_out"].shape[1]
            groups, n = ssd_st.shape[2], ssd_st.shape[4]
            zs, xbc, dt, nb = _ssd_in(x, ssd_conv8[j], nm, wts["ssd_w_in"], wts["ssd_conv_w"][j],
                                      wts["ssd_conv_b"][j][None], wts["ssd_dt_bias"][j][None], j,
                                      n_seq=n_seq, seg=seg, inner=inner, heads=heads)
            q = inner // heads
            pad = (-seg) % q
            if pad:
                padr = lambda t: jnp.pad(t.reshape(n_seq, seg, -1), ((0, 0), (0, pad), (0, 0))).reshape(
                    n_seq * (seg + pad), -1)
                zs, xbc, dt = padr(zs), padr(xbc), padr(dt)
            g, ssd_st = _ssd_scan(zs, xbc, dt, ssd_st, j, wts["ssd_a_log"][j], wts["ssd_d"][j],
                                  wts["ssd_norm_w"][j][None], n_seq=n_seq, seg=seg + pad, inner=inner,
                                  groups=groups, n=n, heads=heads)
            if pad:
                g = g.reshape(n_seq, seg + pad, -1)[:, :seg].reshape(n_seq * seg, -1)
            x = _out_proj(x, g, wts["ssd_w_out"], j, n_seq=n_seq, seg=seg)
            new_sc.append(nb)
        x, nf = _ffn_layer(x, ffn8[i], wts["norm_ffn"][i][None], wts["ffn_w_up"], wts["ffn_conv_w"][i],
                           wts["ffn_conv_b"][i][None], wts["ffn_w_down"], wts["norm_final"][None], i,
                           n_seq=n_seq, seg=seg, final_norm=(i == depth - 1))
        new_f.append(nf)
    return x, jnp.stack(new_a), jnp.stack(new_sc), ssd_st, jnp.stack(new_f)


def kernel(x_prompt, x_sample, state_conv_a, state_ssd_conv, state_ssd, state_ffn_conv, meta_tokens, norm_mix,
           norm_ffn, norm_final, sc_w_in, sc_conv_w, sc_w_out, ssd_w_in, ssd_conv_w, ssd_conv_b, ssd_dt_bias,
           ssd_a_log, ssd_d, ssd_norm_w, ssd_w_out, ffn_w_up, ffn_conv_w, ffn_conv_b, ffn_w_down):
    b, seq, d = x_prompt.shape
    sb, sseq, _ = x_sample.shape
    n_meta = meta_tokens.shape[0]
    heads = ssd_a_log.shape[1]
    inner = ssd_w_out.shape[1]
    n_state = state_ssd.shape[-1]
    groups = (ssd_conv_w.shape[2] - inner) // (2 * n_state)
    wts = dict(
        norm_mix=norm_mix, norm_ffn=norm_ffn, norm_final=norm_final,
        sc_w_in=sc_w_in.astype(BF16), sc_conv_w=sc_conv_w, sc_w_out=sc_w_out.astype(BF16),
        ssd_w_in=ssd_w_in.astype(BF16), ssd_conv_w=ssd_conv_w, ssd_conv_b=ssd_conv_b, ssd_dt_bias=ssd_dt_bias,
        ssd_a_log=ssd_a_log, ssd_d=ssd_d, ssd_norm_w=ssd_norm_w, ssd_w_out=ssd_w_out.astype(BF16),
        ffn_w_up=ffn_w_up.astype(BF16), ffn_conv_w=ffn_conv_w, ffn_conv_b=ffn_conv_b,
        ffn_w_down=ffn_w_down.astype(BF16))

    def zeros8(like, n_seq):
        return jnp.zeros((like.shape[0], n_seq, SUBLANES, like.shape[-1]), F32)

    zero_st = jnp.zeros((state_ssd.shape[0], 1, groups, inner // groups, n_state), F32)
    _, m_a, m_sc, m_st, m_f = _trunk(
        meta_tokens.astype(F32), zeros8(state_conv_a, 1), zeros8(state_ssd_conv, 1), zero_st,
        zeros8(state_ffn_conv, 1), wts, n_seq=1, seg=n_meta)
    rep = lambda t: jnp.broadcast_to(t, (t.shape[0], b) + t.shape[2:])
    yp, p_a, p_sc, p_st, p_f = _trunk(
        x_prompt.reshape(b * seq, d), rep(m_a), rep(m_sc), rep(m_st), rep(m_f), wts, n_seq=b, seg=seq)
    ys, s_a, s_sc, s_st, s_f = _trunk(
        x_sample.reshape(sb * sseq, d), jax.vmap(_pad8)(state_conv_a), jax.vmap(_pad8)(state_ssd_conv),
        jax.vmap(lambda s: _state_to_kernel(s, groups))(state_ssd.astype(F32)), jax.vmap(_pad8)(state_ffn_conv),
        wts, n_seq=sb, seg=sseq)

    tail = lambda t, like: t[:, :, SUBLANES - like.shape[2]:, :]
    unst = jax.vmap(lambda s: _state_from_kernel(s, heads))
    return (yp.reshape(b, seq, d), ys.reshape(sb, sseq, d),
            tail(p_a, state_conv_a), tail(p_sc, state_ssd_conv), unst(p_st), tail(p_f, state_ffn_conv),
            tail(s_a, state_conv_a), tail(s_sc, state_ssd_conv), unst(s_st), tail(s_f, state_ffn_conv))
```
